```python
import math
import jax, jax.numpy as jnp
from jax import lax
import numpy as np

D_MODEL = 1024
BATCH = 16
SEQ = 2048
DEPTH = 1
DEC_BATCH = 8
DEC_SEQ = 16
PAST_LEN = 2048

CHUNK = 64
N_META = 16
HEAD_DIM = 64
FOX_HEADS = 8
RWKV_HEADS = 8
FOX_WIDTH = FOX_HEADS * HEAD_DIM
RWKV_WIDTH = RWKV_HEADS * HEAD_DIM
Q_BLOCK = 128
DECAY_LORA = 64
AAA_LORA = 64
GATE_LORA = 128
N_EXPERTS = 32
TOP_K = 4
D_EXPERT = D_MODEL
SWIGLU_LIMIT = 7.0
SWIGLU_ALPHA = 1.702
MOE_MAX_BLOCK = 256
LN_EPS = 1e-5
GN_EPS = 64e-5
DN_ALPHA = (2 * DEPTH) ** 0.25
DN_BETA = (8 * DEPTH) ** -0.25

OFF_FQ = 0
OFF_FK = OFF_FQ + FOX_WIDTH
OFF_FV = OFF_FK + FOX_WIDTH
OFF_FF = OFF_FV + FOX_WIDTH
OFF_RR = OFF_FF + FOX_HEADS
OFF_RK = OFF_RR + RWKV_WIDTH
OFF_RV = OFF_RK + RWKV_WIDTH
OFF_GA = OFF_RV + RWKV_WIDTH
OFF_GB = OFF_GA + D_MODEL
IN_COLS = OFF_GB + D_MODEL

kernel_name = 'fox_rwkv7_moe_streaming_step'


def layer_norm(x, g, b):
    xf = x.astype(jnp.float32)
    mu = jnp.mean(xf, -1, keepdims=True)
    var = jnp.mean(jnp.square(xf - mu), -1, keepdims=True)
    return ((xf - mu) * lax.rsqrt(var + LN_EPS) * g + b).astype(x.dtype)


def fox_block(q, k, v, cq, ck, q0):
    tq, tk = q.shape[1], k.shape[1]
    s = jnp.einsum('bqhd,bkhd->bhqk', q, k).astype(jnp.float32) * (HEAD_DIM ** -0.5)
    s = s + cq[..., :, None] - ck[..., None, :]
    causal = (q0 + jnp.arange(tq))[:, None] >= jnp.arange(tk)[None, :]
    p = jax.nn.softmax(jnp.where(causal, s, -jnp.inf), axis=-1)
    return jnp.einsum('bhqk,bkhd->bqhd', p.astype(v.dtype), v)


def fox_prompt(q, k, v, logf):
    L = q.shape[1]
    Lp = -(-L // Q_BLOCK) * Q_BLOCK
    pad4 = ((0, 0), (0, Lp - L), (0, 0), (0, 0))
    q, k, v = jnp.pad(q, pad4), jnp.pad(k, pad4), jnp.pad(v, pad4)
    c = jnp.pad(jnp.cumsum(logf, axis=1), ((0, 0), (0, Lp - L), (0, 0))).transpose(0, 2, 1)
    outs = [fox_block(q[:, s:s + Q_BLOCK], k[:, :s + Q_BLOCK], v[:, :s + Q_BLOCK],
                      c[:, :, s:s + Q_BLOCK], c[:, :, :s + Q_BLOCK], s)
            for s in range(0, Lp, Q_BLOCK)]
    return jnp.concatenate(outs, axis=1)[:, :L]


def rwkv7_mixer(h, prev_row, rkv, rkv_prev_row, s0, mu_w, mu_a, mu_g, mu_rkv, w0, w1, w2,
                a0, a1, a2, g1, g2, k_k, k_a, r_k, gn_g, gn_b):
    B, T, _ = h.shape
    f32 = jnp.float32
    dx = jnp.concatenate([prev_row, h[:, :-1]], axis=1) - h
    xw, xa, xg = h + dx * mu_w, h + dx * mu_a, h + dx * mu_g
    rkv = rkv + (jnp.concatenate([rkv_prev_row, rkv[:, :-1]], axis=1) - rkv) * mu_rkv
    r, k, v = jnp.split(rkv, 3, axis=-1)
    w_log = -jax.nn.softplus(-(w0 + jnp.tanh(xw @ w1) @ w2)) - 0.5
    a = jax.nn.sigmoid(a0 + (xa @ a1) @ a2)
    g = jax.nn.sigmoid(xg @ g1) @ g2
    heads = lambda t: t.astype(f32).reshape(B, T, RWKV_HEADS, HEAD_DIM)
    kk = heads(k * k_k)
    kk = kk * lax.rsqrt(jnp.maximum(jnp.sum(kk * kk, -1, keepdims=True), 1e-24))
    k = k * (1 + (a - 1) * k_a)
    r_h, k_h, v_h, a_h = heads(r), heads(k), heads(v), heads(a)
    decay = jnp.exp(-jnp.exp(heads(w_log)))

    def step(S, inp):
        r_t, w_t, k_t, v_t, kk_t, b_t = inp
        sa = jnp.einsum('bhvk,bhk->bhv', S, -kk_t)
        S = S * w_t[:, :, None, :] + sa[..., None] * b_t[:, :, None, :] + v_t[..., None] * k_t[:, :, None, :]
        return S, jnp.einsum('bhvk,bhk->bhv', S, r_t)

    xs = tuple(jnp.moveaxis(t, 1, 0) for t in (r_h, decay, k_h, v_h, kk, kk * a_h))
    s_fin, out = lax.scan(step, s0.astype(f32), xs)
    out = jnp.moveaxis(out, 0, 1)
    mu = jnp.mean(out, -1, keepdims=True)
    var = jnp.mean(jnp.square(out - mu), -1, keepdims=True)
    out = ((out - mu) * lax.rsqrt(var + GN_EPS)).reshape(B, T, RWKV_WIDTH) * gn_g + gn_b
    bonus = jnp.sum(r_h * k_h * r_k, -1, keepdims=True) * v_h
    o = (out + bonus.reshape(B, T, RWKV_WIDTH)) * g
    return o.astype(h.dtype), s_fin


def routed_moe(h, w_router, b_router, w_e1, b_e1, w_e2, b_e2):
    B, T, D = h.shape
    x = h.reshape(-1, D)
    n_tok = x.shape[0]
    logits = (x @ w_router + b_router).astype(jnp.float32)
    top_val, top_idx = lax.top_k(logits, TOP_K)
    gate = jax.nn.softmax(top_val, axis=-1)
    n_asg = n_tok * TOP_K
    blk = min(MOE_MAX_BLOCK, max(8, 1 << int(math.log2(max(1, n_asg // N_EXPERTS)))))
    n_blocks = -(-n_asg // blk) + N_EXPERTS
    rows = n_blocks * blk
    flat_e = top_idx.reshape(-1)
    order = jnp.argsort(flat_e)
    se = flat_e[order]
    counts = jnp.bincount(flat_e, length=N_EXPERTS)
    starts = jnp.cumsum(counts) - counts
    padded = (counts + blk - 1) // blk * blk
    pends = jnp.cumsum(padded)
    dest = (pends - padded)[se] + jnp.arange(n_asg) - starts[se]
    slot_tok = jnp.full((rows,), n_tok, jnp.int32).at[dest].set((order // TOP_K).astype(jnp.int32))
    slot_w = jnp.zeros((rows,), jnp.float32).at[dest].set(gate.reshape(-1)[order])
    blk_e = jnp.minimum(jnp.searchsorted(pends, jnp.arange(n_blocks) * blk, side='right'), N_EXPERTS - 1)
    xb = jnp.concatenate([x, jnp.zeros((1, D), x.dtype)], axis=0)[slot_tok].reshape(n_blocks, blk, D)

    def expert_block(args):
        xe, e = args
        hc = xe @ w_e1[e] + b_e1[e]
        glu = jnp.minimum(hc[:, ::2], SWIGLU_LIMIT)
        lin = jnp.clip(hc[:, 1::2], -SWIGLU_LIMIT, SWIGLU_LIMIT)
        return (glu * jax.nn.sigmoid(SWIGLU_ALPHA * glu) * (lin + 1)) @ w_e2[e] + b_e2[e]

    yb = lax.map(expert_block, (xb, blk_e)).reshape(rows, D).astype(jnp.float32)
    y = jnp.zeros((n_tok + 1, D), jnp.float32).at[slot_tok].add(yb * slot_w[:, None])
    return y[:n_tok].reshape(B, T, D).astype(h.dtype)


def trunk_layer(h, prev_row, s0, past_k, past_v, past_logf, w_in, b_forget, mu_w, mu_a, mu_g, mu_rkv,
                w0, w1, w2, a0, a1, a2, g1, g2, k_k, k_a, r_k, gn_g, gn_b, w_up_a, w_up_b, w_out,
                ln1_g, ln1_b, w_router, b_router, w_e1, b_e1, w_e2, b_e2, ln2_g, ln2_b):
    B, T, _ = h.shape
    new_shift = h[:, -1:]
    proj = h @ w_in
    q = proj[..., OFF_FQ:OFF_FK].reshape(B, T, FOX_HEADS, HEAD_DIM)
    k = proj[..., OFF_FK:OFF_FV].reshape(B, T, FOX_HEADS, HEAD_DIM)
    v = proj[..., OFF_FV:OFF_FF].reshape(B, T, FOX_HEADS, HEAD_DIM)
    logf = jax.nn.log_sigmoid((proj[..., OFF_FF:OFF_RR] + b_forget).astype(jnp.float32))
    if past_k is None:
        fox = fox_prompt(q, k, v, logf)
    else:
        p = past_k.shape[1]
        c = jnp.cumsum(jnp.concatenate([past_logf.astype(jnp.float32), logf], axis=1), axis=1).transpose(0, 2, 1)
        fox = fox_block(q, jnp.concatenate([past_k, k], axis=1), jnp.concatenate([past_v, v], axis=1),
                        c[:, :, p:], c, p)
    rw, s_fin = rwkv7_mixer(h, prev_row, proj[..., OFF_RR:OFF_GA], prev_row @ w_in[:, OFF_RR:OFF_GA], s0,
                            mu_w, mu_a, mu_g, mu_rkv, w0, w1, w2, a0, a1, a2, g1, g2, k_k, k_a, r_k, gn_g, gn_b)
    merged = (jax.nn.sigmoid(proj[..., OFF_GA:OFF_GB]) * (fox.reshape(B, T, FOX_WIDTH) @ w_up_a)
              + jax.nn.sigmoid(proj[..., OFF_GB:IN_COLS]) * (rw @ w_up_b))
    h = layer_norm(DN_ALPHA * h + merged @ w_out, ln1_g, ln1_b)
    h = layer_norm(DN_ALPHA * h + routed_moe(h, w_router, b_router, w_e1, b_e1, w_e2, b_e2), ln2_g, ln2_b)
    return h, k, v, logf, s_fin, new_shift


def setup_inputs(seed: int = 0) -> dict:
    key = jax.random.key(seed)
    ks = iter(jax.random.split(key, 64))

    def nrm(shape, scale):
        return jax.random.normal(next(ks), shape, jnp.float32) * scale

    def uni(shape, lo, hi):
        return jax.random.uniform(next(ks), shape, jnp.float32, lo, hi)

    L, D, E, F = DEPTH, D_MODEL, N_EXPERTS, D_EXPERT
    sd = D ** -0.5
    w_in = jnp.concatenate([
        nrm((L, D, 2 * FOX_WIDTH), sd),
        nrm((L, D, FOX_WIDTH), DN_BETA * sd),
        nrm((L, D, FOX_HEADS), sd),
        nrm((L, D, 2 * RWKV_WIDTH), sd),
        nrm((L, D, RWKV_WIDTH), DN_BETA * sd),
        nrm((L, D, 2 * D), sd)], axis=-1)
    return {
        'x_prompt': nrm((BATCH, SEQ, D), 1.0),
        'x_sample': nrm((DEC_BATCH, DEC_SEQ, D), 1.0),
        'cache_fox_k': nrm((L, DEC_BATCH, PAST_LEN, FOX_HEADS, HEAD_DIM), 1.0),
        'cache_fox_v': nrm((L, DEC_BATCH, PAST_LEN, FOX_HEADS, HEAD_DIM), DN_BETA),
        'cache_fox_logf': jax.nn.log_sigmoid(2.5 + nrm((L, DEC_BATCH, PAST_LEN, FOX_HEADS), 1.0)),
        'state_rwkv': nrm((L, DEC_BATCH, RWKV_HEADS, HEAD_DIM, HEAD_DIM), 0.1),
        'state_shift': nrm((L, DEC_BATCH, 1, D), 1.0),
        'meta': nrm((N_META, D), 1.0),
        'ln0_g': 1.0 + nrm((D,), 0.02),
        'ln0_b': nrm((D,), 0.02),
        'w_in': w_in,
        'b_forget': jnp.linspace(1.0, 4.0, FOX_HEADS, dtype=jnp.float32) + nrm((L, FOX_HEADS), 0.1),
        'mu_w': uni((L, D), 0.0, 1.0),
        'mu_a': uni((L, D), 0.0, 1.0),
        'mu_g': uni((L, D), 0.0, 1.0),
        'mu_rkv': uni((L, 3 * RWKV_WIDTH), 0.0, 1.0),
        'w0': uni((L, RWKV_WIDTH), -6.0, 1.0),
        'w1': nrm((L, D, DECAY_LORA), sd),
        'w2': nrm((L, DECAY_LORA, RWKV_WIDTH), 0.5 * DECAY_LORA ** -0.5),
        'a0': nrm((L, RWKV_WIDTH), 0.1),
        'a1': nrm((L, D, AAA_LORA), sd),
        'a2': nrm((L, AAA_LORA, RWKV_WIDTH), 0.5 * AAA_LORA ** -0.5),
        'g1': nrm((L, D, GATE_LORA), sd),
        'g2': nrm((L, GATE_LORA, RWKV_WIDTH), GATE_LORA ** -0.5),
        'k_k': 0.85 + nrm((L, RWKV_WIDTH), 0.02),
        'k_a': 1.0 + nrm((L, RWKV_WIDTH), 0.02),
        'r_k': nrm((L, RWKV_HEADS, HEAD_DIM), 0.1),
        'gn_g': 1.0 + nrm((L, RWKV_WIDTH), 0.02),
        'gn_b': nrm((L, RWKV_WIDTH), 0.02),
        'w_up_a': nrm((L, FOX_WIDTH, D), DN_BETA * FOX_WIDTH ** -0.5),
        'w_up_b': nrm((L, RWKV_WIDTH, D), DN_BETA * RWKV_WIDTH ** -0.5),
        'w_out': nrm((L, D, D), DN_BETA * sd),
        'ln1_g': 1.0 + nrm((L, D), 0.02),
        'ln1_b': nrm((L, D), 0.02),
        'w_router': nrm((L, D, E), sd),
        'b_router': nrm((L, E), 0.01),
        'w_e1': nrm((L, E, D, 2 * F), DN_BETA * sd),
        'b_e1': nrm((L, E, 2 * F), 0.01),
        'w_e2': nrm((L, E, F, D), DN_BETA * F ** -0.5),
        'b_e2': nrm((L, E, D), 0.01),
        'ln2_g': 1.0 + nrm((L, D), 0.02),
        'ln2_b': nrm((L, D), 0.02),
    }


def reference(x_prompt, x_sample, cache_fox_k, cache_fox_v, cache_fox_logf, state_rwkv, state_shift,
              meta, ln0_g, ln0_b, w_in, b_forget, mu_w, mu_a, mu_g, mu_rkv, w0, w1, w2, a0, a1, a2,
              g1, g2, k_k, k_a, r_k, gn_g, gn_b, w_up_a, w_up_b, w_out, ln1_g, ln1_b,
              w_router, b_router, w_e1, b_e1, w_e2, b_e2, ln2_g, ln2_b):
    b = x_prompt.shape[0]
    meta_b = jnp.broadcast_to(meta, (b, N_META, D_MODEL)).astype(x_prompt.dtype)
    h_p = layer_norm(jnp.concatenate([meta_b, x_prompt], axis=1), ln0_g, ln0_b)
    h_s = layer_norm(x_sample, ln0_g, ln0_b)
    zero_row = jnp.zeros((b, 1, D_MODEL), h_p.dtype)
    zero_state = jnp.zeros((b, RWKV_HEADS, HEAD_DIM, HEAD_DIM), jnp.float32)
    st_p, st_s = [], []
    for l in range(DEPTH):
        lw = [t[l] for t in (w_in, b_forget, mu_w, mu_a, mu_g, mu_rkv, w0, w1, w2, a0, a1, a2, g1, g2,
                             k_k, k_a, r_k, gn_g, gn_b, w_up_a, w_up_b, w_out, ln1_g, ln1_b,
                             w_router, b_router, w_e1, b_e1, w_e2, b_e2, ln2_g, ln2_b)]
        h_p, *new_p = trunk_layer(h_p, zero_row, zero_state, None, None, None, *lw)
        h_s, *new_s = trunk_layer(h_s, state_shift[l], state_rwkv[l], cache_fox_k[l], cache_fox_v[l],
                                  cache_fox_logf[l], *lw)
        st_p.append(new_p)
        st_s.append(new_s)
    fox_k_p, fox_v_p, fox_logf_p, rwkv_p, shift_p = [jnp.stack(s, 0) for s in zip(*st_p)]
    fox_k_s, fox_v_s, fox_logf_s, rwkv_s, shift_s = [jnp.stack(s, 0) for s in zip(*st_s)]
    return (h_p[:, N_META:], h_s, fox_k_p, fox_v_p, fox_logf_p, rwkv_p, shift_p,
            fox_k_s, fox_v_s, fox_logf_s, rwkv_s, shift_s)
```

```python
import functools
import math

import jax
import jax.numpy as jnp
from jax import lax
from jax.experimental import pallas as pl
from jax.experimental.pallas import tpu as pltpu

F32 = jnp.float32
BF16 = jnp.bfloat16
I32 = jnp.int32

N_META = 16
HEAD_DIM = 64
N_HEADS = 8
HW = N_HEADS * HEAD_DIM
TOP_K = 4
SWIGLU_LIMIT = 7.0
SWIGLU_ALPHA = 1.702
LN_EPS = 1e-5
GN_EPS = 64e-5
LANES = 128
NEG_BIG = -1e30
VMEM_LIMIT_BYTES = 56 * 1024 * 1024
HIGHEST = lax.Precision.HIGHEST

NT_DIMS = (((1,), (1,)), ((), ()))
TN_DIMS = (((0,), (0,)), ((), ()))


def _params(*sem):
    return pltpu.CompilerParams(dimension_semantics=sem, vmem_limit_bytes=VMEM_LIMIT_BYTES)


def _largest_tile(n, cap, mult=8):
    best = None
    for d in range(mult, min(n, cap) + 1, mult):
        if n % d == 0:
            best = d
    assert best is not None, (n, cap, mult)
    return best


def _sigmoid(x):
    return 1.0 / (1.0 + jnp.exp(-x))


def _softplus(x):
    return jnp.maximum(x, 0.0) + jnp.log1p(jnp.exp(-jnp.abs(x)))


def _layer_norm(x, g, b):
    mu = jnp.mean(x, -1, keepdims=True)
    xc = x - mu
    var = jnp.mean(xc * xc, -1, keepdims=True)
    return xc * lax.rsqrt(var + LN_EPS) * g + b


def _dot(a, b):
    return jnp.dot(a, b, preferred_element_type=F32)


def _full(shape):
    n = len(shape)
    return pl.BlockSpec(shape, lambda *_: (0,) * n)


C_Q, C_K, C_V, C_RKV, C_GA, C_END = 0, HW, 2 * HW, 3 * HW, 6 * HW, 6 * HW + 2048


def _inproj_kernel(x_ref, prev_ref, g_ref, b_ref, wm_ref, wff_ref, bff_ref, mu_ref, w1_ref, a1_ref, g1_ref,
                   h_ref, q_ref, k_ref, v_ref, rkv_ref, gate_ref, lmid_ref, logf_ref, rkv0_ref,
                   carry_ref):
    t = pl.program_id(1)
    tt = x_ref.shape[0]
    h = _layer_norm(x_ref[...], g_ref[...], b_ref[...])
    h_ref[...] = h

    @pl.when(t == 0)
    def _():
        prev = prev_ref[...]
        carry_ref[...] = prev
        p8 = jnp.broadcast_to(prev, (8, prev.shape[1])).astype(BF16)
        rkv0_ref[...] = _dot(p8, wm_ref[:, C_RKV:C_GA])[0:1]

    rows = lax.broadcasted_iota(I32, h.shape, 0)
    hprev = jnp.where(rows == 0, carry_ref[...], pltpu.roll(h, 1, axis=0))
    carry_ref[...] = h[tt - 1:tt, :]
    dx = hprev - h
    hb = h.astype(BF16)
    q_ref[...] = _dot(hb, wm_ref[:, C_Q:C_K]).astype(BF16)
    k_ref[...] = _dot(hb, wm_ref[:, C_K:C_V])
    v_ref[...] = _dot(hb, wm_ref[:, C_V:C_RKV])
    rkv_ref[...] = _dot(hb, wm_ref[:, C_RKV:C_GA]).astype(BF16)
    gate_ref[...] = _sigmoid(_dot(hb, wm_ref[:, C_GA:C_END])).astype(BF16)
    ff = _dot(hb, wff_ref[...]) + bff_ref[...]
    logf_ref[...] = -_softplus(-ff)
    mu = mu_ref[...]
    lmid_ref[:, 0:64] = _dot((h + dx * mu[0:1]).astype(BF16), w1_ref[...])
    lmid_ref[:, 64:128] = _dot((h + dx * mu[1:2]).astype(BF16), a1_ref[...])
    lmid_ref[:, 128:256] = _dot((h + dx * mu[2:3]).astype(BF16), g1_ref[...])


def _inproj(x, prev_row, ln_g, ln_b, wm, wff, bff, mu3, w1, a1, g1):
    B, T, D = x.shape
    tt = _largest_tile(T, 384)
    nt = T // tt
    tile = lambda w: pl.BlockSpec((None, tt, w), lambda b, t: (b, t, 0))
    row = lambda w: pl.BlockSpec((None, 1, w), lambda b, t: (b, 0, 0))
    out_shape = [
        jax.ShapeDtypeStruct((B, T, D), F32),
        jax.ShapeDtypeStruct((B, T, HW), BF16),
        jax.ShapeDtypeStruct((B, T, HW), F32),
        jax.ShapeDtypeStruct((B, T, HW), F32),
        jax.ShapeDtypeStruct((B, T, 3 * HW), BF16),
        jax.ShapeDtypeStruct((B, T, 2 * D), BF16),
        jax.ShapeDtypeStruct((B, T, 256), F32),
        jax.ShapeDtypeStruct((B, T, LANES), F32),
        jax.ShapeDtypeStruct((B, 1, 3 * HW), F32),
    ]
    return pl.pallas_call(
        _inproj_kernel,
        grid=(B, nt),
        in_specs=[tile(D), row(D), _full((1, D)), _full((1, D)), _full(wm.shape), _full(wff.shape),
                  _full(bff.shape), _full(mu3.shape), _full(w1.shape), _full(a1.shape), _full(g1.shape)],
        out_specs=[tile(D), tile(HW), tile(HW), tile(HW), tile(3 * HW), tile(2 * D), tile(256), tile(LANES),
                   row(3 * HW)],
        out_shape=out_shape,
        scratch_shapes=[pltpu.VMEM((1, D), F32)],
        compiler_params=_params("arbitrary", "arbitrary"),
        name="inproj",
    )(x, prev_row, ln_g, ln_b, wm, wff, bff, mu3, w1, a1, g1)


def _cumsum_kernel(x_ref, c_ref, carry_ref):
    t = pl.program_id(1)
    tt = x_ref.shape[0]

    @pl.when(t == 0)
    def _():
        carry_ref[...] = jnp.zeros_like(carry_ref)

    r = lax.broadcasted_iota(I32, (tt, tt), 0)
    c = lax.broadcasted_iota(I32, (tt, tt), 1)
    tri = jnp.where(r >= c, 1.0, 0.0).astype(F32)
    cs = jnp.dot(tri, x_ref[...], preferred_element_type=F32, precision=HIGHEST) + carry_ref[...]
    c_ref[...] = cs
    carry_ref[...] = cs[tt - 1:tt, :]


def _cumsum_time(x):
    B, T, W = x.shape
    tt = _largest_tile(T, 384)
    spec = pl.BlockSpec((None, tt, W), lambda b, t: (b, t, 0))
    return pl.pallas_call(
        _cumsum_kernel,
        grid=(B, T // tt),
        in_specs=[spec],
        out_specs=spec,
        out_shape=jax.ShapeDtypeStruct((B, T, W), F32),
        scratch_shapes=[pltpu.VMEM((1, W), F32)],
        compiler_params=_params("arbitrary", "arbitrary"),
        name="logf_cumsum",
    )(x)


def _fox_kernel(q_ref, k_ref, v_ref, cq_ref, ck_ref, o_ref, m_ref, l_ref, acc_ref, *, q0, tq, tk, nk):
    qi = pl.program_id(1)
    ki = pl.program_id(2)

    @pl.when(ki == 0)
    def _():
        m_ref[...] = jnp.full_like(m_ref, NEG_BIG)
        l_ref[...] = jnp.zeros_like(l_ref)
        acc_ref[...] = jnp.zeros_like(acc_ref)

    first_q = q0 + qi * tq

    @pl.when(ki * tk <= first_q + tq - 1)
    def _():
        q = q_ref[...]
        k = k_ref[...].astype(BF16)
        v = v_ref[...].astype(BF16)
        cq = cq_ref[...]
        eye = jnp.where(lax.broadcasted_iota(I32, (8, LANES), 0) == lax.broadcasted_iota(I32, (8, LANES), 1),
                        1.0, 0.0).astype(F32)
        ck_t = lax.dot_general(eye, ck_ref[...], NT_DIMS, preferred_element_type=F32, precision=HIGHEST)
        rows = first_q + lax.broadcasted_iota(I32, (tq, tk), 0)
        cols = ki * tk + lax.broadcasted_iota(I32, (tq, tk), 1)
        causal = rows >= cols
        scale = HEAD_DIM ** -0.5
        for h in range(N_HEADS):
            hs = slice(h * HEAD_DIM, (h + 1) * HEAD_DIM)
            s = lax.dot_general(q[:, hs], k[:, hs], NT_DIMS, preferred_element_type=F32) * scale
            s = s + cq[:, h:h + 1] - ck_t[h:h + 1, :]
            s = jnp.where(causal, s, NEG_BIG)
            m_prev = m_ref[h][:, 0:1]
            m_new = jnp.maximum(m_prev, jnp.max(s, -1, keepdims=True))
            alpha = jnp.exp(m_prev - m_new)
            p = jnp.exp(s - m_new)
            l_new = alpha * l_ref[h][:, 0:1] + jnp.sum(p, -1, keepdims=True)
            acc_ref[:, hs] = alpha * acc_ref[:, hs] + _dot(p.astype(BF16), v[:, hs])
            m_ref[h] = jnp.broadcast_to(m_new, (tq, LANES))
            l_ref[h] = jnp.broadcast_to(l_new, (tq, LANES))

    @pl.when(ki == nk - 1)
    def _():
        for h in range(N_HEADS):
            hs = slice(h * HEAD_DIM, (h + 1) * HEAD_DIM)
            o_ref[:, hs] = (acc_ref[:, hs] / l_ref[h][:, 0:1]).astype(o_ref.dtype)


def _fox_attention(q, k, v, cq, ck, q0):
    B, Tq, _ = q.shape
    Tk = k.shape[1]
    tq = _largest_tile(Tq, 384)
    tk = _largest_tile(Tk, 384)
    nq, nk = Tq // tq, Tk // tk
    last_tile = lambda qi: jnp.minimum((q0 + (qi + 1) * tq - 1) // tk, nk - 1)
    qspec = lambda w: pl.BlockSpec((None, tq, w), lambda b, qi, ki: (b, qi, 0))
    kspec = lambda w: pl.BlockSpec((None, tk, w), lambda b, qi, ki: (b, jnp.minimum(ki, last_tile(qi)), 0))
    return pl.pallas_call(
        functools.partial(_fox_kernel, q0=q0, tq=tq, tk=tk, nk=nk),
        grid=(B, nq, nk),
        in_specs=[qspec(HW), kspec(HW), kspec(HW), qspec(LANES), kspec(LANES)],
        out_specs=qspec(HW),
        out_shape=jax.ShapeDtypeStruct((B, Tq, HW), BF16),
        scratch_shapes=[pltpu.VMEM((N_HEADS, tq, LANES), F32), pltpu.VMEM((N_HEADS, tq, LANES), F32),
                        pltpu.VMEM((tq, HW), F32)],
        compiler_params=_params("arbitrary", "arbitrary", "arbitrary"),
        name="fox_attention",
    )(q, k, v, cq, ck)


def _rwkv_kernel(rkv_ref, lmid_ref, rkv0_ref, s0_ref, mu_ref, w0_ref, w2_ref, a0_ref, a2_ref, g2_ref,
                 kk_ref, ka_ref, rk_ref, gng_ref, gnb_ref, o_ref, sfin_ref, state_ref, carry_ref, *, chunk, levels):
    C = chunk
    T = rkv_ref.shape[0]
    state_ref[...] = s0_ref[...]
    carry_ref[...] = rkv0_ref[...]
    row_w = lax.broadcasted_iota(I32, (C, 3 * HW), 0)
    row_h = lax.broadcasted_iota(I32, (C, HW), 0)
    r_i = lax.broadcasted_iota(I32, (C, C), 0)
    c_i = lax.broadcasted_iota(I32, (C, C), 1)
    strict = r_i > c_i
    incl = r_i >= c_i
    mid = C // 2 - 1 if C > 1 else 0

    def chunk_body(i, carry):
        off = pl.multiple_of(i * C, C)
        x = rkv_ref[pl.ds(off, C), :].astype(F32)
        prev = jnp.where(row_w == 0, carry_ref[...], pltpu.roll(x, 1, axis=0))
        carry_ref[...] = x[C - 1:C, :]
        x = x + (prev - x) * mu_ref[...]
        r, k0, v = x[:, 0:HW], x[:, HW:2 * HW], x[:, 2 * HW:3 * HW]
        lm = lmid_ref[pl.ds(off, C), :]
        w_pre = w0_ref[...] + _dot(jnp.tanh(lm[:, 0:64]).astype(BF16), w2_ref[...])
        a = _sigmoid(a0_ref[...] + _dot(lm[:, 64:128].astype(BF16), a2_ref[...]))
        g = _dot(_sigmoid(lm[:, 128:256]).astype(BF16), g2_ref[...])
        w_log = -_softplus(-w_pre) - 0.5
        logdec = -jnp.exp(w_log)
        L = logdec
        sh = 1
        while sh < C:
            L = L + jnp.where(row_h >= sh, pltpu.roll(L, sh, axis=0), 0.0)
            sh *= 2
        l_mid = L[mid:mid + 1, :]
        l_tot = L[C - 1:C, :]
        e_a = jnp.exp(L - logdec - l_mid)
        e_r = jnp.exp(L - l_mid)
        e_k = jnp.exp(l_mid - L)
        e_s = jnp.exp(l_tot - L)
        w_tot = jnp.exp(l_tot)
        e_mid = jnp.exp(l_mid)
        kk_raw = k0 * kk_ref[...]
        k = k0 * (1.0 + (a - 1.0) * ka_ref[...])
        rk = rk_ref[...]
        outs = []
        for h in range(N_HEADS):
            hs = slice(h * HEAD_DIM, (h + 1) * HEAD_DIM)
            kkh = kk_raw[:, hs]
            kkh = kkh * lax.rsqrt(jnp.maximum(jnp.sum(kkh * kkh, -1, keepdims=True), 1e-24))
            r_h, k_h, v_h = r[:, hs], k[:, hs], v[:, hs]
            b_h = kkh * a[:, hs]
            aq = -kkh * e_a[:, hs]
            rq = r_h * e_r[:, hs]
            bk = b_h * e_k[:, hs]
            kd = k_h * e_k[:, hs]
            ar = jnp.concatenate([aq, rq], axis=0)
            s_old = state_ref[h]
            g_b = lax.dot_general(ar, bk, NT_DIMS, preferred_element_type=F32)
            g_k = lax.dot_general(ar, kd, NT_DIMS, preferred_element_type=F32)
            x0 = lax.dot_general(ar, s_old * e_mid[:, hs], NT_DIMS, preferred_element_type=F32)
            a_mat = jnp.where(strict, g_b[0:C], 0.0)
            u = x0[0:C] + _dot(jnp.where(strict, g_k[0:C], 0.0), v_h)
            for lvl in range(levels):
                u = u + _dot(a_mat, u)
                if lvl + 1 < levels:
                    a_mat = _dot(a_mat, a_mat)
            y = x0[C:2 * C] + _dot(jnp.where(incl, g_b[C:2 * C], 0.0), u) + _dot(jnp.where(incl, g_k[C:2 * C], 0.0), v_h)
            s_new = (s_old * w_tot[:, hs]
                     + lax.dot_general(u, b_h * e_s[:, hs], TN_DIMS, preferred_element_type=F32)
                     + lax.dot_general(v_h, k_h * e_s[:, hs], TN_DIMS, preferred_element_type=F32))
            state_ref[h] = s_new
            mu = jnp.mean(y, -1, keepdims=True)
            yc = y - mu
            var = jnp.mean(yc * yc, -1, keepdims=True)
            bonus = jnp.sum(r_h * k_h * rk[:, hs], -1, keepdims=True) * v_h
            outs.append((yc * lax.rsqrt(var + GN_EPS), bonus))
        yn = jnp.concatenate([o[0] for o in outs], axis=1)
        bonus = jnp.concatenate([o[1] for o in outs], axis=1)
        out = (yn * gng_ref[...] + gnb_ref[...] + bonus) * g
        o_ref[pl.ds(off, C), :] = out.astype(o_ref.dtype)
        return carry

    lax.fori_loop(0, T // C, chunk_body, 0)
    sfin_ref[...] = state_ref[...]


def _rwkv(rkv, lmid, rkv0, s0, mu_rkv, w0, w2, a0, a2, g2, k_k, k_a, r_k, gn_g, gn_b):
    B, T, _ = rkv.shape
    chunk = _largest_tile(T, 64, mult=16)
    levels = max(1, math.ceil(math.log2(chunk)))
    seq = lambda w: pl.BlockSpec((None, T, w), lambda b: (b, 0, 0))
    st = pl.BlockSpec((None, N_HEADS, HEAD_DIM, HEAD_DIM), lambda b: (b, 0, 0, 0))
    vec = lambda a: _full(a.shape)
    params = (mu_rkv, w0, w2, a0, a2, g2, k_k, k_a, r_k, gn_g, gn_b)
    return pl.pallas_call(
        functools.partial(_rwkv_kernel, chunk=chunk, levels=levels),
        grid=(B,),
        in_specs=[seq(3 * HW), seq(256), pl.BlockSpec((None, 1, 3 * HW), lambda b: (b, 0, 0)), st]
                 + [vec(p) for p in params],
        out_specs=[seq(HW), st],
        out_shape=[jax.ShapeDtypeStruct((B, T, HW), BF16),
                   jax.ShapeDtypeStruct((B, N_HEADS, HEAD_DIM, HEAD_DIM), F32)],
        scratch_shapes=[pltpu.VMEM((N_HEADS, HEAD_DIM, HEAD_DIM), F32), pltpu.VMEM((1, 3 * HW), F32)],
        compiler_params=_params("arbitrary"),
        name="rwkv7",
    )(rkv, lmid, rkv0, s0, *params)


def _merge_kernel(fox_ref, rw_ref, gate_ref, h_ref, wa_ref, wb_ref, wo_ref, g_ref, b_ref, wrh_ref, wrl_ref, br_ref,
                  h1_ref, idx_ref, gt_ref, rank_ref, cnt_ref, carry_ref, *, dn_alpha):
    i = pl.program_id(0)
    tm, D = h_ref.shape

    @pl.when(i == 0)
    def _():
        carry_ref[...] = jnp.zeros_like(carry_ref)

    gates = gate_ref[...].astype(F32)
    merged = gates[:, 0:D] * _dot(fox_ref[...], wa_ref[...]) + gates[:, D:2 * D] * _dot(rw_ref[...], wb_ref[...])
    z = dn_alpha * h_ref[...] + _dot(merged.astype(BF16), wo_ref[...])
    h1 = _layer_norm(z, g_ref[...], b_ref[...])
    h1_ref[...] = h1
    hi = h1.astype(BF16)
    lo = (h1 - hi.astype(F32)).astype(BF16)
    logits = _dot(hi, wrh_ref[...]) + _dot(hi, wrl_ref[...]) + _dot(lo, wrh_ref[...]) + br_ref[...]
    lane = lax.broadcasted_iota(I32, (tm, LANES), 1)
    lane_f = lane.astype(F32)
    cur = logits
    vals, idxs = [], []
    for _ in range(TOP_K):
        m = jnp.max(cur, -1, keepdims=True)
        ix = jnp.min(jnp.where(cur == m, lane_f, float(LANES)), -1, keepdims=True)
        vals.append(m)
        idxs.append(ix)
        cur = jnp.where(lane_f == ix, -3e38, cur)
    exps = [jnp.exp(vk - vals[0]) for vk in vals]
    denom = exps[0] + exps[1] + exps[2] + exps[3]
    onehot = jnp.zeros((tm, LANES), F32)
    for ix in idxs:
        onehot = onehot + jnp.where(lane_f == ix, 1.0, 0.0)
    r_i = lax.broadcasted_iota(I32, (tm, tm), 0)
    c_i = lax.broadcasted_iota(I32, (tm, tm), 1)
    tri = jnp.where(r_i > c_i, 1.0, 0.0).astype(BF16)
    before = _dot(tri, onehot.astype(BF16)) + carry_ref[...]
    idx_out = jnp.zeros((tm, LANES), F32)
    gt_out = jnp.zeros((tm, LANES), F32)
    rank_out = jnp.zeros((tm, LANES), F32)
    for kx in range(TOP_K):
        rank_k = jnp.sum(jnp.where(lane_f == idxs[kx], before, 0.0), -1, keepdims=True)
        idx_out = jnp.where(lane == kx, idxs[kx], idx_out)
        gt_out = jnp.where(lane == kx, exps[kx] / denom, gt_out)
        rank_out = jnp.where(lane == kx, rank_k, rank_out)
    idx_ref[...] = idx_out.astype(I32)
    gt_ref[...] = gt_out
    rank_ref[...] = rank_out.astype(I32)
    total = carry_ref[...] + jnp.sum(onehot, 0, keepdims=True)
    carry_ref[...] = total
    cnt_ref[...] = total.astype(I32)


def _merge_route(fox, rw, gates, h, wa, wb, wo, ln_g, ln_b, wr_hi, wr_lo, br, dn_alpha):
    N, D = h.shape
    tm = _largest_tile(N, 256)
    tile = lambda w: pl.BlockSpec((tm, w), lambda i: (i, 0))
    return pl.pallas_call(
        functools.partial(_merge_kernel, dn_alpha=dn_alpha),
        grid=(N // tm,),
        in_specs=[tile(HW), tile(HW), tile(2 * D), tile(D), _full(wa.shape), _full(wb.shape), _full(wo.shape),
                  _full((1, D)), _full((1, D)), _full(wr_hi.shape), _full(wr_lo.shape), _full(br.shape)],
        out_specs=[tile(D), tile(LANES), tile(LANES), tile(LANES), _full((1, LANES))],
        out_shape=[jax.ShapeDtypeStruct((N, D), F32), jax.ShapeDtypeStruct((N, LANES), I32),
                   jax.ShapeDtypeStruct((N, LANES), F32), jax.ShapeDtypeStruct((N, LANES), I32),
                   jax.ShapeDtypeStruct((1, LANES), I32)],
        scratch_shapes=[pltpu.VMEM((1, LANES), F32)],
        compiler_params=_params("arbitrary"),
        name="merge_route",
    )(fox, rw, gates, h, wa, wb, wo, ln_g, ln_b, wr_hi, wr_lo, br)


def _moe_gather_start(tok_ref, x_hbm, buf, sem, slot, bm):
    def body(r, c):
        tok = tok_ref[0, 0, r]
        pltpu.make_async_copy(x_hbm.at[pl.ds(tok, 1)], buf.at[slot, pl.ds(r, 1)], sem.at[slot]).start()
        return c
    lax.fori_loop(0, bm, body, 0, unroll=8)


def _moe_kernel(be_ref, nused_ref, tok_ref, tokn_ref, x_hbm, w1g_ref, w1l_ref, b1g_ref, b1l_ref, w2_ref, b2_ref,
                y_ref, buf, sem):
    j = pl.program_id(0)
    nb = pl.num_programs(0)
    bm = buf.shape[1]
    nused = nused_ref[0]
    slot = j % 2

    @pl.when(j == 0)
    def _():
        _moe_gather_start(tok_ref, x_hbm, buf, sem, 0, bm)

    @pl.when(jnp.logical_and(j + 1 < nb, j + 1 < nused))
    def _():
        _moe_gather_start(tokn_ref, x_hbm, buf, sem, 1 - slot, bm)

    @pl.when(jnp.logical_or(j < nused, j == 0))
    def _():
        pltpu.make_async_copy(x_hbm.at[pl.ds(0, bm)], buf.at[slot], sem.at[slot]).wait()

    @pl.when(j < nused)
    def _():
        x = buf[slot].astype(BF16)
        glu = jnp.minimum(_dot(x, w1g_ref[...]) + b1g_ref[...], SWIGLU_LIMIT)
        lin = jnp.clip(_dot(x, w1l_ref[...]) + b1l_ref[...], -SWIGLU_LIMIT, SWIGLU_LIMIT)
        act = glu * _sigmoid(SWIGLU_ALPHA * glu) * (lin + 1.0)
        y_ref[...] = _dot(act.astype(BF16), w2_ref[...]) + b2_ref[...]

    @pl.when(j >= nused)
    def _():
        y_ref[...] = jnp.zeros_like(y_ref)


def _moe_experts(x, slot_tok, blk_e, nused, w1g, w1l, b1g, b1l, w2, b2, bm):
    N, D = x.shape
    nb = slot_tok.shape[0]
    F = w1g.shape[2]
    last = lambda j, be, nu: jnp.minimum(j, jnp.maximum(nu[0] - 1, 0))
    wspec = lambda k, n: pl.BlockSpec((None, k, n), lambda j, be, nu: (be[last(j, be, nu)], 0, 0))
    grid_spec = pltpu.PrefetchScalarGridSpec(
        num_scalar_prefetch=2,
        grid=(nb,),
        in_specs=[
            pl.BlockSpec((1, 1, bm), lambda j, be, nu: (j, 0, 0), memory_space=pltpu.SMEM),
            pl.BlockSpec((1, 1, bm), lambda j, be, nu: (jnp.minimum(j + 1, nb - 1), 0, 0), memory_space=pltpu.SMEM),
            pl.BlockSpec(memory_space=pl.ANY),
            wspec(D, F), wspec(D, F), wspec(1, F), wspec(1, F), wspec(F, D), wspec(1, D),
        ],
        out_specs=pl.BlockSpec((bm, D), lambda j, be, nu: (j, 0)),
        scratch_shapes=[pltpu.VMEM((2, bm, D), F32), pltpu.SemaphoreType.DMA((2,))],
    )
    return pl.pallas_call(
        _moe_kernel,
        grid_spec=grid_spec,
        out_shape=jax.ShapeDtypeStruct((nb * bm, D), F32),
        compiler_params=_params("arbitrary"),
        name="moe_experts",
    )(blk_e, nused, slot_tok, slot_tok, x, w1g, w1l, b1g, b1l, w2, b2)


def _combine_gather_start(dest_ref, y_hbm, buf, sem, slot, tm):
    def body(r, c):
        for kx in range(TOP_K):
            d = dest_ref[0, 0, r * TOP_K + kx]
            pltpu.make_async_copy(y_hbm.at[pl.ds(d, 1)], buf.at[slot, kx, pl.ds(r, 1)], sem.at[slot]).start()
        return c
    lax.fori_loop(0, tm, body, 0, unroll=4)


def _combine_kernel(dest_ref, destn_ref, gt_ref, h1_ref, g_ref, b_ref, y_hbm, o_ref, buf, sem, *, dn_alpha):
    i = pl.program_id(0)
    n = pl.num_programs(0)
    tm = h1_ref.shape[0]
    slot = i % 2

    @pl.when(i == 0)
    def _():
        _combine_gather_start(dest_ref, y_hbm, buf, sem, 0, tm)

    @pl.when(i + 1 < n)
    def _():
        _combine_gather_start(destn_ref, y_hbm, buf, sem, 1 - slot, tm)

    for kx in range(TOP_K):
        pltpu.make_async_copy(y_hbm.at[pl.ds(0, tm)], buf.at[slot, kx], sem.at[slot]).wait()
    gt = gt_ref[...]
    moe = gt[:, 0:1] * buf[slot, 0]
    for kx in range(1, TOP_K):
        moe = moe + gt[:, kx:kx + 1] * buf[slot, kx]
    o_ref[...] = _layer_norm(dn_alpha * h1_ref[...] + moe, g_ref[...], b_ref[...])


def _moe_combine(dest, gate, h1, ln_g, ln_b, yb, dn_alpha):
    N, D = h1.shape
    tm = _largest_tile(N, 256)
    n = N // tm
    dest3 = dest.reshape(n, 1, tm * TOP_K)
    tile = lambda w: pl.BlockSpec((tm, w), lambda i: (i, 0))
    return pl.pallas_call(
        functools.partial(_combine_kernel, dn_alpha=dn_alpha),
        grid=(n,),
        in_specs=[
            pl.BlockSpec((1, 1, tm * TOP_K), lambda i: (i, 0, 0), memory_space=pltpu.SMEM),
            pl.BlockSpec((1, 1, tm * TOP_K), lambda i: (jnp.minimum(i + 1, n - 1), 0, 0), memory_space=pltpu.SMEM),
            tile(LANES), tile(D), _full((1, D)), _full((1, D)),
            pl.BlockSpec(memory_space=pl.ANY),
        ],
        out_specs=tile(D),
        out_shape=jax.ShapeDtypeStruct((N, D), F32),
        scratch_shapes=[pltpu.VMEM((2, TOP_K, tm, D), F32), pltpu.SemaphoreType.DMA((2,))],
        compiler_params=_params("arbitrary"),
        name="moe_combine",
    )(dest3, dest3, gate, h1, ln_g, ln_b, yb)


def _route_tables(top_idx, rank, counts, n_experts):
    n_tok = top_idx.shape[0]
    n_asg = n_tok * TOP_K
    bm = min(256, max(8, 1 << int(math.log2(max(1, n_asg // n_experts)))))
    nb = -(-n_asg // bm) + n_experts
    padded = (counts + bm - 1) // bm * bm
    pends = jnp.cumsum(padded)
    dest = (pends - padded)[top_idx] + rank
    tok = jnp.broadcast_to(jnp.arange(n_tok, dtype=I32)[:, None], dest.shape)
    slot_tok = jnp.zeros((nb * bm,), I32).at[dest.reshape(-1)].set(tok.reshape(-1))
    blk_e = jnp.minimum(jnp.searchsorted(pends, jnp.arange(nb, dtype=I32) * bm, side='right'),
                        n_experts - 1).astype(I32)
    nused = (pends[-1] // bm).astype(I32).reshape(1)
    return dest.astype(I32), slot_tok.reshape(nb, 1, bm), blk_e, nused, bm


def _stream(x, prev_row, s0, past_k, past_v, past_logf, wts):
    B, T, D = x.shape
    dn_alpha = wts['dn_alpha']
    h, q, k, v, rkv, gates, lmid, logf, rkv0 = _inproj(
        x, prev_row, wts['ln0_g'], wts['ln0_b'], wts['wm'], wts['wff'], wts['bff'], wts['mu3'],
        wts['w1'], wts['a1'], wts['g1'])
    if past_k is None:
        c = _cumsum_time(logf)
        fox = _fox_attention(q, k, v, c, c, 0)
    else:
        P = past_k.shape[1]
        past_pad = jnp.pad(past_logf.astype(F32), ((0, 0), (0, 0), (0, LANES - N_HEADS)))
        c = _cumsum_time(jnp.concatenate([past_pad, logf], axis=1))
        k_all = jnp.concatenate([past_k.reshape(B, P, HW), k], axis=1)
        v_all = jnp.concatenate([past_v.reshape(B, P, HW), v], axis=1)
        fox = _fox_attention(q, k_all, v_all, c[:, P:], c, P)
    rw, s_fin = _rwkv(rkv, lmid, rkv0, s0, wts['mu_rkv'], wts['w0'], wts['w2'], wts['a0'], wts['a2'], wts['g2'],
                      wts['k_k'], wts['k_a'], wts['r_k'], wts['gn_g'], wts['gn_b'])
    N = B * T
    h1, top_idx, gate, rank, counts = _merge_route(
        fox.reshape(N, HW), rw.reshape(N, HW), gates.reshape(N, 2 * D), h.reshape(N, D),
        wts['w_up_a'], wts['w_up_b'], wts['w_out'], wts['ln1_g'], wts['ln1_b'],
        wts['wr_hi'], wts['wr_lo'], wts['br'], dn_alpha)
    n_experts = wts['n_experts']
    dest, slot_tok, blk_e, nused, bm = _route_tables(top_idx[:, :TOP_K], rank[:, :TOP_K], counts[0, :n_experts],
                                                     n_experts)
    yb = _moe_experts(h1, slot_tok, blk_e, nused, wts['w1g'], wts['w1l'], wts['b1g'], wts['b1l'],
                      wts['we2'], wts['be2'], bm)
    y = _moe_combine(dest, gate, h1, wts['ln2_g'], wts['ln2_b'], yb, dn_alpha)
    new_k = k.reshape(B, T, N_HEADS, HEAD_DIM)
    new_v = v.reshape(B, T, N_HEADS, HEAD_DIM)
    return y.reshape(B, T, D), new_k, new_v, logf[:, :, :N_HEADS], s_fin, h[:, T - 1:T, :]


def kernel(x_prompt, x_sample, cache_fox_k, cache_fox_v, cache_fox_logf, state_rwkv, state_shift, meta, ln0_g, ln0_b, w_in, b_forget, mu_w, mu_a, mu_g, mu_rkv, w0, w1, w2, a0, a1, a2, g1, g2, k_k, k_a, r_k, gn_g, gn_b, w_up_a, w_up_b, w_out, ln1_g, ln1_b, w_router, b_router, w_e1, b_e1, w_e2, b_e2, ln2_g, ln2_b):
    depth, D, in_cols = w_in.shape
    assert depth == 1 and D == 1024 and in_cols == 6 * HW + N_HEADS + 2 * D
    n_experts = w_router.shape[2]
    assert n_experts <= LANES
    B = x_prompt.shape[0]
    l = 0
    w = w_in[l]
    off_ff = 3 * HW
    row = lambda a: a.reshape(1, -1).astype(F32)
    wr = jnp.pad(w_router[l], ((0, 0), (0, LANES - n_experts)))
    wr_hi = wr.astype(BF16)
    wts = dict(
        dn_alpha=float((2 * depth) ** 0.25), n_experts=n_experts,
        ln0_g=row(ln0_g), ln0_b=row(ln0_b),
        wm=jnp.concatenate([w[:, :off_ff], w[:, off_ff + N_HEADS:]], axis=1).astype(BF16),
        wff=jnp.pad(w[:, off_ff:off_ff + N_HEADS], ((0, 0), (0, LANES - N_HEADS))).astype(BF16),
        bff=jnp.pad(row(b_forget[l]), ((0, 0), (0, LANES - N_HEADS))),
        mu3=jnp.stack([mu_w[l], mu_a[l], mu_g[l]], axis=0),
        w1=w1[l].astype(BF16), a1=a1[l].astype(BF16), g1=g1[l].astype(BF16),
        mu_rkv=row(mu_rkv[l]), w0=row(w0[l]), w2=w2[l].astype(BF16), a0=row(a0[l]), a2=a2[l].astype(BF16),
        g2=g2[l].astype(BF16), k_k=row(k_k[l]), k_a=row(k_a[l]), r_k=row(r_k[l]), gn_g=row(gn_g[l]),
        gn_b=row(gn_b[l]),
        w_up_a=w_up_a[l].astype(BF16), w_up_b=w_up_b[l].astype(BF16), w_out=w_out[l].astype(BF16),
        ln1_g=row(ln1_g[l]), ln1_b=row(ln1_b[l]),
        wr_hi=wr_hi, wr_lo=(wr - wr_hi.astype(F32)).astype(BF16),
        br=jnp.pad(row(b_router[l]), ((0, 0), (0, LANES - n_experts)), constant_values=NEG_BIG),
        w1g=w_e1[l][:, :, 0::2].astype(BF16), w1l=w_e1[l][:, :, 1::2].astype(BF16),
        b1g=b_e1[l][:, None, 0::2], b1l=b_e1[l][:, None, 1::2],
        we2=w_e2[l].astype(BF16), be2=b_e2[l][:, None, :],
        ln2_g=row(ln2_g[l]), ln2_b=row(ln2_b[l]),
    )
    meta_b = jnp.broadcast_to(meta, (B, N_META, D)).astype(x_prompt.dtype)
    xp = jnp.concatenate([meta_b, x_prompt], axis=1)
    zero_row = jnp.zeros((B, 1, D), F32)
    zero_state = jnp.zeros((B, N_HEADS, HEAD_DIM, HEAD_DIM), F32)
    y_p, k_p, v_p, lf_p, s_p, sh_p = _stream(xp, zero_row, zero_state, None, None, None, wts)
    y_s, k_s, v_s, lf_s, s_s, sh_s = _stream(x_sample, state_shift[l], state_rwkv[l], cache_fox_k[l],
                                             cache_fox_v[l], cache_fox_logf[l], wts)
    ex = lambda a: a[None]
    return (y_p[:, N_META:], y_s, ex(k_p), ex(v_p), ex(lf_p), ex(s_p), ex(sh_p),
            ex(k_s), ex(v_s), ex(lf_s), ex(s_s), ex(sh_s))
```

```python
import functools
import math

import jax
import jax.numpy as jnp
from jax import lax
from jax.experimental import pallas as pl
from jax.experimental.pallas import tpu as pltpu

F32 = jnp.float32
BF16 = jnp.bfloat16
I32 = jnp.int32

N_META = 16
HEAD_DIM = 64
N_HEADS = 8
HW = N_HEADS * HEAD_DIM
TOP_K = 4
SWIGLU_LIMIT = 7.0
SWIGLU_ALPHA = 1.702
LN_EPS = 1e-5
GN_EPS = 64e-5
LANES = 128
NEG_BIG = -1e30
VMEM_LIMIT_BYTES = 56 * 1024 * 1024
HIGHEST = lax.Precision.HIGHEST

NT_DIMS = (((1,), (1,)), ((), ()))
TN_DIMS = (((0,), (0,)), ((), ()))


def _params(*sem):
    return pltpu.CompilerParams(dimension_semantics=sem, vmem_limit_bytes=VMEM_LIMIT_BYTES)


def _largest_tile(n, cap, mult=8):
    best = None
    for d in range(mult, min(n, cap) + 1, mult):
        if n % d == 0:
            best = d
    assert best is not None, (n, cap, mult)
    return best


def _sigmoid(x):
    return 1.0 / (1.0 + jnp.exp(-x))


def _softplus(x):
    return jnp.maximum(x, 0.0) + jnp.log1p(jnp.exp(-jnp.abs(x)))


def _layer_norm(x, g, b):
    mu = jnp.mean(x, -1, keepdims=True)
    xc = x - mu
    var = jnp.mean(xc * xc, -1, keepdims=True)
    return xc * lax.rsqrt(var + LN_EPS) * g + b


def _dot(a, b):
    return jnp.dot(a, b, preferred_element_type=F32)


def _full(shape):
    n = len(shape)
    return pl.BlockSpec(shape, lambda *_: (0,) * n)


C_Q, C_K, C_V, C_RKV, C_GA, C_END = 0, HW, 2 * HW, 3 * HW, 6 * HW, 6 * HW + 2048


def _inproj_kernel(x_ref, prev_ref, g_ref, b_ref, wm_ref, wff_ref, bff_ref, mu_ref, w1_ref, a1_ref, g1_ref,
                   h_ref, q_ref, k_ref, v_ref, rkv_ref, gate_ref, lmid_ref, logf_ref, rkv0_ref,
                   carry_ref):
    t = pl.program_id(1)
    tt = x_ref.shape[0]
    h = _layer_norm(x_ref[...], g_ref[...], b_ref[...])
    h_ref[...] = h

    @pl.when(t == 0)
    def _():
        prev = prev_ref[...]
        carry_ref[...] = prev
        p8 = jnp.broadcast_to(prev, (8, prev.shape[1])).astype(BF16)
        rkv0_ref[...] = _dot(p8, wm_ref[:, C_RKV:C_GA])[0:1]

    rows = lax.broadcasted_iota(I32, h.shape, 0)
    hprev = jnp.where(rows == 0, carry_ref[...], pltpu.roll(h, 1, axis=0))
    carry_ref[...] = h[tt - 1:tt, :]
    dx = hprev - h
    hb = h.astype(BF16)
    q_ref[...] = _dot(hb, wm_ref[:, C_Q:C_K]).astype(BF16)
    k_ref[...] = _dot(hb, wm_ref[:, C_K:C_V])
    v_ref[...] = _dot(hb, wm_ref[:, C_V:C_RKV])
    rkv_ref[...] = _dot(hb, wm_ref[:, C_RKV:C_GA]).astype(BF16)
    gate_ref[...] = _sigmoid(_dot(hb, wm_ref[:, C_GA:C_END])).astype(BF16)
    ff = _dot(hb, wff_ref[...]) + bff_ref[...]
    logf_ref[...] = -_softplus(-ff)
    mu = mu_ref[...]
    lmid_ref[:, 0:64] = _dot((h + dx * mu[0:1]).astype(BF16), w1_ref[...])
    lmid_ref[:, 64:128] = _dot((h + dx * mu[1:2]).astype(BF16), a1_ref[...])
    lmid_ref[:, 128:256] = _dot((h + dx * mu[2:3]).astype(BF16), g1_ref[...])


def _inproj(x, prev_row, ln_g, ln_b, wm, wff, bff, mu3, w1, a1, g1):
    B, T, D = x.shape
    tt = _largest_tile(T, 384)
    nt = T // tt
    tile = lambda w: pl.BlockSpec((None, tt, w), lambda b, t: (b, t, 0))
    row = lambda w: pl.BlockSpec((None, 1, w), lambda b, t: (b, 0, 0))
    out_shape = [
        jax.ShapeDtypeStruct((B, T, D), F32),
        jax.ShapeDtypeStruct((B, T, HW), BF16),
        jax.ShapeDtypeStruct((B, T, HW), F32),
        jax.ShapeDtypeStruct((B, T, HW), F32),
        jax.ShapeDtypeStruct((B, T, 3 * HW), BF16),
        jax.ShapeDtypeStruct((B, T, 2 * D), BF16),
        jax.ShapeDtypeStruct((B, T, 256), F32),
        jax.ShapeDtypeStruct((B, T, LANES), F32),
        jax.ShapeDtypeStruct((B, 1, 3 * HW), F32),
    ]
    return pl.pallas_call(
        _inproj_kernel,
        grid=(B, nt),
        in_specs=[tile(D), row(D), _full((1, D)), _full((1, D)), _full(wm.shape), _full(wff.shape),
                  _full(bff.shape), _full(mu3.shape), _full(w1.shape), _full(a1.shape), _full(g1.shape)],
        out_specs=[tile(D), tile(HW), tile(HW), tile(HW), tile(3 * HW), tile(2 * D), tile(256), tile(LANES),
                   row(3 * HW)],
        out_shape=out_shape,
        scratch_shapes=[pltpu.VMEM((1, D), F32)],
        compiler_params=_params("arbitrary", "arbitrary"),
        name="inproj",
    )(x, prev_row, ln_g, ln_b, wm, wff, bff, mu3, w1, a1, g1)


def _cumsum_kernel(x_ref, c_ref, carry_ref):
    t = pl.program_id(1)
    tt = x_ref.shape[0]

    @pl.when(t == 0)
    def _():
        carry_ref[...] = jnp.zeros_like(carry_ref)

    r = lax.broadcasted_iota(I32, (tt, tt), 0)
    c = lax.broadcasted_iota(I32, (tt, tt), 1)
    tri = jnp.where(r >= c, 1.0, 0.0).astype(F32)
    cs = jnp.dot(tri, x_ref[...], preferred_element_type=F32, precision=HIGHEST) + carry_ref[...]
    c_ref[...] = cs
    carry_ref[...] = cs[tt - 1:tt, :]


def _cumsum_time(x):
    B, T, W = x.shape
    tt = _largest_tile(T, 384)
    spec = pl.BlockSpec((None, tt, W), lambda b, t: (b, t, 0))
    return pl.pallas_call(
        _cumsum_kernel,
        grid=(B, T // tt),
        in_specs=[spec],
        out_specs=spec,
        out_shape=jax.ShapeDtypeStruct((B, T, W), F32),
        scratch_shapes=[pltpu.VMEM((1, W), F32)],
        compiler_params=_params("arbitrary", "arbitrary"),
        name="logf_cumsum",
    )(x)


def _fox_kernel(q_ref, k_ref, v_ref, cq_ref, ck_ref, o_ref, m_ref, l_ref, acc_ref, *, q0, tq, tk, nk):
    qi = pl.program_id(1)
    ki = pl.program_id(2)

    @pl.when(ki == 0)
    def _():
        m_ref[...] = jnp.full_like(m_ref, NEG_BIG)
        l_ref[...] = jnp.zeros_like(l_ref)
        acc_ref[...] = jnp.zeros_like(acc_ref)

    first_q = q0 + qi * tq

    @pl.when(ki * tk <= first_q + tq - 1)
    def _():
        q = q_ref[...]
        k = k_ref[...].astype(BF16)
        v = v_ref[...].astype(BF16)
        cq = cq_ref[...]
        eye = jnp.where(lax.broadcasted_iota(I32, (8, LANES), 0) == lax.broadcasted_iota(I32, (8, LANES), 1),
                        1.0, 0.0).astype(F32)
        ck_t = lax.dot_general(eye, ck_ref[...], NT_DIMS, preferred_element_type=F32, precision=HIGHEST)
        rows = first_q + lax.broadcasted_iota(I32, (tq, tk), 0)
        cols = ki * tk + lax.broadcasted_iota(I32, (tq, tk), 1)
        causal = rows >= cols
        scale = HEAD_DIM ** -0.5
        for h in range(N_HEADS):
            hs = slice(h * HEAD_DIM, (h + 1) * HEAD_DIM)
            s = lax.dot_general(q[:, hs], k[:, hs], NT_DIMS, preferred_element_type=F32) * scale
            s = s + cq[:, h:h + 1] - ck_t[h:h + 1, :]
            s = jnp.where(causal, s, NEG_BIG)
            m_prev = m_ref[h][:, 0:1]
            m_new = jnp.maximum(m_prev, jnp.max(s, -1, keepdims=True))
            alpha = jnp.exp(m_prev - m_new)
            p = jnp.exp(s - m_new)
            l_new = alpha * l_ref[h][:, 0:1] + jnp.sum(p, -1, keepdims=True)
            acc_ref[:, hs] = alpha * acc_ref[:, hs] + _dot(p.astype(BF16), v[:, hs])
            m_ref[h] = jnp.broadcast_to(m_new, (tq, LANES))
            l_ref[h] = jnp.broadcast_to(l_new, (tq, LANES))

    @pl.when(ki == nk - 1)
    def _():
        for h in range(N_HEADS):
            hs = slice(h * HEAD_DIM, (h + 1) * HEAD_DIM)
            o_ref[:, hs] = (acc_ref[:, hs] / l_ref[h][:, 0:1]).astype(o_ref.dtype)


def _fox_attention(q, k, v, cq, ck, q0):
    B, Tq, _ = q.shape
    Tk = k.shape[1]
    tq = _largest_tile(Tq, 384)
    tk = _largest_tile(Tk, 384)
    nq, nk = Tq // tq, Tk // tk
    last_tile = lambda qi: jnp.minimum((q0 + (qi + 1) * tq - 1) // tk, nk - 1)
    qspec = lambda w: pl.BlockSpec((None, tq, w), lambda b, qi, ki: (b, qi, 0))
    kspec = lambda w: pl.BlockSpec((None, tk, w), lambda b, qi, ki: (b, jnp.minimum(ki, last_tile(qi)), 0))
    return pl.pallas_call(
        functools.partial(_fox_kernel, q0=q0, tq=tq, tk=tk, nk=nk),
        grid=(B, nq, nk),
        in_specs=[qspec(HW), kspec(HW), kspec(HW), qspec(LANES), kspec(LANES)],
        out_specs=qspec(HW),
        out_shape=jax.ShapeDtypeStruct((B, Tq, HW), BF16),
        scratch_shapes=[pltpu.VMEM((N_HEADS, tq, LANES), F32), pltpu.VMEM((N_HEADS, tq, LANES), F32),
                        pltpu.VMEM((tq, HW), F32)],
        compiler_params=_params("arbitrary", "arbitrary", "arbitrary"),
        name="fox_attention",
    )(q, k, v, cq, ck)


def _rwkv_kernel(rkv_ref, lmid_ref, rkv0_ref, s0_ref, mu_ref, w0_ref, w2_ref, a0_ref, a2_ref, g2_ref,
                 kk_ref, ka_ref, rk_ref, gng_ref, gnb_ref, o_ref, sfin_ref, state_ref, carry_ref, *, chunk, levels):
    C = chunk
    T = rkv_ref.shape[0]
    state_ref[...] = s0_ref[...]
    carry_ref[...] = rkv0_ref[...]
    row_w = lax.broadcasted_iota(I32, (C, 3 * HW), 0)
    row_h = lax.broadcasted_iota(I32, (C, HW), 0)
    r_i = lax.broadcasted_iota(I32, (C, C), 0)
    c_i = lax.broadcasted_iota(I32, (C, C), 1)
    strict = r_i > c_i
    incl = r_i >= c_i
    mid = C // 2 - 1 if C > 1 else 0

    def chunk_body(i, carry):
        off = pl.multiple_of(i * C, C)
        x = rkv_ref[pl.ds(off, C), :].astype(F32)
        prev = jnp.where(row_w == 0, carry_ref[...], pltpu.roll(x, 1, axis=0))
        carry_ref[...] = x[C - 1:C, :]
        x = x + (prev - x) * mu_ref[...]
        r, k0, v = x[:, 0:HW], x[:, HW:2 * HW], x[:, 2 * HW:3 * HW]
        lm = lmid_ref[pl.ds(off, C), :]
        w_pre = w0_ref[...] + _dot(jnp.tanh(lm[:, 0:64]).astype(BF16), w2_ref[...])
        a = _sigmoid(a0_ref[...] + _dot(lm[:, 64:128].astype(BF16), a2_ref[...]))
        g = _dot(_sigmoid(lm[:, 128:256]).astype(BF16), g2_ref[...])
        w_log = -_softplus(-w_pre) - 0.5
        logdec = -jnp.exp(w_log)
        L = logdec
        sh = 1
        while sh < C:
            L = L + jnp.where(row_h >= sh, pltpu.roll(L, sh, axis=0), 0.0)
            sh *= 2
        l_mid = L[mid:mid + 1, :]
        l_tot = L[C - 1:C, :]
        e_a = jnp.exp(L - logdec - l_mid)
        e_r = jnp.exp(L - l_mid)
        e_k = jnp.exp(l_mid - L)
        e_s = jnp.exp(l_tot - L)
        w_tot = jnp.exp(l_tot)
        e_mid = jnp.exp(l_mid)
        kk_raw = k0 * kk_ref[...]
        k = k0 * (1.0 + (a - 1.0) * ka_ref[...])
        rk = rk_ref[...]
        H = range(N_HEADS)
        hsl = [slice(h * HEAD_DIM, (h + 1) * HEAD_DIM) for h in H]
        nt = lambda x, y: lax.dot_general(x, y, NT_DIMS, preferred_element_type=F32)
        tn = lambda x, y: lax.dot_general(x, y, TN_DIMS, preferred_element_type=F32)
        kkh = [kk_raw[:, hs] for hs in hsl]
        kkh = [x * lax.rsqrt(jnp.maximum(jnp.sum(x * x, -1, keepdims=True), 1e-24)) for x in kkh]
        r_h = [r[:, hs] for hs in hsl]
        k_h = [k[:, hs] for hs in hsl]
        v_h = [v[:, hs] for hs in hsl]
        b_h = [kkh[h] * a[:, hsl[h]] for h in H]
        ar = [jnp.concatenate([-kkh[h] * e_a[:, hsl[h]], r_h[h] * e_r[:, hsl[h]]], axis=0) for h in H]
        bk = [b_h[h] * e_k[:, hsl[h]] for h in H]
        kd = [k_h[h] * e_k[:, hsl[h]] for h in H]
        s_old = [state_ref[h] for h in H]
        g_b = [nt(ar[h], bk[h]) for h in H]
        g_k = [nt(ar[h], kd[h]) for h in H]
        x0 = [nt(ar[h], s_old[h] * e_mid[:, hsl[h]]) for h in H]
        a_mat = [jnp.where(strict, g_b[h][0:C], 0.0) for h in H]
        u = [x0[h][0:C] + _dot(jnp.where(strict, g_k[h][0:C], 0.0), v_h[h]) for h in H]
        for lvl in range(levels):
            au = [_dot(a_mat[h], u[h]) for h in H]
            if lvl + 1 < levels:
                a_mat = [_dot(a_mat[h], a_mat[h]) for h in H]
            u = [u[h] + au[h] for h in H]
        y = [x0[h][C:2 * C] + _dot(jnp.where(incl, g_b[h][C:2 * C], 0.0), u[h])
             + _dot(jnp.where(incl, g_k[h][C:2 * C], 0.0), v_h[h]) for h in H]
        for h in H:
            uv = jnp.concatenate([u[h], v_h[h]], axis=0)
            bks = jnp.concatenate([b_h[h] * e_s[:, hsl[h]], k_h[h] * e_s[:, hsl[h]]], axis=0)
            state_ref[h] = s_old[h] * w_tot[:, hsl[h]] + tn(uv, bks)
        outs = []
        for h in H:
            mu = jnp.mean(y[h], -1, keepdims=True)
            yc = y[h] - mu
            var = jnp.mean(yc * yc, -1, keepdims=True)
            bonus = jnp.sum(r_h[h] * k_h[h] * rk[:, hsl[h]], -1, keepdims=True) * v_h[h]
            outs.append((yc * lax.rsqrt(var + GN_EPS), bonus))
        yn = jnp.concatenate([o[0] for o in outs], axis=1)
        bonus = jnp.concatenate([o[1] for o in outs], axis=1)
        out = (yn * gng_ref[...] + gnb_ref[...] + bonus) * g
        o_ref[pl.ds(off, C), :] = out.astype(o_ref.dtype)
        return carry

    lax.fori_loop(0, T // C, chunk_body, 0)
    sfin_ref[...] = state_ref[...]


def _rwkv(rkv, lmid, rkv0, s0, mu_rkv, w0, w2, a0, a2, g2, k_k, k_a, r_k, gn_g, gn_b):
    B, T, _ = rkv.shape
    chunk = _largest_tile(T, 64, mult=16)
    levels = max(1, math.ceil(math.log2(chunk)))
    seq = lambda w: pl.BlockSpec((None, T, w), lambda b: (b, 0, 0))
    st = pl.BlockSpec((None, N_HEADS, HEAD_DIM, HEAD_DIM), lambda b: (b, 0, 0, 0))
    vec = lambda a: _full(a.shape)
    params = (mu_rkv, w0, w2, a0, a2, g2, k_k, k_a, r_k, gn_g, gn_b)
    return pl.pallas_call(
        functools.partial(_rwkv_kernel, chunk=chunk, levels=levels),
        grid=(B,),
        in_specs=[seq(3 * HW), seq(256), pl.BlockSpec((None, 1, 3 * HW), lambda b: (b, 0, 0)), st]
                 + [vec(p) for p in params],
        out_specs=[seq(HW), st],
        out_shape=[jax.ShapeDtypeStruct((B, T, HW), BF16),
                   jax.ShapeDtypeStruct((B, N_HEADS, HEAD_DIM, HEAD_DIM), F32)],
        scratch_shapes=[pltpu.VMEM((N_HEADS, HEAD_DIM, HEAD_DIM), F32), pltpu.VMEM((1, 3 * HW), F32)],
        compiler_params=_params("arbitrary"),
        name="rwkv7",
    )(rkv, lmid, rkv0, s0, *params)


def _merge_kernel(fox_ref, rw_ref, gate_ref, h_ref, wa_ref, wb_ref, wo_ref, g_ref, b_ref, wrh_ref, wrl_ref, br_ref,
                  h1_ref, idx_ref, gt_ref, rank_ref, cnt_ref, carry_ref, *, dn_alpha):
    i = pl.program_id(0)
    tm, D = h_ref.shape

    @pl.when(i == 0)
    def _():
        carry_ref[...] = jnp.zeros_like(carry_ref)

    gates = gate_ref[...].astype(F32)
    merged = gates[:, 0:D] * _dot(fox_ref[...], wa_ref[...]) + gates[:, D:2 * D] * _dot(rw_ref[...], wb_ref[...])
    z = dn_alpha * h_ref[...] + _dot(merged.astype(BF16), wo_ref[...])
    h1 = _layer_norm(z, g_ref[...], b_ref[...])
    h1_ref[...] = h1
    hi = h1.astype(BF16)
    lo = (h1 - hi.astype(F32)).astype(BF16)
    logits = _dot(hi, wrh_ref[...]) + _dot(hi, wrl_ref[...]) + _dot(lo, wrh_ref[...]) + br_ref[...]
    lane = lax.broadcasted_iota(I32, (tm, LANES), 1)
    lane_f = lane.astype(F32)
    cur = logits
    vals, idxs = [], []
    for _ in range(TOP_K):
        m = jnp.max(cur, -1, keepdims=True)
        ix = jnp.min(jnp.where(cur == m, lane_f, float(LANES)), -1, keepdims=True)
        vals.append(m)
        idxs.append(ix)
        cur = jnp.where(lane_f == ix, -3e38, cur)
    exps = [jnp.exp(vk - vals[0]) for vk in vals]
    denom = exps[0] + exps[1] + exps[2] + exps[3]
    onehot = jnp.zeros((tm, LANES), F32)
    for ix in idxs:
        onehot = onehot + jnp.where(lane_f == ix, 1.0, 0.0)
    r_i = lax.broadcasted_iota(I32, (tm, tm), 0)
    c_i = lax.broadcasted_iota(I32, (tm, tm), 1)
    tri = jnp.where(r_i > c_i, 1.0, 0.0).astype(BF16)
    before = _dot(tri, onehot.astype(BF16)) + carry_ref[...]
    idx_out = jnp.zeros((tm, LANES), F32)
    gt_out = jnp.zeros((tm, LANES), F32)
    rank_out = jnp.zeros((tm, LANES), F32)
    for kx in range(TOP_K):
        rank_k = jnp.sum(jnp.where(lane_f == idxs[kx], before, 0.0), -1, keepdims=True)
        idx_out = jnp.where(lane == kx, idxs[kx], idx_out)
        gt_out = jnp.where(lane == kx, exps[kx] / denom, gt_out)
        rank_out = jnp.where(lane == kx, rank_k, rank_out)
    idx_ref[...] = idx_out.astype(I32)
    gt_ref[...] = gt_out
    rank_ref[...] = rank_out.astype(I32)
    total = carry_ref[...] + jnp.sum(onehot, 0, keepdims=True)
    carry_ref[...] = total
    cnt_ref[...] = total.astype(I32)


def _merge_route(fox, rw, gates, h, wa, wb, wo, ln_g, ln_b, wr_hi, wr_lo, br, dn_alpha):
    N, D = h.shape
    tm = _largest_tile(N, 256)
    tile = lambda w: pl.BlockSpec((tm, w), lambda i: (i, 0))
    return pl.pallas_call(
        functools.partial(_merge_kernel, dn_alpha=dn_alpha),
        grid=(N // tm,),
        in_specs=[tile(HW), tile(HW), tile(2 * D), tile(D), _full(wa.shape), _full(wb.shape), _full(wo.shape),
                  _full((1, D)), _full((1, D)), _full(wr_hi.shape), _full(wr_lo.shape), _full(br.shape)],
        out_specs=[tile(D), tile(LANES), tile(LANES), tile(LANES), _full((1, LANES))],
        out_shape=[jax.ShapeDtypeStruct((N, D), F32), jax.ShapeDtypeStruct((N, LANES), I32),
                   jax.ShapeDtypeStruct((N, LANES), F32), jax.ShapeDtypeStruct((N, LANES), I32),
                   jax.ShapeDtypeStruct((1, LANES), I32)],
        scratch_shapes=[pltpu.VMEM((1, LANES), F32)],
        compiler_params=_params("arbitrary"),
        name="merge_route",
    )(fox, rw, gates, h, wa, wb, wo, ln_g, ln_b, wr_hi, wr_lo, br)


PERM_W = 256


def _deinterleave_kernel(w_ref, g_ref, l_ref):
    half = PERM_W // 2
    ii = lax.broadcasted_iota(I32, (PERM_W, PERM_W), 0)
    jj = lax.broadcasted_iota(I32, (PERM_W, PERM_W), 1)
    src = jnp.where(jj < half, 2 * jj, 2 * (jj - half) + 1)
    perm = jnp.where(ii == src, 1.0, 0.0).astype(BF16)
    for c in range(w_ref.shape[1] // PERM_W):
        w = w_ref[:, c * PERM_W:(c + 1) * PERM_W].astype(BF16)
        out = _dot(w, perm)
        g_ref[:, c * half:(c + 1) * half] = out[:, :half].astype(BF16)
        l_ref[:, c * half:(c + 1) * half] = out[:, half:].astype(BF16)


def _deinterleave_w1(w_e1):
    E, D, F2 = w_e1.shape
    wc = 512
    out = jax.ShapeDtypeStruct((E, D, F2 // 2), BF16)
    return pl.pallas_call(
        _deinterleave_kernel,
        grid=(E, F2 // wc),
        in_specs=[pl.BlockSpec((None, D, wc), lambda e, c: (e, 0, c))],
        out_specs=[pl.BlockSpec((None, D, wc // 2), lambda e, c: (e, 0, c))] * 2,
        out_shape=[out, out],
        compiler_params=_params("arbitrary", "arbitrary"),
        name="w1_deinterleave",
    )(w_e1)


def _moe_gather_start(tok_ref, x_hbm, buf, sem, slot, bm):
    def body(r, c):
        tok = tok_ref[0, 0, r]
        pltpu.make_async_copy(x_hbm.at[pl.ds(tok, 1)], buf.at[slot, pl.ds(r, 1)], sem.at[slot]).start()
        return c
    lax.fori_loop(0, bm, body, 0, unroll=8)


def _moe_kernel(be_ref, nused_ref, tok_ref, tokn_ref, x_hbm, w1g_ref, w1l_ref, b1g_ref, b1l_ref, w2_ref, b2_ref,
                y_ref, buf, sem):
    j = pl.program_id(0)
    nb = pl.num_programs(0)
    bm = buf.shape[1]
    nused = nused_ref[0]
    slot = j % 2

    @pl.when(j == 0)
    def _():
        _moe_gather_start(tok_ref, x_hbm, buf, sem, 0, bm)

    @pl.when(jnp.logical_and(j + 1 < nb, j + 1 < nused))
    def _():
        _moe_gather_start(tokn_ref, x_hbm, buf, sem, 1 - slot, bm)

    @pl.when(jnp.logical_or(j < nused, j == 0))
    def _():
        pltpu.make_async_copy(x_hbm.at[pl.ds(0, bm)], buf.at[slot], sem.at[slot]).wait()

    @pl.when(j < nused)
    def _():
        x = buf[slot].astype(BF16)
        glu = jnp.minimum(_dot(x, w1g_ref[...]) + b1g_ref[...], SWIGLU_LIMIT)
        lin = jnp.clip(_dot(x, w1l_ref[...]) + b1l_ref[...], -SWIGLU_LIMIT, SWIGLU_LIMIT)
        act = glu * _sigmoid(SWIGLU_ALPHA * glu) * (lin + 1.0)
        y_ref[...] = _dot(act.astype(BF16), w2_ref[...]) + b2_ref[...]

    @pl.when(j >= nused)
    def _():
        y_ref[...] = jnp.zeros_like(y_ref)


def _moe_experts(x, slot_tok, blk_e, nused, w1g, w1l, b1g, b1l, w2, b2, bm):
    N, D = x.shape
    nb = slot_tok.shape[0]
    F = w1g.shape[2]
    last = lambda j, be, nu: jnp.minimum(j, jnp.maximum(nu[0] - 1, 0))
    wspec = lambda k, n: pl.BlockSpec((None, k, n), lambda j, be, nu: (be[last(j, be, nu)], 0, 0))
    grid_spec = pltpu.PrefetchScalarGridSpec(
        num_scalar_prefetch=2,
        grid=(nb,),
        in_specs=[
            pl.BlockSpec((1, 1, bm), lambda j, be, nu: (j, 0, 0), memory_space=pltpu.SMEM),
            pl.BlockSpec((1, 1, bm), lambda j, be, nu: (jnp.minimum(j + 1, nb - 1), 0, 0), memory_space=pltpu.SMEM),
            pl.BlockSpec(memory_space=pl.ANY),
            wspec(D, F), wspec(D, F), wspec(1, F), wspec(1, F), wspec(F, D), wspec(1, D),
        ],
        out_specs=pl.BlockSpec((bm, D), lambda j, be, nu: (j, 0)),
        scratch_shapes=[pltpu.VMEM((2, bm, D), F32), pltpu.SemaphoreType.DMA((2,))],
    )
    return pl.pallas_call(
        _moe_kernel,
        grid_spec=grid_spec,
        out_shape=jax.ShapeDtypeStruct((nb * bm, D), F32),
        compiler_params=_params("arbitrary"),
        name="moe_experts",
    )(blk_e, nused, slot_tok, slot_tok, x, w1g, w1l, b1g, b1l, w2, b2)


def _combine_gather_start(dest_ref, y_hbm, buf, sem, slot, tm):
    def body(r, c):
        for kx in range(TOP_K):
            d = dest_ref[0, 0, r * TOP_K + kx]
            pltpu.make_async_copy(y_hbm.at[pl.ds(d, 1)], buf.at[slot, kx, pl.ds(r, 1)], sem.at[slot]).start()
        return c
    lax.fori_loop(0, tm, body, 0, unroll=4)


def _combine_kernel(dest_ref, destn_ref, gt_ref, h1_ref, g_ref, b_ref, y_hbm, o_ref, buf, sem, *, dn_alpha):
    i = pl.program_id(0)
    n = pl.num_programs(0)
    tm = h1_ref.shape[0]
    slot = i % 2

    @pl.when(i == 0)
    def _():
        _combine_gather_start(dest_ref, y_hbm, buf, sem, 0, tm)

    @pl.when(i + 1 < n)
    def _():
        _combine_gather_start(destn_ref, y_hbm, buf, sem, 1 - slot, tm)

    for kx in range(TOP_K):
        pltpu.make_async_copy(y_hbm.at[pl.ds(0, tm)], buf.at[slot, kx], sem.at[slot]).wait()
    gt = gt_ref[...]
    moe = gt[:, 0:1] * buf[slot, 0]
    for kx in range(1, TOP_K):
        moe = moe + gt[:, kx:kx + 1] * buf[slot, kx]
    o_ref[...] = _layer_norm(dn_alpha * h1_ref[...] + moe, g_ref[...], b_ref[...])


def _moe_combine(dest, gate, h1, ln_g, ln_b, yb, dn_alpha):
    N, D = h1.shape
    tm = _largest_tile(N, 256)
    n = N // tm
    dest3 = dest.reshape(n, 1, tm * TOP_K)
    tile = lambda w: pl.BlockSpec((tm, w), lambda i: (i, 0))
    return pl.pallas_call(
        functools.partial(_combine_kernel, dn_alpha=dn_alpha),
        grid=(n,),
        in_specs=[
            pl.BlockSpec((1, 1, tm * TOP_K), lambda i: (i, 0, 0), memory_space=pltpu.SMEM),
            pl.BlockSpec((1, 1, tm * TOP_K), lambda i: (jnp.minimum(i + 1, n - 1), 0, 0), memory_space=pltpu.SMEM),
            tile(LANES), tile(D), _full((1, D)), _full((1, D)),
            pl.BlockSpec(memory_space=pl.ANY),
        ],
        out_specs=tile(D),
        out_shape=jax.ShapeDtypeStruct((N, D), F32),
        scratch_shapes=[pltpu.VMEM((2, TOP_K, tm, D), F32), pltpu.SemaphoreType.DMA((2,))],
        compiler_params=_params("arbitrary"),
        name="moe_combine",
    )(dest3, dest3, gate, h1, ln_g, ln_b, yb)


def _route_tables(top_idx, rank, counts, n_experts):
    n_tok = top_idx.shape[0]
    n_asg = n_tok * TOP_K
    bm = min(256, max(8, 1 << int(math.log2(max(1, n_asg // n_experts)))))
    nb = -(-n_asg // bm) + n_experts
    padded = (counts + bm - 1) // bm * bm
    pends = jnp.cumsum(padded)
    dest = (pends - padded)[top_idx] + rank
    tok = jnp.broadcast_to(jnp.arange(n_tok, dtype=I32)[:, None], dest.shape)
    slot_tok = jnp.zeros((nb * bm,), I32).at[dest.reshape(-1)].set(tok.reshape(-1))
    blk_e = jnp.minimum(jnp.searchsorted(pends, jnp.arange(nb, dtype=I32) * bm, side='right'),
                        n_experts - 1).astype(I32)
    nused = (pends[-1] // bm).astype(I32).reshape(1)
    return dest.astype(I32), slot_tok.reshape(nb, 1, bm), blk_e, nused, bm


def _stream(x, prev_row, s0, past_k, past_v, past_logf, wts):
    B, T, D = x.shape
    dn_alpha = wts['dn_alpha']
    h, q, k, v, rkv, gates, lmid, logf, rkv0 = _inproj(
        x, prev_row, wts['ln0_g'], wts['ln0_b'], wts['wm'], wts['wff'], wts['bff'], wts['mu3'],
        wts['w1'], wts['a1'], wts['g1'])
    if past_k is None:
        c = _cumsum_time(logf)
        fox = _fox_attention(q, k, v, c, c, 0)
    else:
        P = past_k.shape[1]
        past_pad = jnp.pad(past_logf.astype(F32), ((0, 0), (0, 0), (0, LANES - N_HEADS)))
        c = _cumsum_time(jnp.concatenate([past_pad, logf], axis=1))
        k_all = jnp.concatenate([past_k.reshape(B, P, HW), k], axis=1)
        v_all = jnp.concatenate([past_v.reshape(B, P, HW), v], axis=1)
        fox = _fox_attention(q, k_all, v_all, c[:, P:], c, P)
    rw, s_fin = _rwkv(rkv, lmid, rkv0, s0, wts['mu_rkv'], wts['w0'], wts['w2'], wts['a0'], wts['a2'], wts['g2'],
                      wts['k_k'], wts['k_a'], wts['r_k'], wts['gn_g'], wts['gn_b'])
    N = B * T
    h1, top_idx, gate, rank, counts = _merge_route(
        fox.reshape(N, HW), rw.reshape(N, HW), gates.reshape(N, 2 * D), h.reshape(N, D),
        wts['w_up_a'], wts['w_up_b'], wts['w_out'], wts['ln1_g'], wts['ln1_b'],
        wts['wr_hi'], wts['wr_lo'], wts['br'], dn_alpha)
    n_experts = wts['n_experts']
    dest, slot_tok, blk_e, nused, bm = _route_tables(top_idx[:, :TOP_K], rank[:, :TOP_K], counts[0, :n_experts],
                                                     n_experts)
    yb = _moe_experts(h1, slot_tok, blk_e, nused, wts['w1g'], wts['w1l'], wts['b1g'], wts['b1l'],
                      wts['we2'], wts['be2'], bm)
    y = _moe_combine(dest, gate, h1, wts['ln2_g'], wts['ln2_b'], yb, dn_alpha)
    new_k = k.reshape(B, T, N_HEADS, HEAD_DIM)
    new_v = v.reshape(B, T, N_HEADS, HEAD_DIM)
    return y.reshape(B, T, D), new_k, new_v, logf[:, :, :N_HEADS], s_fin, h[:, T - 1:T, :]


def kernel(x_prompt, x_sample, cache_fox_k, cache_fox_v, cache_fox_logf, state_rwkv, state_shift, meta, ln0_g, ln0_b, w_in, b_forget, mu_w, mu_a, mu_g, mu_rkv, w0, w1, w2, a0, a1, a2, g1, g2, k_k, k_a, r_k, gn_g, gn_b, w_up_a, w_up_b, w_out, ln1_g, ln1_b, w_router, b_router, w_e1, b_e1, w_e2, b_e2, ln2_g, ln2_b):
    depth, D, in_cols = w_in.shape
    assert depth == 1 and D == 1024 and in_cols == 6 * HW + N_HEADS + 2 * D
    n_experts = w_router.shape[2]
    assert n_experts <= LANES
    B = x_prompt.shape[0]
    l = 0
    w = w_in[l]
    off_ff = 3 * HW
    row = lambda a: a.reshape(1, -1).astype(F32)
    wr = jnp.pad(w_router[l], ((0, 0), (0, LANES - n_experts)))
    wr_hi = wr.astype(BF16)
    w1g, w1l = _deinterleave_w1(w_e1[l])
    wts = dict(
        dn_alpha=float((2 * depth) ** 0.25), n_experts=n_experts,
        ln0_g=row(ln0_g), ln0_b=row(ln0_b),
        wm=jnp.concatenate([w[:, :off_ff], w[:, off_ff + N_HEADS:]], axis=1).astype(BF16),
        wff=jnp.pad(w[:, off_ff:off_ff + N_HEADS], ((0, 0), (0, LANES - N_HEADS))).astype(BF16),
        bff=jnp.pad(row(b_forget[l]), ((0, 0), (0, LANES - N_HEADS))),
        mu3=jnp.stack([mu_w[l], mu_a[l], mu_g[l]], axis=0),
        w1=w1[l].astype(BF16), a1=a1[l].astype(BF16), g1=g1[l].astype(BF16),
        mu_rkv=row(mu_rkv[l]), w0=row(w0[l]), w2=w2[l].astype(BF16), a0=row(a0[l]), a2=a2[l].astype(BF16),
        g2=g2[l].astype(BF16), k_k=row(k_k[l]), k_a=row(k_a[l]), r_k=row(r_k[l]), gn_g=row(gn_g[l]),
        gn_b=row(gn_b[l]),
        w_up_a=w_up_a[l].astype(BF16), w_up_b=w_up_b[l].astype(BF16), w_out=w_out[l].astype(BF16),
        ln1_g=row(ln1_g[l]), ln1_b=row(ln1_b[l]),
        wr_hi=wr_hi, wr_lo=(wr - wr_hi.astype(F32)).astype(BF16),
        br=jnp.pad(row(b_router[l]), ((0, 0), (0, LANES - n_experts)), constant_values=NEG_BIG),
        w1g=w1g, w1l=w1l,
        b1g=b_e1[l][:, None, 0::2], b1l=b_e1[l][:, None, 1::2],
        we2=w_e2[l].astype(BF16), be2=b_e2[l][:, None, :],
        ln2_g=row(ln2_g[l]), ln2_b=row(ln2_b[l]),
    )
    meta_b = jnp.broadcast_to(meta, (B, N_META, D)).astype(x_prompt.dtype)
    xp = jnp.concatenate([meta_b, x_prompt], axis=1)
    zero_row = jnp.zeros((B, 1, D), F32)
    zero_state = jnp.zeros((B, N_HEADS, HEAD_DIM, HEAD_DIM), F32)
    y_p, k_p, v_p, lf_p, s_p, sh_p = _stream(xp, zero_row, zero_state, None, None, None, wts)
    y_s, k_s, v_s, lf_s, s_s, sh_s = _stream(x_sample, state_shift[l], state_rwkv[l], cache_fox_k[l],
                                             cache_fox_v[l], cache_fox_logf[l], wts)
    ex = lambda a: a[None]
    return (y_p[:, N_META:], y_s, ex(k_p), ex(v_p), ex(lf_p), ex(s_p), ex(sh_p),
            ex(k_s), ex(v_s), ex(lf_s), ex(s_s), ex(sh_s))
```

```python
import functools
import math

import jax
import jax.numpy as jnp
from jax import lax
from jax.experimental import pallas as pl
from jax.experimental.pallas import tpu as pltpu

F32 = jnp.float32
BF16 = jnp.bfloat16
I32 = jnp.int32
U32 = jnp.uint32

N_META = 16
HEAD_DIM = 64
N_HEADS = 8
HW = N_HEADS * HEAD_DIM
TOP_K = 4
SWIGLU_LIMIT = 7.0
SWIGLU_ALPHA = 1.702
LN_EPS = 1e-5
GN_EPS = 64e-5
LANES = 128
NEG_BIG = -1e30
VMEM_LIMIT_BYTES = 56 * 1024 * 1024
HIGHEST = lax.Precision.HIGHEST

NT_DIMS = (((1,), (1,)), ((), ()))
TN_DIMS = (((0,), (0,)), ((), ()))


def _params(*sem):
    return pltpu.CompilerParams(dimension_semantics=sem, vmem_limit_bytes=VMEM_LIMIT_BYTES)


def _largest_tile(n, cap, mult=8):
    best = None
    for d in range(mult, min(n, cap) + 1, mult):
        if n % d == 0:
            best = d
    assert best is not None, (n, cap, mult)
    return best


def _sigmoid(x):
    return 1.0 / (1.0 + jnp.exp(-x))


def _softplus(x):
    return jnp.maximum(x, 0.0) + jnp.log1p(jnp.exp(-jnp.abs(x)))


def _layer_norm(x, g, b):
    mu = jnp.mean(x, -1, keepdims=True)
    xc = x - mu
    var = jnp.mean(xc * xc, -1, keepdims=True)
    return xc * lax.rsqrt(var + LN_EPS) * g + b


def _dot(a, b):
    return jnp.dot(a, b, preferred_element_type=F32)


def _full(shape):
    n = len(shape)
    return pl.BlockSpec(shape, lambda *_: (0,) * n)


def _pack_bf16_pair(x):
    w = x.shape[1] // 2
    bits = lambda t: lax.bitcast_convert_type(t.astype(BF16).astype(F32), U32)
    return (bits(x[:, :w]) >> 16) | (bits(x[:, w:]) & jnp.uint32(0xFFFF0000))


def _unpack_bf16_pair(u):
    return lax.bitcast_convert_type(u << 16, F32), lax.bitcast_convert_type(u & jnp.uint32(0xFFFF0000), F32)


C_Q, C_K, C_V, C_RKV, C_GA, C_END = 0, HW, 2 * HW, 3 * HW, 6 * HW, 6 * HW + 2048


def _inproj_kernel(x_ref, prev_ref, g_ref, b_ref, wm_ref, wff_ref, bff_ref, mu_ref, w1_ref, a1_ref, g1_ref,
                   h_ref, q_ref, k_ref, v_ref, rkv_ref, gate_ref, lmid_ref, logf_ref, rkv0_ref,
                   carry_ref):
    t = pl.program_id(1)
    tt = x_ref.shape[0]
    h = _layer_norm(x_ref[...], g_ref[...], b_ref[...])
    h_ref[...] = h

    @pl.when(t == 0)
    def _():
        prev = prev_ref[...]
        carry_ref[...] = prev
        p8 = jnp.broadcast_to(prev, (8, prev.shape[1])).astype(BF16)
        rkv0_ref[...] = _dot(p8, wm_ref[:, C_RKV:C_GA])[0:1]

    rows = lax.broadcasted_iota(I32, h.shape, 0)
    hprev = jnp.where(rows == 0, carry_ref[...], pltpu.roll(h, 1, axis=0))
    carry_ref[...] = h[tt - 1:tt, :]
    dx = hprev - h
    hb = h.astype(BF16)
    q_ref[...] = _dot(hb, wm_ref[:, C_Q:C_K]).astype(BF16)
    k_ref[...] = _dot(hb, wm_ref[:, C_K:C_V])
    v_ref[...] = _dot(hb, wm_ref[:, C_V:C_RKV])
    rkv_ref[...] = _dot(hb, wm_ref[:, C_RKV:C_GA]).astype(BF16)
    gate_ref[...] = _sigmoid(_dot(hb, wm_ref[:, C_GA:C_END])).astype(BF16)
    ff = _dot(hb, wff_ref[...]) + bff_ref[...]
    logf_ref[...] = -_softplus(-ff)
    mu = mu_ref[...]
    lmid_ref[:, 0:64] = _dot((h + dx * mu[0:1]).astype(BF16), w1_ref[...])
    lmid_ref[:, 64:128] = _dot((h + dx * mu[1:2]).astype(BF16), a1_ref[...])
    lmid_ref[:, 128:256] = _dot((h + dx * mu[2:3]).astype(BF16), g1_ref[...])


def _inproj(x, prev_row, ln_g, ln_b, wm, wff, bff, mu3, w1, a1, g1):
    B, T, D = x.shape
    tt = _largest_tile(T, 384)
    nt = T // tt
    tile = lambda w: pl.BlockSpec((None, tt, w), lambda b, t: (b, t, 0))
    row = lambda w: pl.BlockSpec((None, 1, w), lambda b, t: (b, 0, 0))
    out_shape = [
        jax.ShapeDtypeStruct((B, T, D), F32),
        jax.ShapeDtypeStruct((B, T, HW), BF16),
        jax.ShapeDtypeStruct((B, T, HW), F32),
        jax.ShapeDtypeStruct((B, T, HW), F32),
        jax.ShapeDtypeStruct((B, T, 3 * HW), BF16),
        jax.ShapeDtypeStruct((B, T, 2 * D), BF16),
        jax.ShapeDtypeStruct((B, T, 256), F32),
        jax.ShapeDtypeStruct((B, T, LANES), F32),
        jax.ShapeDtypeStruct((B, 1, 3 * HW), F32),
    ]
    return pl.pallas_call(
        _inproj_kernel,
        grid=(B, nt),
        in_specs=[tile(D), row(D), _full((1, D)), _full((1, D)), _full(wm.shape), _full(wff.shape),
                  _full(bff.shape), _full(mu3.shape), _full(w1.shape), _full(a1.shape), _full(g1.shape)],
        out_specs=[tile(D), tile(HW), tile(HW), tile(HW), tile(3 * HW), tile(2 * D), tile(256), tile(LANES),
                   row(3 * HW)],
        out_shape=out_shape,
        scratch_shapes=[pltpu.VMEM((1, D), F32)],
        compiler_params=_params("arbitrary", "arbitrary"),
        name="inproj",
    )(x, prev_row, ln_g, ln_b, wm, wff, bff, mu3, w1, a1, g1)


HEAD_PAD = 2 * HEAD_DIM
C_SPLIT = 3


def _aug_select_matrices():
    rows = jnp.arange(HW)
    sel_k = jnp.zeros((HW, N_HEADS * HEAD_PAD), F32).at[rows, (rows // HEAD_DIM) * HEAD_PAD + rows % HEAD_DIM].set(1.0)
    p = jnp.repeat(jnp.arange(C_SPLIT), N_HEADS)
    h = jnp.tile(jnp.arange(N_HEADS), C_SPLIT)
    sel_c = jnp.zeros((C_SPLIT * LANES, N_HEADS * HEAD_PAD), F32).at[p * LANES + h, h * HEAD_PAD + HEAD_DIM + p].set(1.0)
    return sel_k.astype(BF16), (sel_k * HEAD_DIM ** -0.5).astype(BF16), sel_c.astype(BF16)


def _fox_prep_kernel(lf_ref, k_ref, v_ref, selk_ref, selc_ref, ka_ref, vt_ref, carry_ref):
    t = pl.program_id(1)
    tt = lf_ref.shape[0]

    @pl.when(t == 0)
    def _():
        carry_ref[...] = jnp.zeros_like(carry_ref)

    r = lax.broadcasted_iota(I32, (tt, tt), 0)
    c = lax.broadcasted_iota(I32, (tt, tt), 1)
    tri = jnp.where(r >= c, 1.0, 0.0).astype(F32)
    cs = jnp.dot(tri, lf_ref[...], preferred_element_type=F32, precision=HIGHEST) + carry_ref[...]
    carry_ref[...] = cs[tt - 1:tt, :]
    neg = -cs
    hi = neg.astype(BF16)
    r1 = neg - hi.astype(F32)
    mid = r1.astype(BF16)
    lo = (r1 - mid.astype(F32)).astype(BF16)
    parts = jnp.concatenate([hi, mid, lo], axis=1)
    ka = _dot(k_ref[...].astype(BF16), selk_ref[...]) + _dot(parts, selc_ref[...])
    ka_ref[...] = ka.astype(BF16)
    ii = lax.broadcasted_iota(I32, (HW, HW), 0)
    jj = lax.broadcasted_iota(I32, (HW, HW), 1)
    eye = jnp.where(ii == jj, 1.0, 0.0).astype(BF16)
    vt_ref[...] = lax.dot_general(eye, v_ref[...].astype(BF16), NT_DIMS, preferred_element_type=F32).astype(BF16)


def _fox_prep(logf, k, v, sel_k, sel_c):
    B, T, _ = k.shape
    tt = _largest_tile(T, 384)
    nt = T // tt
    tile = lambda w: pl.BlockSpec((None, tt, w), lambda b, t: (b, t, 0))
    return pl.pallas_call(
        _fox_prep_kernel,
        grid=(B, nt),
        in_specs=[tile(LANES), tile(HW), tile(HW), _full(sel_k.shape), _full(sel_c.shape)],
        out_specs=[tile(N_HEADS * HEAD_PAD), pl.BlockSpec((None, None, HW, tt), lambda b, t: (b, t, 0, 0))],
        out_shape=[jax.ShapeDtypeStruct((B, T, N_HEADS * HEAD_PAD), BF16),
                   jax.ShapeDtypeStruct((B, nt, HW, tt), BF16)],
        scratch_shapes=[pltpu.VMEM((1, LANES), F32)],
        compiler_params=_params("arbitrary", "arbitrary"),
        name="fox_prep",
    )(logf, k, v, sel_k, sel_c)


def _fox_kernel(q_ref, ka_ref, vt_ref, selq_ref, o_ref, qa_ref, m_ref, l_ref, acc_ref, *, q0, tq, tk, nk):
    qi = pl.program_id(1)
    ki = pl.program_id(2)

    @pl.when(ki == 0)
    def _():
        m_ref[...] = jnp.full_like(m_ref, NEG_BIG)
        l_ref[...] = jnp.zeros_like(l_ref)
        acc_ref[...] = jnp.zeros_like(acc_ref)
        lane = lax.broadcasted_iota(I32, qa_ref.shape, 1) % HEAD_PAD
        ones = jnp.where(jnp.logical_and(lane >= HEAD_DIM, lane < HEAD_DIM + C_SPLIT), 1.0, 0.0)
        qa_ref[...] = (_dot(q_ref[...], selq_ref[...]) + ones).astype(BF16)

    first_q = q0 + qi * tq
    tile_first = ki * tk
    tile_last = tile_first + tk - 1

    def scores(h):
        hp = slice(h * HEAD_PAD, (h + 1) * HEAD_PAD)
        return lax.dot_general(ka_ref[:, hp], qa_ref[:, hp], NT_DIMS, preferred_element_type=F32)

    def tile_update(masked):
        if masked:
            key_pos = tile_first + lax.broadcasted_iota(I32, (tk, tq), 0)
            qry_pos = first_q + lax.broadcasted_iota(I32, (tk, tq), 1)
            bias = jnp.where(qry_pos >= key_pos, 0.0, NEG_BIG)
        m_all = m_ref[...]
        l_all = l_ref[...]
        m_rows, l_rows = [], []
        s_next = scores(0)
        for h in range(N_HEADS):
            hs = slice(h * HEAD_DIM, (h + 1) * HEAD_DIM)
            s = s_next
            if h + 1 < N_HEADS:
                s_next = scores(h + 1)
            if masked:
                s = s + bias
            m_prev = m_all[h:h + 1, :]
            m_new = jnp.maximum(m_prev, jnp.max(s, 0, keepdims=True))
            alpha = jnp.exp(m_prev - m_new)
            p = jnp.exp(s - m_new)
            l_rows.append(alpha * l_all[h:h + 1, :] + jnp.sum(p, 0, keepdims=True))
            m_rows.append(m_new)
            acc_ref[hs, :] = alpha * acc_ref[hs, :] + _dot(vt_ref[hs, :], p.astype(BF16))
        m_ref[...] = jnp.concatenate(m_rows, axis=0)
        l_ref[...] = jnp.concatenate(l_rows, axis=0)

    @pl.when(jnp.logical_and(tile_first <= first_q + tq - 1, tile_last > first_q))
    def _():
        tile_update(True)

    @pl.when(tile_last <= first_q)
    def _():
        tile_update(False)

    @pl.when(ki == nk - 1)
    def _():
        on = jnp.concatenate(
            [acc_ref[h * HEAD_DIM:(h + 1) * HEAD_DIM, :] / l_ref[h:h + 1, :] for h in range(N_HEADS)], axis=0)
        eye = jnp.where(lax.broadcasted_iota(I32, (tq, tq), 0) == lax.broadcasted_iota(I32, (tq, tq), 1),
                        1.0, 0.0).astype(BF16)
        o_ref[...] = lax.dot_general(eye, on.astype(BF16), NT_DIMS, preferred_element_type=F32).astype(o_ref.dtype)


def _fox_attention(q, k_aug, v_t, sel_q, q0):
    B, Tq, _ = q.shape
    Tk = k_aug.shape[1]
    WA = N_HEADS * HEAD_PAD
    tq = _largest_tile(Tq, 384)
    nk, tk = v_t.shape[1], v_t.shape[3]
    assert nk * tk == Tk
    nq = Tq // tq
    last_tile = lambda qi: jnp.minimum((q0 + (qi + 1) * tq - 1) // tk, nk - 1)
    qspec = lambda w: pl.BlockSpec((None, tq, w), lambda b, qi, ki: (b, qi, 0))
    return pl.pallas_call(
        functools.partial(_fox_kernel, q0=q0, tq=tq, tk=tk, nk=nk),
        grid=(B, nq, nk),
        in_specs=[qspec(HW),
                  pl.BlockSpec((None, tk, WA), lambda b, qi, ki: (b, jnp.minimum(ki, last_tile(qi)), 0)),
                  pl.BlockSpec((None, None, HW, tk), lambda b, qi, ki: (b, jnp.minimum(ki, last_tile(qi)), 0, 0)),
                  pl.BlockSpec(sel_q.shape, lambda b, qi, ki: (0, 0))],
        out_specs=qspec(HW),
        out_shape=jax.ShapeDtypeStruct((B, Tq, HW), BF16),
        scratch_shapes=[pltpu.VMEM((tq, WA), BF16), pltpu.VMEM((N_HEADS, tq), F32),
                        pltpu.VMEM((N_HEADS, tq), F32), pltpu.VMEM((HW, tq), F32)],
        compiler_params=_params("arbitrary", "arbitrary", "arbitrary"),
        name="fox_attention",
    )(q, k_aug, v_t, sel_q)


def _rwkv_kernel(rkv_ref, lmid_ref, rkv0_ref, s0_ref, mu_ref, w0_ref, w2_ref, a0_ref, a2_ref, g2_ref,
                 kk_ref, ka_ref, rk_ref, gng_ref, gnb_ref, o_ref, sfin_ref, state_ref, carry_ref, *, chunk, levels):
    C = chunk
    T = rkv_ref.shape[0]
    state_ref[...] = s0_ref[...]
    carry_ref[...] = rkv0_ref[...]
    row_w = lax.broadcasted_iota(I32, (C, 3 * HW), 0)
    row_h = lax.broadcasted_iota(I32, (C, HW), 0)
    r_i = lax.broadcasted_iota(I32, (C, C), 0)
    c_i = lax.broadcasted_iota(I32, (C, C), 1)
    strict = r_i > c_i
    incl = r_i >= c_i
    mid = C // 2 - 1 if C > 1 else 0

    def chunk_body(i, carry):
        off = pl.multiple_of(i * C, C)
        x = rkv_ref[pl.ds(off, C), :].astype(F32)
        prev = jnp.where(row_w == 0, carry_ref[...], pltpu.roll(x, 1, axis=0))
        carry_ref[...] = x[C - 1:C, :]
        x = x + (prev - x) * mu_ref[...]
        r, k0, v = x[:, 0:HW], x[:, HW:2 * HW], x[:, 2 * HW:3 * HW]
        lm = lmid_ref[pl.ds(off, C), :]
        w_pre = w0_ref[...] + _dot(jnp.tanh(lm[:, 0:64]).astype(BF16), w2_ref[...])
        a = _sigmoid(a0_ref[...] + _dot(lm[:, 64:128].astype(BF16), a2_ref[...]))
        g = _dot(_sigmoid(lm[:, 128:256]).astype(BF16), g2_ref[...])
        w_log = -_softplus(-w_pre) - 0.5
        logdec = -jnp.exp(w_log)
        L = logdec
        sh = 1
        while sh < C:
            L = L + jnp.where(row_h >= sh, pltpu.roll(L, sh, axis=0), 0.0)
            sh *= 2
        l_mid = L[mid:mid + 1, :]
        l_tot = L[C - 1:C, :]
        e_a = jnp.exp(L - logdec - l_mid)
        e_r = jnp.exp(L - l_mid)
        e_k = jnp.exp(l_mid - L)
        e_s = jnp.exp(l_tot - L)
        w_tot = jnp.exp(l_tot)
        e_mid = jnp.exp(l_mid)
        kk_raw = k0 * kk_ref[...]
        k = k0 * (1.0 + (a - 1.0) * ka_ref[...])
        rk = rk_ref[...]
        H = range(N_HEADS)
        hsl = [slice(h * HEAD_DIM, (h + 1) * HEAD_DIM) for h in H]
        nt = lambda x, y: lax.dot_general(x, y, NT_DIMS, preferred_element_type=F32)
        tn = lambda x, y: lax.dot_general(x, y, TN_DIMS, preferred_element_type=F32)
        kkh = [kk_raw[:, hs] for hs in hsl]
        kkh = [x * lax.rsqrt(jnp.maximum(jnp.sum(x * x, -1, keepdims=True), 1e-24)) for x in kkh]
        r_h = [r[:, hs] for hs in hsl]
        k_h = [k[:, hs] for hs in hsl]
        v_h = [v[:, hs] for hs in hsl]
        b_h = [kkh[h] * a[:, hsl[h]] for h in H]
        ar = [jnp.concatenate([-kkh[h] * e_a[:, hsl[h]], r_h[h] * e_r[:, hsl[h]]], axis=0) for h in H]
        bk = [b_h[h] * e_k[:, hsl[h]] for h in H]
        kd = [k_h[h] * e_k[:, hsl[h]] for h in H]
        s_old = [state_ref[h] for h in H]
        g_b = [nt(ar[h], bk[h]) for h in H]
        g_k = [nt(ar[h], kd[h]) for h in H]
        x0 = [nt(ar[h], s_old[h] * e_mid[:, hsl[h]]) for h in H]
        a_mat = [jnp.where(strict, g_b[h][0:C], 0.0) for h in H]
        u = [x0[h][0:C] + _dot(jnp.where(strict, g_k[h][0:C], 0.0), v_h[h]) for h in H]
        for lvl in range(levels):
            au = [_dot(a_mat[h], u[h]) for h in H]
            if lvl + 1 < levels:
                a_mat = [_dot(a_mat[h], a_mat[h]) for h in H]
            u = [u[h] + au[h] for h in H]
        y = [x0[h][C:2 * C] + _dot(jnp.where(incl, g_b[h][C:2 * C], 0.0), u[h])
             + _dot(jnp.where(incl, g_k[h][C:2 * C], 0.0), v_h[h]) for h in H]
        for h in H:
            uv = jnp.concatenate([u[h], v_h[h]], axis=0)
            bks = jnp.concatenate([b_h[h] * e_s[:, hsl[h]], k_h[h] * e_s[:, hsl[h]]], axis=0)
            state_ref[h] = s_old[h] * w_tot[:, hsl[h]] + tn(uv, bks)
        outs = []
        for h in H:
            mu = jnp.mean(y[h], -1, keepdims=True)
            yc = y[h] - mu
            var = jnp.mean(yc * yc, -1, keepdims=True)
            bonus = jnp.sum(r_h[h] * k_h[h] * rk[:, hsl[h]], -1, keepdims=True) * v_h[h]
            outs.append((yc * lax.rsqrt(var + GN_EPS), bonus))
        yn = jnp.concatenate([o[0] for o in outs], axis=1)
        bonus = jnp.concatenate([o[1] for o in outs], axis=1)
        out = (yn * gng_ref[...] + gnb_ref[...] + bonus) * g
        o_ref[pl.ds(off, C), :] = out.astype(o_ref.dtype)
        return carry

    lax.fori_loop(0, T // C, chunk_body, 0)
    sfin_ref[...] = state_ref[...]


def _rwkv(rkv, lmid, rkv0, s0, mu_rkv, w0, w2, a0, a2, g2, k_k, k_a, r_k, gn_g, gn_b):
    B, T, _ = rkv.shape
    chunk = _largest_tile(T, 64, mult=16)
    levels = max(1, math.ceil(math.log2(chunk)))
    seq = lambda w: pl.BlockSpec((None, T, w), lambda b: (b, 0, 0))
    st = pl.BlockSpec((None, N_HEADS, HEAD_DIM, HEAD_DIM), lambda b: (b, 0, 0, 0))
    vec = lambda a: _full(a.shape)
    params = (mu_rkv, w0, w2, a0, a2, g2, k_k, k_a, r_k, gn_g, gn_b)
    return pl.pallas_call(
        functools.partial(_rwkv_kernel, chunk=chunk, levels=levels),
        grid=(B,),
        in_specs=[seq(3 * HW), seq(256), pl.BlockSpec((None, 1, 3 * HW), lambda b: (b, 0, 0)), st]
                 + [vec(p) for p in params],
        out_specs=[seq(HW), st],
        out_shape=[jax.ShapeDtypeStruct((B, T, HW), BF16),
                   jax.ShapeDtypeStruct((B, N_HEADS, HEAD_DIM, HEAD_DIM), F32)],
        scratch_shapes=[pltpu.VMEM((N_HEADS, HEAD_DIM, HEAD_DIM), F32), pltpu.VMEM((1, 3 * HW), F32)],
        compiler_params=_params("arbitrary"),
        name="rwkv7",
    )(rkv, lmid, rkv0, s0, *params)


def _merge_kernel(fox_ref, rw_ref, gate_ref, h_ref, wa_ref, wb_ref, wo_ref, g_ref, b_ref, wrh_ref, wrl_ref, br_ref,
                  h1_ref, h1p_ref, idx_ref, gt_ref, rank_ref, cnt_ref, carry_ref, *, dn_alpha):
    i = pl.program_id(0)
    tm, D = h_ref.shape

    @pl.when(i == 0)
    def _():
        carry_ref[...] = jnp.zeros_like(carry_ref)

    gates = gate_ref[...].astype(F32)
    merged = gates[:, 0:D] * _dot(fox_ref[...], wa_ref[...]) + gates[:, D:2 * D] * _dot(rw_ref[...], wb_ref[...])
    z = dn_alpha * h_ref[...] + _dot(merged.astype(BF16), wo_ref[...])
    h1 = _layer_norm(z, g_ref[...], b_ref[...])
    h1_ref[...] = h1
    h1p_ref[...] = _pack_bf16_pair(h1)
    hi = h1.astype(BF16)
    lo = (h1 - hi.astype(F32)).astype(BF16)
    logits = _dot(hi, wrh_ref[...]) + _dot(hi, wrl_ref[...]) + _dot(lo, wrh_ref[...]) + br_ref[...]
    lane = lax.broadcasted_iota(I32, (tm, LANES), 1)
    lane_f = lane.astype(F32)
    cur = logits
    vals, idxs = [], []
    for _ in range(TOP_K):
        m = jnp.max(cur, -1, keepdims=True)
        ix = jnp.min(jnp.where(cur == m, lane_f, float(LANES)), -1, keepdims=True)
        vals.append(m)
        idxs.append(ix)
        cur = jnp.where(lane_f == ix, -3e38, cur)
    exps = [jnp.exp(vk - vals[0]) for vk in vals]
    denom = exps[0] + exps[1] + exps[2] + exps[3]
    onehot = jnp.zeros((tm, LANES), F32)
    for ix in idxs:
        onehot = onehot + jnp.where(lane_f == ix, 1.0, 0.0)
    r_i = lax.broadcasted_iota(I32, (tm, tm), 0)
    c_i = lax.broadcasted_iota(I32, (tm, tm), 1)
    tri = jnp.where(r_i > c_i, 1.0, 0.0).astype(BF16)
    before = _dot(tri, onehot.astype(BF16)) + carry_ref[...]
    idx_out = jnp.zeros((tm, LANES), F32)
    gt_out = jnp.zeros((tm, LANES), F32)
    rank_out = jnp.zeros((tm, LANES), F32)
    for kx in range(TOP_K):
        rank_k = jnp.sum(jnp.where(lane_f == idxs[kx], before, 0.0), -1, keepdims=True)
        idx_out = jnp.where(lane == kx, idxs[kx], idx_out)
        gt_out = jnp.where(lane == kx, exps[kx] / denom, gt_out)
        rank_out = jnp.where(lane == kx, rank_k, rank_out)
    idx_ref[...] = idx_out.astype(I32)
    gt_ref[...] = gt_out
    rank_ref[...] = rank_out.astype(I32)
    total = carry_ref[...] + jnp.sum(onehot, 0, keepdims=True)
    carry_ref[...] = total
    cnt_ref[...] = total.astype(I32)


def _merge_route(fox, rw, gates, h, wa, wb, wo, ln_g, ln_b, wr_hi, wr_lo, br, dn_alpha):
    N, D = h.shape
    tm = _largest_tile(N, 256)
    tile = lambda w: pl.BlockSpec((tm, w), lambda i: (i, 0))
    return pl.pallas_call(
        functools.partial(_merge_kernel, dn_alpha=dn_alpha),
        grid=(N // tm,),
        in_specs=[tile(HW), tile(HW), tile(2 * D), tile(D), _full(wa.shape), _full(wb.shape), _full(wo.shape),
                  _full((1, D)), _full((1, D)), _full(wr_hi.shape), _full(wr_lo.shape), _full(br.shape)],
        out_specs=[tile(D), tile(D // 2), tile(LANES), tile(LANES), tile(LANES), _full((1, LANES))],
        out_shape=[jax.ShapeDtypeStruct((N, D), F32), jax.ShapeDtypeStruct((N, D // 2), U32),
                   jax.ShapeDtypeStruct((N, LANES), I32),
                   jax.ShapeDtypeStruct((N, LANES), F32), jax.ShapeDtypeStruct((N, LANES), I32),
                   jax.ShapeDtypeStruct((1, LANES), I32)],
        scratch_shapes=[pltpu.VMEM((1, LANES), F32)],
        compiler_params=_params("arbitrary"),
        name="merge_route",
    )(fox, rw, gates, h, wa, wb, wo, ln_g, ln_b, wr_hi, wr_lo, br)


PERM_W = 256


def _deinterleave_kernel(w_ref, g_ref, l_ref):
    half = PERM_W // 2
    ii = lax.broadcasted_iota(I32, (PERM_W, PERM_W), 0)
    jj = lax.broadcasted_iota(I32, (PERM_W, PERM_W), 1)
    src = jnp.where(jj < half, 2 * jj, 2 * (jj - half) + 1)
    perm = jnp.where(ii == src, 1.0, 0.0).astype(BF16)
    for c in range(w_ref.shape[1] // PERM_W):
        w = w_ref[:, c * PERM_W:(c + 1) * PERM_W].astype(BF16)
        out = _dot(w, perm)
        g_ref[:, c * half:(c + 1) * half] = out[:, :half].astype(BF16)
        l_ref[:, c * half:(c + 1) * half] = out[:, half:].astype(BF16)


def _deinterleave_w1(w_e1):
    E, D, F2 = w_e1.shape
    wc = 512
    out = jax.ShapeDtypeStruct((E, D, F2 // 2), BF16)
    return pl.pallas_call(
        _deinterleave_kernel,
        grid=(E, F2 // wc),
        in_specs=[pl.BlockSpec((None, D, wc), lambda e, c: (e, 0, c))],
        out_specs=[pl.BlockSpec((None, D, wc // 2), lambda e, c: (e, 0, c))] * 2,
        out_shape=[out, out],
        compiler_params=_params("arbitrary", "arbitrary"),
        name="w1_deinterleave",
    )(w_e1)


def _dispatch_kernel(pad_ref, dest_ref, x_ref, xs_hbm, zero_ref, sem, *, n_pad):
    i = pl.program_id(0)
    tm = x_ref.shape[0]

    if n_pad:
        @pl.when(i == 0)
        def _():
            zero_ref[...] = jnp.zeros_like(zero_ref)

            def zbody(r, c):
                pltpu.make_async_copy(zero_ref.at[pl.ds(0, 1)], xs_hbm.at[pl.ds(pad_ref[r], 1)], sem.at[1]).start()
                return c
            lax.fori_loop(0, n_pad, zbody, 0, unroll=8)
            for _ in range(n_pad // tm):
                pltpu.make_async_copy(zero_ref, xs_hbm.at[pl.ds(0, tm)], sem.at[1]).wait()

    def body(r, c):
        for kx in range(TOP_K):
            d = dest_ref[0, 0, r * TOP_K + kx]
            pltpu.make_async_copy(x_ref.at[pl.ds(r, 1)], xs_hbm.at[pl.ds(d, 1)], sem.at[0]).start()
        return c
    lax.fori_loop(0, tm, body, 0, unroll=8)
    for _ in range(TOP_K):
        pltpu.make_async_copy(x_ref, xs_hbm.at[pl.ds(0, tm)], sem.at[0]).wait()


def _moe_dispatch(xp, dest, pad_slots, rows):
    N, W = xp.shape
    tm = _largest_tile(N, 256)
    n = N // tm
    n_pad = pad_slots.shape[0]
    assert n_pad % tm == 0, (n_pad, tm)
    grid_spec = pltpu.PrefetchScalarGridSpec(
        num_scalar_prefetch=1,
        grid=(n,),
        in_specs=[pl.BlockSpec((1, 1, tm * TOP_K), lambda i, pad: (i, 0, 0), memory_space=pltpu.SMEM),
                  pl.BlockSpec((tm, W), lambda i, pad: (i, 0))],
        out_specs=pl.BlockSpec(memory_space=pl.ANY),
        scratch_shapes=[pltpu.VMEM((tm, W), U32), pltpu.SemaphoreType.DMA((2,))],
    )
    return pl.pallas_call(
        functools.partial(_dispatch_kernel, n_pad=n_pad),
        grid_spec=grid_spec,
        out_shape=jax.ShapeDtypeStruct((rows, W), U32),
        compiler_params=_params("arbitrary"),
        name="moe_dispatch",
    )(pad_slots, dest.reshape(n, 1, tm * TOP_K), xp)


def _moe_kernel(be_ref, nused_ref, xs_ref, w1g_ref, w1l_ref, b1g_ref, b1l_ref, w2_ref, b2_ref, y_ref):
    j = pl.program_id(0)
    nused = nused_ref[0]

    @pl.when(j < nused)
    def _():
        lo, hi = _unpack_bf16_pair(xs_ref[...])
        x = jnp.concatenate([lo, hi], axis=1).astype(BF16)
        glu = jnp.minimum(_dot(x, w1g_ref[...]) + b1g_ref[...], SWIGLU_LIMIT)
        lin = jnp.clip(_dot(x, w1l_ref[...]) + b1l_ref[...], -SWIGLU_LIMIT, SWIGLU_LIMIT)
        act = glu * _sigmoid(SWIGLU_ALPHA * glu) * (lin + 1.0)
        y_ref[...] = _pack_bf16_pair(_dot(act.astype(BF16), w2_ref[...]) + b2_ref[...])

    @pl.when(j >= nused)
    def _():
        y_ref[...] = jnp.zeros_like(y_ref)


def _moe_experts(xs, blk_e, nused, w1g, w1l, b1g, b1l, w2, b2, bm):
    rows, W = xs.shape
    nb = rows // bm
    D, F = w1g.shape[1], w1g.shape[2]
    last = lambda j, be, nu: jnp.minimum(j, jnp.maximum(nu[0] - 1, 0))
    wspec = lambda k, n: pl.BlockSpec((None, k, n), lambda j, be, nu: (be[last(j, be, nu)], 0, 0))
    grid_spec = pltpu.PrefetchScalarGridSpec(
        num_scalar_prefetch=2,
        grid=(nb,),
        in_specs=[pl.BlockSpec((bm, W), lambda j, be, nu: (j, 0)),
                  wspec(D, F), wspec(D, F), wspec(1, F), wspec(1, F), wspec(F, D), wspec(1, D)],
        out_specs=pl.BlockSpec((bm, W), lambda j, be, nu: (j, 0)),
    )
    return pl.pallas_call(
        _moe_kernel,
        grid_spec=grid_spec,
        out_shape=jax.ShapeDtypeStruct((rows, W), U32),
        compiler_params=_params("arbitrary"),
        name="moe_experts",
    )(blk_e, nused, xs, w1g, w1l, b1g, b1l, w2, b2)


def _combine_gather_start(dest_ref, y_hbm, buf, sem, slot, tm):
    def body(r, c):
        for kx in range(TOP_K):
            d = dest_ref[0, 0, r * TOP_K + kx]
            pltpu.make_async_copy(y_hbm.at[pl.ds(d, 1)], buf.at[slot, kx, pl.ds(r, 1)], sem.at[slot]).start()
        return c
    lax.fori_loop(0, tm, body, 0, unroll=8)


def _combine_kernel(dest_ref, destn_ref, gt_ref, h1_ref, g_ref, b_ref, y_hbm, o_ref, buf, sem, *, dn_alpha):
    i = pl.program_id(0)
    n = pl.num_programs(0)
    tm = h1_ref.shape[0]
    slot = i % 2

    @pl.when(i == 0)
    def _():
        _combine_gather_start(dest_ref, y_hbm, buf, sem, 0, tm)

    @pl.when(i + 1 < n)
    def _():
        _combine_gather_start(destn_ref, y_hbm, buf, sem, 1 - slot, tm)

    for kx in range(TOP_K):
        pltpu.make_async_copy(y_hbm.at[pl.ds(0, tm)], buf.at[slot, kx], sem.at[slot]).wait()
    gt = gt_ref[...]
    lo, hi = _unpack_bf16_pair(buf[slot, 0])
    moe_lo, moe_hi = gt[:, 0:1] * lo, gt[:, 0:1] * hi
    for kx in range(1, TOP_K):
        lo, hi = _unpack_bf16_pair(buf[slot, kx])
        moe_lo = moe_lo + gt[:, kx:kx + 1] * lo
        moe_hi = moe_hi + gt[:, kx:kx + 1] * hi
    moe = jnp.concatenate([moe_lo, moe_hi], axis=1)
    o_ref[...] = _layer_norm(dn_alpha * h1_ref[...] + moe, g_ref[...], b_ref[...])


def _moe_combine(dest, gate, h1, ln_g, ln_b, yb, dn_alpha):
    N, D = h1.shape
    tm = _largest_tile(N, 256)
    n = N // tm
    dest3 = dest.reshape(n, 1, tm * TOP_K)
    tile = lambda w: pl.BlockSpec((tm, w), lambda i: (i, 0))
    return pl.pallas_call(
        functools.partial(_combine_kernel, dn_alpha=dn_alpha),
        grid=(n,),
        in_specs=[
            pl.BlockSpec((1, 1, tm * TOP_K), lambda i: (i, 0, 0), memory_space=pltpu.SMEM),
            pl.BlockSpec((1, 1, tm * TOP_K), lambda i: (jnp.minimum(i + 1, n - 1), 0, 0), memory_space=pltpu.SMEM),
            tile(LANES), tile(D), _full((1, D)), _full((1, D)),
            pl.BlockSpec(memory_space=pl.ANY),
        ],
        out_specs=tile(D),
        out_shape=jax.ShapeDtypeStruct((N, D), F32),
        scratch_shapes=[pltpu.VMEM((2, TOP_K, tm, yb.shape[1]), U32), pltpu.SemaphoreType.DMA((2,))],
        compiler_params=_params("arbitrary"),
        name="moe_combine",
    )(dest3, dest3, gate, h1, ln_g, ln_b, yb)


def _route_tables(top_idx, rank, counts, n_experts):
    n_tok = top_idx.shape[0]
    n_asg = n_tok * TOP_K
    bm = min(256, max(8, 1 << int(math.log2(max(1, n_asg // n_experts)))))
    nb = -(-n_asg // bm) + n_experts
    padded = (counts + bm - 1) // bm * bm
    pends = jnp.cumsum(padded)
    dest = (pends - padded)[top_idx] + rank
    blk_start = jnp.arange(nb, dtype=I32) * bm
    blk_e = jnp.minimum(jnp.sum(pends[None, :] <= blk_start[:, None], axis=1), n_experts - 1).astype(I32)
    nused = (pends[-1] // bm).astype(I32).reshape(1)
    n_pad = nb * bm - n_asg
    gap = padded - counts
    gap_end = jnp.cumsum(gap)
    i = jnp.arange(n_pad, dtype=I32)
    e = jnp.sum(gap_end[None, :] <= i[:, None], axis=1)
    ec = jnp.minimum(e, n_experts - 1)
    in_group = (pends - padded + counts)[ec] + i - (gap_end - gap)[ec]
    pad_slots = jnp.where(e < n_experts, in_group, pends[-1] + i - gap_end[-1]).astype(I32)
    return dest.astype(I32), pad_slots, blk_e, nused, bm, nb * bm


def _stream(x, prev_row, s0, past_k, past_v, past_logf, wts):
    B, T, D = x.shape
    dn_alpha = wts['dn_alpha']
    h, q, k, v, rkv, gates, lmid, logf, rkv0 = _inproj(
        x, prev_row, wts['ln0_g'], wts['ln0_b'], wts['wm'], wts['wff'], wts['bff'], wts['mu3'],
        wts['w1'], wts['a1'], wts['g1'])
    if past_k is None:
        k_aug, v_bf = _fox_prep(logf, k, v, wts['sel_k'], wts['sel_c'])
        fox = _fox_attention(q, k_aug, v_bf, wts['sel_q'], 0)
    else:
        P = past_k.shape[1]
        past_pad = jnp.pad(past_logf.astype(F32), ((0, 0), (0, 0), (0, LANES - N_HEADS)))
        k_aug, v_bf = _fox_prep(jnp.concatenate([past_pad, logf], axis=1),
                                jnp.concatenate([past_k.reshape(B, P, HW), k], axis=1),
                                jnp.concatenate([past_v.reshape(B, P, HW), v], axis=1), wts['sel_k'], wts['sel_c'])
        fox = _fox_attention(q, k_aug, v_bf, wts['sel_q'], P)
    rw, s_fin = _rwkv(rkv, lmid, rkv0, s0, wts['mu_rkv'], wts['w0'], wts['w2'], wts['a0'], wts['a2'], wts['g2'],
                      wts['k_k'], wts['k_a'], wts['r_k'], wts['gn_g'], wts['gn_b'])
    N = B * T
    h1, h1p, top_idx, gate, rank, counts = _merge_route(
        fox.reshape(N, HW), rw.reshape(N, HW), gates.reshape(N, 2 * D), h.reshape(N, D),
        wts['w_up_a'], wts['w_up_b'], wts['w_out'], wts['ln1_g'], wts['ln1_b'],
        wts['wr_hi'], wts['wr_lo'], wts['br'], dn_alpha)
    n_experts = wts['n_experts']
    dest, pad_slots, blk_e, nused, bm, rows = _route_tables(top_idx[:, :TOP_K], rank[:, :TOP_K],
                                                            counts[0, :n_experts], n_experts)
    xs = _moe_dispatch(h1p, dest, pad_slots, rows)
    yb = _moe_experts(xs, blk_e, nused, wts['w1g'], wts['w1l'], wts['b1g'], wts['b1l'], wts['we2'], wts['be2'], bm)
    y = _moe_combine(dest, gate, h1, wts['ln2_g'], wts['ln2_b'], yb, dn_alpha)
    new_k = k.reshape(B, T, N_HEADS, HEAD_DIM)
    new_v = v.reshape(B, T, N_HEADS, HEAD_DIM)
    return y.reshape(B, T, D), new_k, new_v, logf[:, :, :N_HEADS], s_fin, h[:, T - 1:T, :]


def kernel(x_prompt, x_sample, cache_fox_k, cache_fox_v, cache_fox_logf, state_rwkv, state_shift, meta, ln0_g, ln0_b, w_in, b_forget, mu_w, mu_a, mu_g, mu_rkv, w0, w1, w2, a0, a1, a2, g1, g2, k_k, k_a, r_k, gn_g, gn_b, w_up_a, w_up_b, w_out, ln1_g, ln1_b, w_router, b_router, w_e1, b_e1, w_e2, b_e2, ln2_g, ln2_b):
    depth, D, in_cols = w_in.shape
    assert depth == 1 and D == 1024 and in_cols == 6 * HW + N_HEADS + 2 * D
    n_experts = w_router.shape[2]
    assert n_experts <= LANES
    B = x_prompt.shape[0]
    l = 0
    w = w_in[l]
    off_ff = 3 * HW
    row = lambda a: a.reshape(1, -1).astype(F32)
    wr = jnp.pad(w_router[l], ((0, 0), (0, LANES - n_experts)))
    wr_hi = wr.astype(BF16)
    w1g, w1l = _deinterleave_w1(w_e1[l])
    sel_k, sel_q, sel_c = _aug_select_matrices()
    wts = dict(
        sel_k=sel_k, sel_q=sel_q, sel_c=sel_c,
        dn_alpha=float((2 * depth) ** 0.25), n_experts=n_experts,
        ln0_g=row(ln0_g), ln0_b=row(ln0_b),
        wm=jnp.concatenate([w[:, :off_ff], w[:, off_ff + N_HEADS:]], axis=1).astype(BF16),
        wff=jnp.pad(w[:, off_ff:off_ff + N_HEADS], ((0, 0), (0, LANES - N_HEADS))).astype(BF16),
        bff=jnp.pad(row(b_forget[l]), ((0, 0), (0, LANES - N_HEADS))),
        mu3=jnp.stack([mu_w[l], mu_a[l], mu_g[l]], axis=0),
        w1=w1[l].astype(BF16), a1=a1[l].astype(BF16), g1=g1[l].astype(BF16),
        mu_rkv=row(mu_rkv[l]), w0=row(w0[l]), w2=w2[l].astype(BF16), a0=row(a0[l]), a2=a2[l].astype(BF16),
        g2=g2[l].astype(BF16), k_k=row(k_k[l]), k_a=row(k_a[l]), r_k=row(r_k[l]), gn_g=row(gn_g[l]),
        gn_b=row(gn_b[l]),
        w_up_a=w_up_a[l].astype(BF16), w_up_b=w_up_b[l].astype(BF16), w_out=w_out[l].astype(BF16),
        ln1_g=row(ln1_g[l]), ln1_b=row(ln1_b[l]),
        wr_hi=wr_hi, wr_lo=(wr - wr_hi.astype(F32)).astype(BF16),
        br=jnp.pad(row(b_router[l]), ((0, 0), (0, LANES - n_experts)), constant_values=NEG_BIG),
        w1g=w1g, w1l=w1l,
        b1g=b_e1[l][:, None, 0::2], b1l=b_e1[l][:, None, 1::2],
        we2=w_e2[l].astype(BF16), be2=b_e2[l][:, None, :],
        ln2_g=row(ln2_g[l]), ln2_b=row(ln2_b[l]),
    )
    meta_b = jnp.broadcast_to(meta, (B, N_META, D)).astype(x_prompt.dtype)
    xp = jnp.concatenate([meta_b, x_prompt], axis=1)
    zero_row = jnp.zeros((B, 1, D), F32)
    zero_state = jnp.zeros((B, N_HEADS, HEAD_DIM, HEAD_DIM), F32)
    y_p, k_p, v_p, lf_p, s_p, sh_p = _stream(xp, zero_row, zero_state, None, None, None, wts)
    y_s, k_s, v_s, lf_s, s_s, sh_s = _stream(x_sample, state_shift[l], state_rwkv[l], cache_fox_k[l],
                                             cache_fox_v[l], cache_fox_logf[l], wts)
    ex = lambda a: a[None]
    return (y_p[:, N_META:], y_s, ex(k_p), ex(v_p), ex(lf_p), ex(s_p), ex(sh_p),
            ex(k_s), ex(v_s), ex(lf_s), ex(s_s), ex(sh_s))
```

```python
import functools
import math

import jax
import jax.numpy as jnp
from jax import lax
from jax.experimental import pallas as pl
from jax.experimental.pallas import tpu as pltpu

F32 = jnp.float32
BF16 = jnp.bfloat16
I32 = jnp.int32
U32 = jnp.uint32

N_META = 16
HEAD_DIM = 64
N_HEADS = 8
HW = N_HEADS * HEAD_DIM
TOP_K = 4
SWIGLU_LIMIT = 7.0
SWIGLU_ALPHA = 1.702
LN_EPS = 1e-5
GN_EPS = 64e-5
LANES = 128
NEG_BIG = -1e30
VMEM_LIMIT_BYTES = 56 * 1024 * 1024
HIGHEST = lax.Precision.HIGHEST

NT_DIMS = (((1,), (1,)), ((), ()))
TN_DIMS = (((0,), (0,)), ((), ()))


def _params(*sem):
    return pltpu.CompilerParams(dimension_semantics=sem, vmem_limit_bytes=VMEM_LIMIT_BYTES)


def _largest_tile(n, cap, mult=8):
    best = None
    for d in range(mult, min(n, cap) + 1, mult):
        if n % d == 0:
            best = d
    assert best is not None, (n, cap, mult)
    return best


def _sigmoid(x):
    return 1.0 / (1.0 + jnp.exp(-x))


def _softplus(x):
    return jnp.maximum(x, 0.0) + jnp.log1p(jnp.exp(-jnp.abs(x)))


def _layer_norm(x, g, b):
    mu = jnp.mean(x, -1, keepdims=True)
    xc = x - mu
    var = jnp.mean(xc * xc, -1, keepdims=True)
    return xc * lax.rsqrt(var + LN_EPS) * g + b


def _dot(a, b):
    return jnp.dot(a, b, preferred_element_type=F32)


def _full(shape):
    n = len(shape)
    return pl.BlockSpec(shape, lambda *_: (0,) * n)


def _pack_bf16_pair(x):
    w = x.shape[1] // 2
    bits = lambda t: lax.bitcast_convert_type(t.astype(BF16).astype(F32), U32)
    return (bits(x[:, :w]) >> 16) | (bits(x[:, w:]) & jnp.uint32(0xFFFF0000))


def _unpack_bf16_pair(u):
    return lax.bitcast_convert_type(u << 16, F32), lax.bitcast_convert_type(u & jnp.uint32(0xFFFF0000), F32)


C_Q, C_K, C_V, C_RKV, C_GA, C_END = 0, HW, 2 * HW, 3 * HW, 6 * HW, 6 * HW + 2048


def _inproj_kernel(x_ref, prev_ref, g_ref, b_ref, wm_ref, wff_ref, bff_ref, mu_ref, w1_ref, a1_ref, g1_ref,
                   h_ref, q_ref, k_ref, v_ref, rkv_ref, gate_ref, lmid_ref, logf_ref, rkv0_ref,
                   carry_ref):
    t = pl.program_id(1)
    tt = x_ref.shape[0]
    h = _layer_norm(x_ref[...], g_ref[...], b_ref[...])
    h_ref[...] = h

    @pl.when(t == 0)
    def _():
        prev = prev_ref[...]
        carry_ref[...] = prev
        p8 = jnp.broadcast_to(prev, (8, prev.shape[1])).astype(BF16)
        rkv0_ref[...] = _dot(p8, wm_ref[:, C_RKV:C_GA])[0:1]

    rows = lax.broadcasted_iota(I32, h.shape, 0)
    hprev = jnp.where(rows == 0, carry_ref[...], pltpu.roll(h, 1, axis=0))
    carry_ref[...] = h[tt - 1:tt, :]
    dx = hprev - h
    hb = h.astype(BF16)
    q_ref[...] = _dot(hb, wm_ref[:, C_Q:C_K]).astype(BF16)
    k_ref[...] = _dot(hb, wm_ref[:, C_K:C_V])
    v_ref[...] = _dot(hb, wm_ref[:, C_V:C_RKV])
    rkv_ref[...] = _dot(hb, wm_ref[:, C_RKV:C_GA]).astype(BF16)
    gate_ref[...] = _sigmoid(_dot(hb, wm_ref[:, C_GA:C_END])).astype(BF16)
    ff = _dot(hb, wff_ref[...]) + bff_ref[...]
    logf_ref[...] = -_softplus(-ff)
    mu = mu_ref[...]
    lmid_ref[:, 0:64] = _dot((h + dx * mu[0:1]).astype(BF16), w1_ref[...])
    lmid_ref[:, 64:128] = _dot((h + dx * mu[1:2]).astype(BF16), a1_ref[...])
    lmid_ref[:, 128:256] = _dot((h + dx * mu[2:3]).astype(BF16), g1_ref[...])


def _inproj(x, prev_row, ln_g, ln_b, wm, wff, bff, mu3, w1, a1, g1):
    B, T, D = x.shape
    tt = _largest_tile(T, 384)
    nt = T // tt
    tile = lambda w: pl.BlockSpec((None, tt, w), lambda b, t: (b, t, 0))
    row = lambda w: pl.BlockSpec((None, 1, w), lambda b, t: (b, 0, 0))
    out_shape = [
        jax.ShapeDtypeStruct((B, T, D), F32),
        jax.ShapeDtypeStruct((B, T, HW), BF16),
        jax.ShapeDtypeStruct((B, T, HW), F32),
        jax.ShapeDtypeStruct((B, T, HW), F32),
        jax.ShapeDtypeStruct((B, T, 3 * HW), BF16),
        jax.ShapeDtypeStruct((B, T, 2 * D), BF16),
        jax.ShapeDtypeStruct((B, T, 256), F32),
        jax.ShapeDtypeStruct((B, T, LANES), F32),
        jax.ShapeDtypeStruct((B, 1, 3 * HW), F32),
    ]
    return pl.pallas_call(
        _inproj_kernel,
        grid=(B, nt),
        in_specs=[tile(D), row(D), _full((1, D)), _full((1, D)), _full(wm.shape), _full(wff.shape),
                  _full(bff.shape), _full(mu3.shape), _full(w1.shape), _full(a1.shape), _full(g1.shape)],
        out_specs=[tile(D), tile(HW), tile(HW), tile(HW), tile(3 * HW), tile(2 * D), tile(256), tile(LANES),
                   row(3 * HW)],
        out_shape=out_shape,
        scratch_shapes=[pltpu.VMEM((1, D), F32)],
        compiler_params=_params("arbitrary", "arbitrary"),
        name="inproj",
    )(x, prev_row, ln_g, ln_b, wm, wff, bff, mu3, w1, a1, g1)


HEAD_PAD = 2 * HEAD_DIM
C_SPLIT = 3


def _aug_select_matrices():
    rows = jnp.arange(HW)
    sel_k = jnp.zeros((HW, N_HEADS * HEAD_PAD), F32).at[rows, (rows // HEAD_DIM) * HEAD_PAD + rows % HEAD_DIM].set(1.0)
    p = jnp.repeat(jnp.arange(C_SPLIT), N_HEADS)
    h = jnp.tile(jnp.arange(N_HEADS), C_SPLIT)
    sel_c = jnp.zeros((C_SPLIT * LANES, N_HEADS * HEAD_PAD), F32).at[p * LANES + h, h * HEAD_PAD + HEAD_DIM + p].set(1.0)
    return sel_k.astype(BF16), (sel_k * HEAD_DIM ** -0.5).astype(BF16), sel_c.astype(BF16)


def _fox_prep_kernel(lf_ref, k_ref, v_ref, selk_ref, selc_ref, ka_ref, vt_ref, carry_ref):
    t = pl.program_id(1)
    tt = lf_ref.shape[0]

    @pl.when(t == 0)
    def _():
        carry_ref[...] = jnp.zeros_like(carry_ref)

    r = lax.broadcasted_iota(I32, (tt, tt), 0)
    c = lax.broadcasted_iota(I32, (tt, tt), 1)
    tri = jnp.where(r >= c, 1.0, 0.0).astype(F32)
    cs = jnp.dot(tri, lf_ref[...], preferred_element_type=F32, precision=HIGHEST) + carry_ref[...]
    carry_ref[...] = cs[tt - 1:tt, :]
    neg = -cs
    hi = neg.astype(BF16)
    r1 = neg - hi.astype(F32)
    mid = r1.astype(BF16)
    lo = (r1 - mid.astype(F32)).astype(BF16)
    parts = jnp.concatenate([hi, mid, lo], axis=1)
    ka = _dot(k_ref[...].astype(BF16), selk_ref[...]) + _dot(parts, selc_ref[...])
    ka_ref[...] = ka.astype(BF16)
    ii = lax.broadcasted_iota(I32, (HW, HW), 0)
    jj = lax.broadcasted_iota(I32, (HW, HW), 1)
    eye = jnp.where(ii == jj, 1.0, 0.0).astype(BF16)
    vt_ref[...] = lax.dot_general(eye, v_ref[...].astype(BF16), NT_DIMS, preferred_element_type=F32).astype(BF16)


def _fox_prep(logf, k, v, sel_k, sel_c):
    B, T, _ = k.shape
    tt = _largest_tile(T, 384)
    nt = T // tt
    tile = lambda w: pl.BlockSpec((None, tt, w), lambda b, t: (b, t, 0))
    return pl.pallas_call(
        _fox_prep_kernel,
        grid=(B, nt),
        in_specs=[tile(LANES), tile(HW), tile(HW), _full(sel_k.shape), _full(sel_c.shape)],
        out_specs=[tile(N_HEADS * HEAD_PAD), pl.BlockSpec((None, None, HW, tt), lambda b, t: (b, t, 0, 0))],
        out_shape=[jax.ShapeDtypeStruct((B, T, N_HEADS * HEAD_PAD), BF16),
                   jax.ShapeDtypeStruct((B, nt, HW, tt), BF16)],
        scratch_shapes=[pltpu.VMEM((1, LANES), F32)],
        compiler_params=_params("arbitrary", "arbitrary"),
        name="fox_prep",
    )(logf, k, v, sel_k, sel_c)


def _fox_kernel(qi_ref, ki_ref, last_ref, q_ref, ka_ref, vt_ref, selq_ref, o_ref, qa_ref, m_ref, l_ref, acc_ref,
                *, q0, tq, tk):
    p = pl.program_id(1)
    qi = qi_ref[p]
    ki = ki_ref[p]

    @pl.when(ki == 0)
    def _():
        m_ref[...] = jnp.full_like(m_ref, NEG_BIG)
        l_ref[...] = jnp.zeros_like(l_ref)
        acc_ref[...] = jnp.zeros_like(acc_ref)
        lane = lax.broadcasted_iota(I32, qa_ref.shape, 1) % HEAD_PAD
        ones = jnp.where(jnp.logical_and(lane >= HEAD_DIM, lane < HEAD_DIM + C_SPLIT), 1.0, 0.0)
        qa_ref[...] = (_dot(q_ref[...], selq_ref[...]) + ones).astype(BF16)

    first_q = q0 + qi * tq
    tile_first = ki * tk
    tile_last = tile_first + tk - 1

    def scores(h):
        hp = slice(h * HEAD_PAD, (h + 1) * HEAD_PAD)
        return lax.dot_general(ka_ref[:, hp], qa_ref[:, hp], NT_DIMS, preferred_element_type=F32)

    def tile_update(masked):
        if masked:
            key_pos = tile_first + lax.broadcasted_iota(I32, (tk, tq), 0)
            qry_pos = first_q + lax.broadcasted_iota(I32, (tk, tq), 1)
            bias = jnp.where(qry_pos >= key_pos, 0.0, NEG_BIG)
        m_all = m_ref[...]
        l_all = l_ref[...]
        m_rows, l_rows = [], []
        s_next = scores(0)
        for h in range(N_HEADS):
            hs = slice(h * HEAD_DIM, (h + 1) * HEAD_DIM)
            s = s_next
            if h + 1 < N_HEADS:
                s_next = scores(h + 1)
            if masked:
                s = s + bias
            m_prev = m_all[h:h + 1, :]
            m_new = jnp.maximum(m_prev, jnp.max(s, 0, keepdims=True))
            alpha = jnp.exp(m_prev - m_new)
            p = jnp.exp(s - m_new)
            l_rows.append(alpha * l_all[h:h + 1, :] + jnp.sum(p, 0, keepdims=True))
            m_rows.append(m_new)
            acc_ref[hs, :] = alpha * acc_ref[hs, :] + _dot(vt_ref[hs, :], p.astype(BF16))
        m_ref[...] = jnp.concatenate(m_rows, axis=0)
        l_ref[...] = jnp.concatenate(l_rows, axis=0)

    @pl.when(jnp.logical_and(tile_first <= first_q + tq - 1, tile_last > first_q))
    def _():
        tile_update(True)

    @pl.when(tile_last <= first_q)
    def _():
        tile_update(False)

    @pl.when(last_ref[p] == 1)
    def _():
        on = jnp.concatenate(
            [acc_ref[h * HEAD_DIM:(h + 1) * HEAD_DIM, :] / l_ref[h:h + 1, :] for h in range(N_HEADS)], axis=0)
        eye = jnp.where(lax.broadcasted_iota(I32, (tq, tq), 0) == lax.broadcasted_iota(I32, (tq, tq), 1),
                        1.0, 0.0).astype(BF16)
        o_ref[...] = lax.dot_general(eye, on.astype(BF16), NT_DIMS, preferred_element_type=F32).astype(o_ref.dtype)


def _fox_attention(q, k_aug, v_t, sel_q, q0):
    B, Tq, _ = q.shape
    Tk = k_aug.shape[1]
    WA = N_HEADS * HEAD_PAD
    tq = _largest_tile(Tq, 384)
    nk, tk = v_t.shape[1], v_t.shape[3]
    assert nk * tk == Tk
    nq = Tq // tq
    pairs = [(qi, ki) for qi in range(nq) for ki in range(min((q0 + (qi + 1) * tq - 1) // tk, nk - 1) + 1)]
    qi_tab = jnp.array([p[0] for p in pairs], I32)
    ki_tab = jnp.array([p[1] for p in pairs], I32)
    last_tab = jnp.array([int(i + 1 == len(pairs) or pairs[i + 1][0] != pairs[i][0]) for i in range(len(pairs))], I32)
    qspec = lambda w: pl.BlockSpec((None, tq, w), lambda b, p, qt, kt, lt: (b, qt[p], 0))
    grid_spec = pltpu.PrefetchScalarGridSpec(
        num_scalar_prefetch=3,
        grid=(B, len(pairs)),
        in_specs=[qspec(HW),
                  pl.BlockSpec((None, tk, WA), lambda b, p, qt, kt, lt: (b, kt[p], 0)),
                  pl.BlockSpec((None, None, HW, tk), lambda b, p, qt, kt, lt: (b, kt[p], 0, 0)),
                  pl.BlockSpec(sel_q.shape, lambda b, p, qt, kt, lt: (0, 0))],
        out_specs=qspec(HW),
        scratch_shapes=[pltpu.VMEM((tq, WA), BF16), pltpu.VMEM((N_HEADS, tq), F32),
                        pltpu.VMEM((N_HEADS, tq), F32), pltpu.VMEM((HW, tq), F32)],
    )
    return pl.pallas_call(
        functools.partial(_fox_kernel, q0=q0, tq=tq, tk=tk),
        grid_spec=grid_spec,
        out_shape=jax.ShapeDtypeStruct((B, Tq, HW), BF16),
        compiler_params=_params("arbitrary", "arbitrary"),
        name="fox_attention",
    )(qi_tab, ki_tab, last_tab, q, k_aug, v_t, sel_q)


def _rwkv_kernel(rkv_ref, lmid_ref, rkv0_ref, s0_ref, mu_ref, w0_ref, w2_ref, a0_ref, a2_ref, g2_ref,
                 kk_ref, ka_ref, rk_ref, gng_ref, gnb_ref, o_ref, sfin_ref, state_ref, carry_ref, *, chunk, levels):
    C = chunk
    NB, T = rkv_ref.shape[0], rkv_ref.shape[1]
    state_ref[...] = s0_ref[...]
    carry_ref[...] = rkv0_ref[...]
    row_w = lax.broadcasted_iota(I32, (C, 3 * HW), 0)
    row_h = lax.broadcasted_iota(I32, (C, HW), 0)
    r_i = lax.broadcasted_iota(I32, (C, C), 0)
    c_i = lax.broadcasted_iota(I32, (C, C), 1)
    strict = r_i > c_i
    incl = r_i >= c_i
    mid = C // 2 - 1 if C > 1 else 0
    hsl = [slice(h * HEAD_DIM, (h + 1) * HEAD_DIM) for h in range(N_HEADS)]
    nt = lambda x, y: lax.dot_general(x, y, NT_DIMS, preferred_element_type=F32)
    tn = lambda x, y: lax.dot_general(x, y, TN_DIMS, preferred_element_type=F32)

    def row_inputs(bb, off):
        x = rkv_ref[bb, pl.ds(off, C), :].astype(F32)
        prev = jnp.where(row_w == 0, carry_ref[bb], pltpu.roll(x, 1, axis=0))
        carry_ref[bb] = x[C - 1:C, :]
        x = x + (prev - x) * mu_ref[...]
        r, k0, v = x[:, 0:HW], x[:, HW:2 * HW], x[:, 2 * HW:3 * HW]
        lm = lmid_ref[bb, pl.ds(off, C), :]
        w_pre = w0_ref[...] + _dot(jnp.tanh(lm[:, 0:64]).astype(BF16), w2_ref[...])
        a = _sigmoid(a0_ref[...] + _dot(lm[:, 64:128].astype(BF16), a2_ref[...]))
        g = _dot(_sigmoid(lm[:, 128:256]).astype(BF16), g2_ref[...])
        w_log = -_softplus(-w_pre) - 0.5
        logdec = -jnp.exp(w_log)
        L = logdec
        sh = 1
        while sh < C:
            L = L + jnp.where(row_h >= sh, pltpu.roll(L, sh, axis=0), 0.0)
            sh *= 2
        l_mid = L[mid:mid + 1, :]
        l_tot = L[C - 1:C, :]
        return dict(r=r, v=v, a=a, g=g, kk_raw=k0 * kk_ref[...], k=k0 * (1.0 + (a - 1.0) * ka_ref[...]),
                    e_a=jnp.exp(L - logdec - l_mid), e_r=jnp.exp(L - l_mid), e_k=jnp.exp(l_mid - L),
                    e_s=jnp.exp(l_tot - L), w_tot=jnp.exp(l_tot), e_mid=jnp.exp(l_mid))

    def chunk_body(i, carry):
        off = pl.multiple_of(i * C, C)
        rows = [row_inputs(bb, off) for bb in range(NB)]
        rk = rk_ref[...]
        chains = [(bb, h) for bb in range(NB) for h in range(N_HEADS)]
        X = range(len(chains))
        col = lambda name: [rows[bb][name][:, hsl[h]] for bb, h in chains]
        kkh = [x * lax.rsqrt(jnp.maximum(jnp.sum(x * x, -1, keepdims=True), 1e-24)) for x in col('kk_raw')]
        r_h, k_h, v_h, a_h = col('r'), col('k'), col('v'), col('a')
        e_a, e_r, e_k, e_s, w_tot, e_mid = col('e_a'), col('e_r'), col('e_k'), col('e_s'), col('w_tot'), col('e_mid')
        b_h = [kkh[c] * a_h[c] for c in X]
        ar = [jnp.concatenate([-kkh[c] * e_a[c], r_h[c] * e_r[c]], axis=0) for c in X]
        bk = [b_h[c] * e_k[c] for c in X]
        kd = [k_h[c] * e_k[c] for c in X]
        s_old = [state_ref[bb, h] for bb, h in chains]
        g_b = [nt(ar[c], bk[c]) for c in X]
        g_k = [nt(ar[c], kd[c]) for c in X]
        x0 = [nt(ar[c], s_old[c] * e_mid[c]) for c in X]
        a_mat = [jnp.where(strict, g_b[c][0:C], 0.0) for c in X]
        u = [x0[c][0:C] + _dot(jnp.where(strict, g_k[c][0:C], 0.0), v_h[c]) for c in X]
        for lvl in range(levels):
            au = [_dot(a_mat[c], u[c]) for c in X]
            if lvl + 1 < levels:
                a_mat = [_dot(a_mat[c], a_mat[c]) for c in X]
            u = [u[c] + au[c] for c in X]
        y = [x0[c][C:2 * C] + _dot(jnp.where(incl, g_b[c][C:2 * C], 0.0), u[c])
             + _dot(jnp.where(incl, g_k[c][C:2 * C], 0.0), v_h[c]) for c in X]
        for c, (bb, h) in enumerate(chains):
            uv = jnp.concatenate([u[c], v_h[c]], axis=0)
            bks = jnp.concatenate([b_h[c] * e_s[c], k_h[c] * e_s[c]], axis=0)
            state_ref[bb, h] = s_old[c] * w_tot[c] + tn(uv, bks)
        outs = []
        for c, (bb, h) in enumerate(chains):
            mu = jnp.mean(y[c], -1, keepdims=True)
            yc = y[c] - mu
            var = jnp.mean(yc * yc, -1, keepdims=True)
            bonus = jnp.sum(r_h[c] * k_h[c] * rk[:, hsl[h]], -1, keepdims=True) * v_h[c]
            outs.append((yc * lax.rsqrt(var + GN_EPS), bonus))
        for bb in range(NB):
            mine = outs[bb * N_HEADS:(bb + 1) * N_HEADS]
            yn = jnp.concatenate([o[0] for o in mine], axis=1)
            bonus = jnp.concatenate([o[1] for o in mine], axis=1)
            out = (yn * gng_ref[...] + gnb_ref[...] + bonus) * rows[bb]['g']
            o_ref[bb, pl.ds(off, C), :] = out.astype(o_ref.dtype)
        return carry

    lax.fori_loop(0, T // C, chunk_body, 0)
    sfin_ref[...] = state_ref[...]


RWKV_ROWS_PER_STEP = 2


def _rwkv(rkv, lmid, rkv0, s0, mu_rkv, w0, w2, a0, a2, g2, k_k, k_a, r_k, gn_g, gn_b):
    B, T, _ = rkv.shape
    nb = RWKV_ROWS_PER_STEP if B % RWKV_ROWS_PER_STEP == 0 else 1
    chunk = _largest_tile(T, 64, mult=16)
    levels = max(1, math.ceil(math.log2(chunk)))
    seq = lambda w: pl.BlockSpec((nb, T, w), lambda b: (b, 0, 0))
    st = pl.BlockSpec((nb, N_HEADS, HEAD_DIM, HEAD_DIM), lambda b: (b, 0, 0, 0))
    vec = lambda a: _full(a.shape)
    params = (mu_rkv, w0, w2, a0, a2, g2, k_k, k_a, r_k, gn_g, gn_b)
    return pl.pallas_call(
        functools.partial(_rwkv_kernel, chunk=chunk, levels=levels),
        grid=(B // nb,),
        in_specs=[seq(3 * HW), seq(256), pl.BlockSpec((nb, 1, 3 * HW), lambda b: (b, 0, 0)), st]
                 + [vec(p) for p in params],
        out_specs=[seq(HW), st],
        out_shape=[jax.ShapeDtypeStruct((B, T, HW), BF16),
                   jax.ShapeDtypeStruct((B, N_HEADS, HEAD_DIM, HEAD_DIM), F32)],
        scratch_shapes=[pltpu.VMEM((nb, N_HEADS, HEAD_DIM, HEAD_DIM), F32), pltpu.VMEM((nb, 1, 3 * HW), F32)],
        compiler_params=_params("arbitrary"),
        name="rwkv7",
    )(rkv, lmid, rkv0, s0, *params)


def _merge_kernel(fox_ref, rw_ref, gate_ref, h_ref, wa_ref, wb_ref, wo_ref, g_ref, b_ref, wrh_ref, wrl_ref, br_ref,
                  h1_ref, h1p_ref, idx_ref, gt_ref, rank_ref, cnt_ref, carry_ref, *, dn_alpha):
    i = pl.program_id(0)
    tm, D = h_ref.shape

    @pl.when(i == 0)
    def _():
        carry_ref[...] = jnp.zeros_like(carry_ref)

    gates = gate_ref[...].astype(F32)
    merged = gates[:, 0:D] * _dot(fox_ref[...], wa_ref[...]) + gates[:, D:2 * D] * _dot(rw_ref[...], wb_ref[...])
    z = dn_alpha * h_ref[...] + _dot(merged.astype(BF16), wo_ref[...])
    h1 = _layer_norm(z, g_ref[...], b_ref[...])
    h1_ref[...] = h1
    h1p_ref[...] = _pack_bf16_pair(h1)
    hi = h1.astype(BF16)
    lo = (h1 - hi.astype(F32)).astype(BF16)
    logits = _dot(hi, wrh_ref[...]) + _dot(hi, wrl_ref[...]) + _dot(lo, wrh_ref[...]) + br_ref[...]
    lane = lax.broadcasted_iota(I32, (tm, LANES), 1)
    lane_f = lane.astype(F32)
    cur = logits
    vals, idxs = [], []
    for _ in range(TOP_K):
        m = jnp.max(cur, -1, keepdims=True)
        ix = jnp.min(jnp.where(cur == m, lane_f, float(LANES)), -1, keepdims=True)
        vals.append(m)
        idxs.append(ix)
        cur = jnp.where(lane_f == ix, -3e38, cur)
    exps = [jnp.exp(vk - vals[0]) for vk in vals]
    denom = exps[0] + exps[1] + exps[2] + exps[3]
    onehot = jnp.zeros((tm, LANES), F32)
    for ix in idxs:
        onehot = onehot + jnp.where(lane_f == ix, 1.0, 0.0)
    r_i = lax.broadcasted_iota(I32, (tm, tm), 0)
    c_i = lax.broadcasted_iota(I32, (tm, tm), 1)
    tri = jnp.where(r_i > c_i, 1.0, 0.0).astype(BF16)
    before = _dot(tri, onehot.astype(BF16)) + carry_ref[...]
    idx_out = jnp.zeros((tm, LANES), F32)
    gt_out = jnp.zeros((tm, LANES), F32)
    rank_out = jnp.zeros((tm, LANES), F32)
    for kx in range(TOP_K):
        rank_k = jnp.sum(jnp.where(lane_f == idxs[kx], before, 0.0), -1, keepdims=True)
        idx_out = jnp.where(lane == kx, idxs[kx], idx_out)
        gt_out = jnp.where(lane == kx, exps[kx] / denom, gt_out)
        rank_out = jnp.where(lane == kx, rank_k, rank_out)
    idx_ref[...] = idx_out.astype(I32)
    gt_ref[...] = gt_out
    rank_ref[...] = rank_out.astype(I32)
    total = carry_ref[...] + jnp.sum(onehot, 0, keepdims=True)
    carry_ref[...] = total
    cnt_ref[...] = total.astype(I32)


def _merge_route(fox, rw, gates, h, wa, wb, wo, ln_g, ln_b, wr_hi, wr_lo, br, dn_alpha):
    N, D = h.shape
    tm = _largest_tile(N, 256)
    tile = lambda w: pl.BlockSpec((tm, w), lambda i: (i, 0))
    return pl.pallas_call(
        functools.partial(_merge_kernel, dn_alpha=dn_alpha),
        grid=(N // tm,),
        in_specs=[tile(HW), tile(HW), tile(2 * D), tile(D), _full(wa.shape), _full(wb.shape), _full(wo.shape),
                  _full((1, D)), _full((1, D)), _full(wr_hi.shape), _full(wr_lo.shape), _full(br.shape)],
        out_specs=[tile(D), tile(D // 2), tile(LANES), tile(LANES), tile(LANES), _full((1, LANES))],
        out_shape=[jax.ShapeDtypeStruct((N, D), F32), jax.ShapeDtypeStruct((N, D // 2), U32),
                   jax.ShapeDtypeStruct((N, LANES), I32),
                   jax.ShapeDtypeStruct((N, LANES), F32), jax.ShapeDtypeStruct((N, LANES), I32),
                   jax.ShapeDtypeStruct((1, LANES), I32)],
        scratch_shapes=[pltpu.VMEM((1, LANES), F32)],
        compiler_params=_params("arbitrary"),
        name="merge_route",
    )(fox, rw, gates, h, wa, wb, wo, ln_g, ln_b, wr_hi, wr_lo, br)


PERM_W = 256


def _deinterleave_kernel(w_ref, g_ref, l_ref):
    half = PERM_W // 2
    ii = lax.broadcasted_iota(I32, (PERM_W, PERM_W), 0)
    jj = lax.broadcasted_iota(I32, (PERM_W, PERM_W), 1)
    src = jnp.where(jj < half, 2 * jj, 2 * (jj - half) + 1)
    perm = jnp.where(ii == src, 1.0, 0.0).astype(BF16)
    for c in range(w_ref.shape[1] // PERM_W):
        w = w_ref[:, c * PERM_W:(c + 1) * PERM_W].astype(BF16)
        out = _dot(w, perm)
        g_ref[:, c * half:(c + 1) * half] = out[:, :half].astype(BF16)
        l_ref[:, c * half:(c + 1) * half] = out[:, half:].astype(BF16)


def _deinterleave_w1(w_e1):
    E, D, F2 = w_e1.shape
    wc = 512
    out = jax.ShapeDtypeStruct((E, D, F2 // 2), BF16)
    return pl.pallas_call(
        _deinterleave_kernel,
        grid=(E, F2 // wc),
        in_specs=[pl.BlockSpec((None, D, wc), lambda e, c: (e, 0, c))],
        out_specs=[pl.BlockSpec((None, D, wc // 2), lambda e, c: (e, 0, c))] * 2,
        out_shape=[out, out],
        compiler_params=_params("arbitrary", "arbitrary"),
        name="w1_deinterleave",
    )(w_e1)


def _dispatch_kernel(pad_ref, dest_ref, x_ref, xs_hbm, zero_ref, sem, *, n_pad):
    i = pl.program_id(0)
    tm = x_ref.shape[0]

    if n_pad:
        @pl.when(i == 0)
        def _():
            zero_ref[...] = jnp.zeros_like(zero_ref)

            def zbody(r, c):
                pltpu.make_async_copy(zero_ref.at[pl.ds(0, 1)], xs_hbm.at[pl.ds(pad_ref[r], 1)], sem.at[1]).start()
                return c
            lax.fori_loop(0, n_pad, zbody, 0, unroll=8)
            for _ in range(n_pad // tm):
                pltpu.make_async_copy(zero_ref, xs_hbm.at[pl.ds(0, tm)], sem.at[1]).wait()

    def body(r, c):
        for kx in range(TOP_K):
            d = dest_ref[0, 0, r * TOP_K + kx]
            pltpu.make_async_copy(x_ref.at[pl.ds(r, 1)], xs_hbm.at[pl.ds(d, 1)], sem.at[0]).start()
        return c
    lax.fori_loop(0, tm, body, 0, unroll=8)
    for _ in range(TOP_K):
        pltpu.make_async_copy(x_ref, xs_hbm.at[pl.ds(0, tm)], sem.at[0]).wait()


def _moe_dispatch(xp, dest, pad_slots, rows):
    N, W = xp.shape
    tm = _largest_tile(N, 256)
    n = N // tm
    n_pad = pad_slots.shape[0]
    assert n_pad % tm == 0, (n_pad, tm)
    grid_spec = pltpu.PrefetchScalarGridSpec(
        num_scalar_prefetch=1,
        grid=(n,),
        in_specs=[pl.BlockSpec((1, 1, tm * TOP_K), lambda i, pad: (i, 0, 0), memory_space=pltpu.SMEM),
                  pl.BlockSpec((tm, W), lambda i, pad: (i, 0))],
        out_specs=pl.BlockSpec(memory_space=pl.ANY),
        scratch_shapes=[pltpu.VMEM((tm, W), U32), pltpu.SemaphoreType.DMA((2,))],
    )
    return pl.pallas_call(
        functools.partial(_dispatch_kernel, n_pad=n_pad),
        grid_spec=grid_spec,
        out_shape=jax.ShapeDtypeStruct((rows, W), U32),
        compiler_params=_params("arbitrary"),
        name="moe_dispatch",
    )(pad_slots, dest.reshape(n, 1, tm * TOP_K), xp)


def _moe_kernel(be_ref, nused_ref, xs_ref, w1g_ref, w1l_ref, b1g_ref, b1l_ref, w2_ref, b2_ref, y_ref):
    j = pl.program_id(0)
    nused = nused_ref[0]

    @pl.when(j < nused)
    def _():
        lo, hi = _unpack_bf16_pair(xs_ref[...])
        x = jnp.concatenate([lo, hi], axis=1).astype(BF16)
        glu = jnp.minimum(_dot(x, w1g_ref[...]) + b1g_ref[...], SWIGLU_LIMIT)
        lin = jnp.clip(_dot(x, w1l_ref[...]) + b1l_ref[...], -SWIGLU_LIMIT, SWIGLU_LIMIT)
        act = glu * _sigmoid(SWIGLU_ALPHA * glu) * (lin + 1.0)
        y_ref[...] = _pack_bf16_pair(_dot(act.astype(BF16), w2_ref[...]) + b2_ref[...])

    @pl.when(j >= nused)
    def _():
        y_ref[...] = jnp.zeros_like(y_ref)


def _moe_experts(xs, blk_e, nused, w1g, w1l, b1g, b1l, w2, b2, bm):
    rows, W = xs.shape
    nb = rows // bm
    D, F = w1g.shape[1], w1g.shape[2]
    last = lambda j, be, nu: jnp.minimum(j, jnp.maximum(nu[0] - 1, 0))
    wspec = lambda k, n: pl.BlockSpec((None, k, n), lambda j, be, nu: (be[last(j, be, nu)], 0, 0))
    grid_spec = pltpu.PrefetchScalarGridSpec(
        num_scalar_prefetch=2,
        grid=(nb,),
        in_specs=[pl.BlockSpec((bm, W), lambda j, be, nu: (j, 0)),
                  wspec(D, F), wspec(D, F), wspec(1, F), wspec(1, F), wspec(F, D), wspec(1, D)],
        out_specs=pl.BlockSpec((bm, W), lambda j, be, nu: (j, 0)),
    )
    return pl.pallas_call(
        _moe_kernel,
        grid_spec=grid_spec,
        out_shape=jax.ShapeDtypeStruct((rows, W), U32),
        compiler_params=_params("arbitrary"),
        name="moe_experts",
    )(blk_e, nused, xs, w1g, w1l, b1g, b1l, w2, b2)


def _combine_gather_start(dest_ref, y_hbm, buf, sem, slot, tm):
    def body(r, c):
        for kx in range(TOP_K):
            d = dest_ref[0, 0, r * TOP_K + kx]
            pltpu.make_async_copy(y_hbm.at[pl.ds(d, 1)], buf.at[slot, kx, pl.ds(r, 1)], sem.at[slot]).start()
        return c
    lax.fori_loop(0, tm, body, 0, unroll=8)


def _combine_kernel(dest_ref, destn_ref, gt_ref, h1_ref, g_ref, b_ref, y_hbm, o_ref, buf, sem, *, dn_alpha):
    i = pl.program_id(0)
    n = pl.num_programs(0)
    tm = h1_ref.shape[0]
    slot = i % 2

    @pl.when(i == 0)
    def _():
        _combine_gather_start(dest_ref, y_hbm, buf, sem, 0, tm)

    @pl.when(i + 1 < n)
    def _():
        _combine_gather_start(destn_ref, y_hbm, buf, sem, 1 - slot, tm)

    for kx in range(TOP_K):
        pltpu.make_async_copy(y_hbm.at[pl.ds(0, tm)], buf.at[slot, kx], sem.at[slot]).wait()
    gt = gt_ref[...]
    lo, hi = _unpack_bf16_pair(buf[slot, 0])
    moe_lo, moe_hi = gt[:, 0:1] * lo, gt[:, 0:1] * hi
    for kx in range(1, TOP_K):
        lo, hi = _unpack_bf16_pair(buf[slot, kx])
        moe_lo = moe_lo + gt[:, kx:kx + 1] * lo
        moe_hi = moe_hi + gt[:, kx:kx + 1] * hi
    moe = jnp.concatenate([moe_lo, moe_hi], axis=1)
    o_ref[...] = _layer_norm(dn_alpha * h1_ref[...] + moe, g_ref[...], b_ref[...])


def _moe_combine(dest, gate, h1, ln_g, ln_b, yb, dn_alpha):
    N, D = h1.shape
    tm = _largest_tile(N, 256)
    n = N // tm
    dest3 = dest.reshape(n, 1, tm * TOP_K)
    tile = lambda w: pl.BlockSpec((tm, w), lambda i: (i, 0))
    return pl.pallas_call(
        functools.partial(_combine_kernel, dn_alpha=dn_alpha),
        grid=(n,),
        in_specs=[
            pl.BlockSpec((1, 1, tm * TOP_K), lambda i: (i, 0, 0), memory_space=pltpu.SMEM),
            pl.BlockSpec((1, 1, tm * TOP_K), lambda i: (jnp.minimum(i + 1, n - 1), 0, 0), memory_space=pltpu.SMEM),
            tile(LANES), tile(D), _full((1, D)), _full((1, D)),
            pl.BlockSpec(memory_space=pl.ANY),
        ],
        out_specs=tile(D),
        out_shape=jax.ShapeDtypeStruct((N, D), F32),
        scratch_shapes=[pltpu.VMEM((2, TOP_K, tm, yb.shape[1]), U32), pltpu.SemaphoreType.DMA((2,))],
        compiler_params=_params("arbitrary"),
        name="moe_combine",
    )(dest3, dest3, gate, h1, ln_g, ln_b, yb)


MOE_BLOCK_ROWS = 512


def _route_tables(top_idx, rank, counts, n_experts):
    n_tok = top_idx.shape[0]
    n_asg = n_tok * TOP_K
    bm = min(MOE_BLOCK_ROWS, max(8, 1 << int(math.log2(max(1, n_asg // n_experts)))))
    nb = -(-n_asg // bm) + n_experts
    padded = (counts + bm - 1) // bm * bm
    pends = jnp.cumsum(padded)
    dest = (pends - padded)[top_idx] + rank
    blk_start = jnp.arange(nb, dtype=I32) * bm
    blk_e = jnp.minimum(jnp.sum(pends[None, :] <= blk_start[:, None], axis=1), n_experts - 1).astype(I32)
    nused = (pends[-1] // bm).astype(I32).reshape(1)
    n_pad = nb * bm - n_asg
    gap = padded - counts
    gap_end = jnp.cumsum(gap)
    i = jnp.arange(n_pad, dtype=I32)
    e = jnp.sum(gap_end[None, :] <= i[:, None], axis=1)
    ec = jnp.minimum(e, n_experts - 1)
    in_group = (pends - padded + counts)[ec] + i - (gap_end - gap)[ec]
    pad_slots = jnp.where(e < n_experts, in_group, pends[-1] + i - gap_end[-1]).astype(I32)
    return dest.astype(I32), pad_slots, blk_e, nused, bm, nb * bm


def _stream(x, prev_row, s0, past_k, past_v, past_logf, wts):
    B, T, D = x.shape
    dn_alpha = wts['dn_alpha']
    h, q, k, v, rkv, gates, lmid, logf, rkv0 = _inproj(
        x, prev_row, wts['ln0_g'], wts['ln0_b'], wts['wm'], wts['wff'], wts['bff'], wts['mu3'],
        wts['w1'], wts['a1'], wts['g1'])
    if past_k is None:
        k_aug, v_bf = _fox_prep(logf, k, v, wts['sel_k'], wts['sel_c'])
        fox = _fox_attention(q, k_aug, v_bf, wts['sel_q'], 0)
    else:
        P = past_k.shape[1]
        past_pad = jnp.pad(past_logf.astype(F32), ((0, 0), (0, 0), (0, LANES - N_HEADS)))
        k_aug, v_bf = _fox_prep(jnp.concatenate([past_pad, logf], axis=1),
                                jnp.concatenate([past_k.reshape(B, P, HW), k], axis=1),
                                jnp.concatenate([past_v.reshape(B, P, HW), v], axis=1), wts['sel_k'], wts['sel_c'])
        fox = _fox_attention(q, k_aug, v_bf, wts['sel_q'], P)
    rw, s_fin = _rwkv(rkv, lmid, rkv0, s0, wts['mu_rkv'], wts['w0'], wts['w2'], wts['a0'], wts['a2'], wts['g2'],
                      wts['k_k'], wts['k_a'], wts['r_k'], wts['gn_g'], wts['gn_b'])
    N = B * T
    h1, h1p, top_idx, gate, rank, counts = _merge_route(
        fox.reshape(N, HW), rw.reshape(N, HW), gates.reshape(N, 2 * D), h.reshape(N, D),
        wts['w_up_a'], wts['w_up_b'], wts['w_out'], wts['ln1_g'], wts['ln1_b'],
        wts['wr_hi'], wts['wr_lo'], wts['br'], dn_alpha)
    n_experts = wts['n_experts']
    dest, pad_slots, blk_e, nused, bm, rows = _route_tables(top_idx[:, :TOP_K], rank[:, :TOP_K],
                                                            counts[0, :n_experts], n_experts)
    xs = _moe_dispatch(h1p, dest, pad_slots, rows)
    yb = _moe_experts(xs, blk_e, nused, wts['w1g'], wts['w1l'], wts['b1g'], wts['b1l'], wts['we2'], wts['be2'], bm)
    y = _moe_combine(dest, gate, h1, wts['ln2_g'], wts['ln2_b'], yb, dn_alpha)
    new_k = k.reshape(B, T, N_HEADS, HEAD_DIM)
    new_v = v.reshape(B, T, N_HEADS, HEAD_DIM)
    return y.reshape(B, T, D), new_k, new_v, logf[:, :, :N_HEADS], s_fin, h[:, T - 1:T, :]


def kernel(x_prompt, x_sample, cache_fox_k, cache_fox_v, cache_fox_logf, state_rwkv, state_shift, meta, ln0_g, ln0_b, w_in, b_forget, mu_w, mu_a, mu_g, mu_rkv, w0, w1, w2, a0, a1, a2, g1, g2, k_k, k_a, r_k, gn_g, gn_b, w_up_a, w_up_b, w_out, ln1_g, ln1_b, w_router, b_router, w_e1, b_e1, w_e2, b_e2, ln2_g, ln2_b):
    depth, D, in_cols = w_in.shape
    assert depth == 1 and D == 1024 and in_cols == 6 * HW + N_HEADS + 2 * D
    n_experts = w_router.shape[2]
    assert n_experts <= LANES
    B = x_prompt.shape[0]
    l = 0
    w = w_in[l]
    off_ff = 3 * HW
    row = lambda a: a.reshape(1, -1).astype(F32)
    wr = jnp.pad(w_router[l], ((0, 0), (0, LANES - n_experts)))
    wr_hi = wr.astype(BF16)
    w1g, w1l = _deinterleave_w1(w_e1[l])
    sel_k, sel_q, sel_c = _aug_select_matrices()
    wts = dict(
        sel_k=sel_k, sel_q=sel_q, sel_c=sel_c,
        dn_alpha=float((2 * depth) ** 0.25), n_experts=n_experts,
        ln0_g=row(ln0_g), ln0_b=row(ln0_b),
        wm=jnp.concatenate([w[:, :off_ff], w[:, off_ff + N_HEADS:]], axis=1).astype(BF16),
        wff=jnp.pad(w[:, off_ff:off_ff + N_HEADS], ((0, 0), (0, LANES - N_HEADS))).astype(BF16),
        bff=jnp.pad(row(b_forget[l]), ((0, 0), (0, LANES - N_HEADS))),
        mu3=jnp.stack([mu_w[l], mu_a[l], mu_g[l]], axis=0),
        w1=w1[l].astype(BF16), a1=a1[l].astype(BF16), g1=g1[l].astype(BF16),
        mu_rkv=row(mu_rkv[l]), w0=row(w0[l]), w2=w2[l].astype(BF16), a0=row(a0[l]), a2=a2[l].astype(BF16),
        g2=g2[l].astype(BF16), k_k=row(k_k[l]), k_a=row(k_a[l]), r_k=row(r_k[l]), gn_g=row(gn_g[l]),
        gn_b=row(gn_b[l]),
        w_up_a=w_up_a[l].astype(BF16), w_up_b=w_up_b[l].astype(BF16), w_out=w_out[l].astype(BF16),
        ln1_g=row(ln1_g[l]), ln1_b=row(ln1_b[l]),
        wr_hi=wr_hi, wr_lo=(wr - wr_hi.astype(F32)).astype(BF16),
        br=jnp.pad(row(b_router[l]), ((0, 0), (0, LANES - n_experts)), constant_values=NEG_BIG),
        w1g=w1g, w1l=w1l,
        b1g=b_e1[l][:, None, 0::2], b1l=b_e1[l][:, None, 1::2],
        we2=w_e2[l].astype(BF16), be2=b_e2[l][:, None, :],
        ln2_g=row(ln2_g[l]), ln2_b=row(ln2_b[l]),
    )
    meta_b = jnp.broadcast_to(meta, (B, N_META, D)).astype(x_prompt.dtype)
    xp = jnp.concatenate([meta_b, x_prompt], axis=1)
    zero_row = jnp.zeros((B, 1, D), F32)
    zero_state = jnp.zeros((B, N_HEADS, HEAD_DIM, HEAD_DIM), F32)
    y_p, k_p, v_p, lf_p, s_p, sh_p = _stream(xp, zero_row, zero_state, None, None, None, wts)
    y_s, k_s, v_s, lf_s, s_s, sh_s = _stream(x_sample, state_shift[l], state_rwkv[l], cache_fox_k[l],
                                             cache_fox_v[l], cache_fox_logf[l], wts)
    ex = lambda a: a[None]
    return (y_p[:, N_META:], y_s, ex(k_p), ex(v_p), ex(lf_p), ex(s_p), ex(sh_p),
            ex(k_s), ex(v_s), ex(lf_s), ex(s_s), ex(sh_s))
```

```python
import functools
import math

import jax
import jax.numpy as jnp
from jax import lax
from jax.experimental import pallas as pl
from jax.experimental.pallas import tpu as pltpu

F32 = jnp.float32
BF16 = jnp.bfloat16
I32 = jnp.int32
U32 = jnp.uint32

N_META = 16
HEAD_DIM = 64
N_HEADS = 8
HW = N_HEADS * HEAD_DIM
TOP_K = 4
SWIGLU_LIMIT = 7.0
SWIGLU_ALPHA = 1.702
LN_EPS = 1e-5
GN_EPS = 64e-5
LANES = 128
NEG_BIG = -1e30
VMEM_LIMIT_BYTES = 56 * 1024 * 1024
HIGHEST = lax.Precision.HIGHEST

NT_DIMS = (((1,), (1,)), ((), ()))
TN_DIMS = (((0,), (0,)), ((), ()))


def _params(*sem):
    return pltpu.CompilerParams(dimension_semantics=sem, vmem_limit_bytes=VMEM_LIMIT_BYTES)


def _largest_tile(n, cap, mult=8):
    best = None
    for d in range(mult, min(n, cap) + 1, mult):
        if n % d == 0:
            best = d
    assert best is not None, (n, cap, mult)
    return best


def _sigmoid(x):
    return 1.0 / (1.0 + jnp.exp(-x))


def _softplus(x):
    return jnp.maximum(x, 0.0) + jnp.log1p(jnp.exp(-jnp.abs(x)))


def _layer_norm(x, g, b):
    mu = jnp.mean(x, -1, keepdims=True)
    xc = x - mu
    var = jnp.mean(xc * xc, -1, keepdims=True)
    return xc * lax.rsqrt(var + LN_EPS) * g + b


def _dot(a, b):
    return jnp.dot(a, b, preferred_element_type=F32)


def _full(shape):
    n = len(shape)
    return pl.BlockSpec(shape, lambda *_: (0,) * n)


def _pack_bf16_pair(x):
    w = x.shape[1] // 2
    bits = lambda t: lax.bitcast_convert_type(t.astype(BF16).astype(F32), U32)
    return (bits(x[:, :w]) >> 16) | (bits(x[:, w:]) & jnp.uint32(0xFFFF0000))


def _unpack_bf16_pair(u):
    return lax.bitcast_convert_type(u << 16, F32), lax.bitcast_convert_type(u & jnp.uint32(0xFFFF0000), F32)


C_Q, C_K, C_V, C_RKV, C_GA, C_END = 0, HW, 2 * HW, 3 * HW, 6 * HW, 6 * HW + 2048


def _inproj_kernel(x_ref, prev_ref, g_ref, b_ref, wm_ref, wff_ref, bff_ref, mu_ref, w1_ref, a1_ref, g1_ref,
                   h_ref, q_ref, k_ref, v_ref, rkv_ref, gate_ref, lmid_ref, logf_ref, rkv0_ref,
                   carry_ref):
    t = pl.program_id(1)
    tt = x_ref.shape[0]
    h = _layer_norm(x_ref[...], g_ref[...], b_ref[...])
    h_ref[...] = h

    @pl.when(t == 0)
    def _():
        prev = prev_ref[...]
        carry_ref[...] = prev
        p8 = jnp.broadcast_to(prev, (8, prev.shape[1])).astype(BF16)
        rkv0_ref[...] = _dot(p8, wm_ref[:, C_RKV:C_GA])[0:1]

    rows = lax.broadcasted_iota(I32, h.shape, 0)
    hprev = jnp.where(rows == 0, carry_ref[...], pltpu.roll(h, 1, axis=0))
    carry_ref[...] = h[tt - 1:tt, :]
    dx = hprev - h
    hb = h.astype(BF16)
    q_ref[...] = _dot(hb, wm_ref[:, C_Q:C_K]).astype(BF16)
    k_ref[...] = _dot(hb, wm_ref[:, C_K:C_V])
    v_ref[...] = _dot(hb, wm_ref[:, C_V:C_RKV])
    rkv_ref[...] = _dot(hb, wm_ref[:, C_RKV:C_GA]).astype(BF16)
    gate_ref[...] = _sigmoid(_dot(hb, wm_ref[:, C_GA:C_END])).astype(BF16)
    ff = _dot(hb, wff_ref[...]) + bff_ref[...]
    logf_ref[...] = -_softplus(-ff)
    mu = mu_ref[...]
    lmid_ref[:, 0:64] = _dot((h + dx * mu[0:1]).astype(BF16), w1_ref[...])
    lmid_ref[:, 64:128] = _dot((h + dx * mu[1:2]).astype(BF16), a1_ref[...])
    lmid_ref[:, 128:256] = _dot((h + dx * mu[2:3]).astype(BF16), g1_ref[...])


def _inproj(x, prev_row, ln_g, ln_b, wm, wff, bff, mu3, w1, a1, g1):
    B, T, D = x.shape
    tt = _largest_tile(T, 384)
    nt = T // tt
    tile = lambda w: pl.BlockSpec((None, tt, w), lambda b, t: (b, t, 0))
    row = lambda w: pl.BlockSpec((None, 1, w), lambda b, t: (b, 0, 0))
    out_shape = [
        jax.ShapeDtypeStruct((B, T, D), F32),
        jax.ShapeDtypeStruct((B, T, HW), BF16),
        jax.ShapeDtypeStruct((B, T, HW), F32),
        jax.ShapeDtypeStruct((B, T, HW), F32),
        jax.ShapeDtypeStruct((B, T, 3 * HW), BF16),
        jax.ShapeDtypeStruct((B, T, 2 * D), BF16),
        jax.ShapeDtypeStruct((B, T, 256), F32),
        jax.ShapeDtypeStruct((B, T, LANES), F32),
        jax.ShapeDtypeStruct((B, 1, 3 * HW), F32),
    ]
    return pl.pallas_call(
        _inproj_kernel,
        grid=(B, nt),
        in_specs=[tile(D), row(D), _full((1, D)), _full((1, D)), _full(wm.shape), _full(wff.shape),
                  _full(bff.shape), _full(mu3.shape), _full(w1.shape), _full(a1.shape), _full(g1.shape)],
        out_specs=[tile(D), tile(HW), tile(HW), tile(HW), tile(3 * HW), tile(2 * D), tile(256), tile(LANES),
                   row(3 * HW)],
        out_shape=out_shape,
        scratch_shapes=[pltpu.VMEM((1, D), F32)],
        compiler_params=_params("arbitrary", "arbitrary"),
        name="inproj",
    )(x, prev_row, ln_g, ln_b, wm, wff, bff, mu3, w1, a1, g1)


HEAD_PAD = 2 * HEAD_DIM
C_SPLIT = 3


def _aug_select_matrices():
    rows = jnp.arange(HW)
    sel_k = jnp.zeros((HW, N_HEADS * HEAD_PAD), F32).at[rows, (rows // HEAD_DIM) * HEAD_PAD + rows % HEAD_DIM].set(1.0)
    p = jnp.repeat(jnp.arange(C_SPLIT), N_HEADS)
    h = jnp.tile(jnp.arange(N_HEADS), C_SPLIT)
    sel_c = jnp.zeros((LANES, N_HEADS * HEAD_PAD), F32).at[p * N_HEADS + h, h * HEAD_PAD + HEAD_DIM + p].set(1.0)
    return (sel_k * HEAD_DIM ** -0.5).astype(BF16), sel_c.astype(BF16)


def _split3(x):
    hi = x.astype(BF16)
    r1 = x - hi.astype(F32)
    mid = r1.astype(BF16)
    return hi, mid, (r1 - mid.astype(F32)).astype(BF16)


def _fox_prep_kernel(lf_ref, k_ref, v_ref, selc_ref, ka_ref, vt_ref, carry_ref):
    t = pl.program_id(1)
    tt = lf_ref.shape[0]

    @pl.when(t == 0)
    def _():
        carry_ref[...] = jnp.zeros_like(carry_ref)

    r = lax.broadcasted_iota(I32, (tt, tt), 0)
    c = lax.broadcasted_iota(I32, (tt, tt), 1)
    tri = jnp.where(r >= c, 1.0, 0.0).astype(BF16)
    cs3 = _dot(tri, jnp.concatenate(_split3(lf_ref[...]), axis=1))
    cs = cs3[:, 0:LANES] + cs3[:, LANES:2 * LANES] + cs3[:, 2 * LANES:3 * LANES] + carry_ref[...]
    carry_ref[...] = cs[tt - 1:tt, :]
    hi, mid, lo = _split3(-cs)
    is_head = lax.broadcasted_iota(I32, (tt, LANES), 1) < N_HEADS
    keep = lambda part: jnp.where(is_head, part.astype(F32), 0.0)
    packed = keep(hi) + pltpu.roll(keep(mid), N_HEADS, axis=1) + pltpu.roll(keep(lo), 2 * N_HEADS, axis=1)
    kc = _dot(packed.astype(BF16), selc_ref[...])
    k = k_ref[...]
    pad = jnp.zeros((tt, HEAD_PAD - HEAD_DIM), F32)
    for h in range(N_HEADS):
        hp = slice(h * HEAD_PAD, (h + 1) * HEAD_PAD)
        ka_ref[:, hp] = (jnp.concatenate([k[:, h * HEAD_DIM:(h + 1) * HEAD_DIM], pad], axis=1) + kc[:, hp]).astype(BF16)
    ii = lax.broadcasted_iota(I32, (HW, HW), 0)
    jj = lax.broadcasted_iota(I32, (HW, HW), 1)
    eye = jnp.where(ii == jj, 1.0, 0.0).astype(BF16)
    vt_ref[...] = lax.dot_general(eye, v_ref[...].astype(BF16), NT_DIMS, preferred_element_type=F32).astype(BF16)


def _fox_prep(logf, k, v, sel_c):
    B, T, _ = k.shape
    tt = _largest_tile(T, 384)
    nt = T // tt
    tile = lambda w: pl.BlockSpec((None, tt, w), lambda b, t: (b, t, 0))
    return pl.pallas_call(
        _fox_prep_kernel,
        grid=(B, nt),
        in_specs=[tile(LANES), tile(HW), tile(HW), _full(sel_c.shape)],
        out_specs=[tile(N_HEADS * HEAD_PAD), pl.BlockSpec((None, None, HW, tt), lambda b, t: (b, t, 0, 0))],
        out_shape=[jax.ShapeDtypeStruct((B, T, N_HEADS * HEAD_PAD), BF16),
                   jax.ShapeDtypeStruct((B, nt, HW, tt), BF16)],
        scratch_shapes=[pltpu.VMEM((1, LANES), F32)],
        compiler_params=_params("arbitrary", "arbitrary"),
        name="fox_prep",
    )(logf, k, v, sel_c)


def _fox_kernel(qi_ref, ki_ref, last_ref, q_ref, ka_ref, vt_ref, selq_ref, o_ref, qa_ref, m_ref, l_ref, acc_ref,
                *, q0, tq, tk):
    p = pl.program_id(1)
    qi = qi_ref[p]
    ki = ki_ref[p]

    @pl.when(ki == 0)
    def _():
        m_ref[...] = jnp.full_like(m_ref, NEG_BIG)
        l_ref[...] = jnp.zeros_like(l_ref)
        acc_ref[...] = jnp.zeros_like(acc_ref)
        lane = lax.broadcasted_iota(I32, qa_ref.shape, 1) % HEAD_PAD
        ones = jnp.where(jnp.logical_and(lane >= HEAD_DIM, lane < HEAD_DIM + C_SPLIT), 1.0, 0.0)
        qa_ref[...] = (_dot(q_ref[...], selq_ref[...]) + ones).astype(BF16)

    first_q = q0 + qi * tq
    tile_first = ki * tk
    tile_last = tile_first + tk - 1

    def scores(h):
        hp = slice(h * HEAD_PAD, (h + 1) * HEAD_PAD)
        return lax.dot_general(ka_ref[:, hp], qa_ref[:, hp], NT_DIMS, preferred_element_type=F32)

    def tile_update(masked):
        if masked:
            key_pos = tile_first + lax.broadcasted_iota(I32, (tk, tq), 0)
            qry_pos = first_q + lax.broadcasted_iota(I32, (tk, tq), 1)
            bias = jnp.where(qry_pos >= key_pos, 0.0, NEG_BIG)
        m_all = m_ref[...]
        l_all = l_ref[...]
        m_rows, l_rows = [], []
        s_next = scores(0)
        for h in range(N_HEADS):
            hs = slice(h * HEAD_DIM, (h + 1) * HEAD_DIM)
            s = s_next
            if h + 1 < N_HEADS:
                s_next = scores(h + 1)
            if masked:
                s = s + bias
            m_prev = m_all[h:h + 1, :]
            m_new = jnp.maximum(m_prev, jnp.max(s, 0, keepdims=True))
            alpha = jnp.exp(m_prev - m_new)
            p = jnp.exp(s - m_new)
            l_rows.append(alpha * l_all[h:h + 1, :] + jnp.sum(p, 0, keepdims=True))
            m_rows.append(m_new)
            acc_ref[hs, :] = alpha * acc_ref[hs, :] + _dot(vt_ref[hs, :], p.astype(BF16))
        m_ref[...] = jnp.concatenate(m_rows, axis=0)
        l_ref[...] = jnp.concatenate(l_rows, axis=0)

    @pl.when(jnp.logical_and(tile_first <= first_q + tq - 1, tile_last > first_q))
    def _():
        tile_update(True)

    @pl.when(tile_last <= first_q)
    def _():
        tile_update(False)

    @pl.when(last_ref[p] == 1)
    def _():
        on = jnp.concatenate(
            [acc_ref[h * HEAD_DIM:(h + 1) * HEAD_DIM, :] / l_ref[h:h + 1, :] for h in range(N_HEADS)], axis=0)
        eye = jnp.where(lax.broadcasted_iota(I32, (tq, tq), 0) == lax.broadcasted_iota(I32, (tq, tq), 1),
                        1.0, 0.0).astype(BF16)
        o_ref[...] = lax.dot_general(eye, on.astype(BF16), NT_DIMS, preferred_element_type=F32).astype(o_ref.dtype)


def _fox_attention(q, k_aug, v_t, sel_q, q0):
    B, Tq, _ = q.shape
    Tk = k_aug.shape[1]
    WA = N_HEADS * HEAD_PAD
    tq = _largest_tile(Tq, 384)
    nk, tk = v_t.shape[1], v_t.shape[3]
    assert nk * tk == Tk
    nq = Tq // tq
    pairs = [(qi, ki) for qi in range(nq) for ki in range(min((q0 + (qi + 1) * tq - 1) // tk, nk - 1) + 1)]
    qi_tab = jnp.array([p[0] for p in pairs], I32)
    ki_tab = jnp.array([p[1] for p in pairs], I32)
    last_tab = jnp.array([int(i + 1 == len(pairs) or pairs[i + 1][0] != pairs[i][0]) for i in range(len(pairs))], I32)
    qspec = lambda w: pl.BlockSpec((None, tq, w), lambda b, p, qt, kt, lt: (b, qt[p], 0))
    grid_spec = pltpu.PrefetchScalarGridSpec(
        num_scalar_prefetch=3,
        grid=(B, len(pairs)),
        in_specs=[qspec(HW),
                  pl.BlockSpec((None, tk, WA), lambda b, p, qt, kt, lt: (b, kt[p], 0)),
                  pl.BlockSpec((None, None, HW, tk), lambda b, p, qt, kt, lt: (b, kt[p], 0, 0)),
                  pl.BlockSpec(sel_q.shape, lambda b, p, qt, kt, lt: (0, 0))],
        out_specs=qspec(HW),
        scratch_shapes=[pltpu.VMEM((tq, WA), BF16), pltpu.VMEM((N_HEADS, tq), F32),
                        pltpu.VMEM((N_HEADS, tq), F32), pltpu.VMEM((HW, tq), F32)],
    )
    return pl.pallas_call(
        functools.partial(_fox_kernel, q0=q0, tq=tq, tk=tk),
        grid_spec=grid_spec,
        out_shape=jax.ShapeDtypeStruct((B, Tq, HW), BF16),
        compiler_params=_params("arbitrary", "arbitrary"),
        name="fox_attention",
    )(qi_tab, ki_tab, last_tab, q, k_aug, v_t, sel_q)


def _rwkv_kernel(rkv_ref, lmid_ref, rkv0_ref, s0_ref, mu_ref, w0_ref, w2_ref, a0_ref, a2_ref, g2_ref,
                 kk_ref, ka_ref, rk_ref, gng_ref, gnb_ref, o_ref, sfin_ref, state_ref, carry_ref, *, chunk, levels):
    C = chunk
    NB, T = rkv_ref.shape[0], rkv_ref.shape[1]
    state_ref[...] = s0_ref[...]
    carry_ref[...] = rkv0_ref[...]
    row_w = lax.broadcasted_iota(I32, (C, 3 * HW), 0)
    row_h = lax.broadcasted_iota(I32, (C, HW), 0)
    r_i = lax.broadcasted_iota(I32, (C, C), 0)
    c_i = lax.broadcasted_iota(I32, (C, C), 1)
    strict = r_i > c_i
    incl = r_i >= c_i
    mid = C // 2 - 1 if C > 1 else 0
    hsl = [slice(h * HEAD_DIM, (h + 1) * HEAD_DIM) for h in range(N_HEADS)]
    nt = lambda x, y: lax.dot_general(x, y, NT_DIMS, preferred_element_type=F32)
    tn = lambda x, y: lax.dot_general(x, y, TN_DIMS, preferred_element_type=F32)

    def row_inputs(bb, off):
        x = rkv_ref[bb, pl.ds(off, C), :].astype(F32)
        prev = jnp.where(row_w == 0, carry_ref[bb], pltpu.roll(x, 1, axis=0))
        carry_ref[bb] = x[C - 1:C, :]
        x = x + (prev - x) * mu_ref[...]
        r, k0, v = x[:, 0:HW], x[:, HW:2 * HW], x[:, 2 * HW:3 * HW]
        lm = lmid_ref[bb, pl.ds(off, C), :]
        w_pre = w0_ref[...] + _dot(jnp.tanh(lm[:, 0:64]).astype(BF16), w2_ref[...])
        a = _sigmoid(a0_ref[...] + _dot(lm[:, 64:128].astype(BF16), a2_ref[...]))
        g = _dot(_sigmoid(lm[:, 128:256]).astype(BF16), g2_ref[...])
        w_log = -_softplus(-w_pre) - 0.5
        logdec = -jnp.exp(w_log)
        L = logdec
        sh = 1
        while sh < C:
            L = L + jnp.where(row_h >= sh, pltpu.roll(L, sh, axis=0), 0.0)
            sh *= 2
        l_mid = L[mid:mid + 1, :]
        l_tot = L[C - 1:C, :]
        return dict(r=r, v=v, a=a, g=g, kk_raw=k0 * kk_ref[...], k=k0 * (1.0 + (a - 1.0) * ka_ref[...]),
                    e_a=jnp.exp(L - logdec - l_mid), e_r=jnp.exp(L - l_mid), e_k=jnp.exp(l_mid - L),
                    e_s=jnp.exp(l_tot - L), w_tot=jnp.exp(l_tot), e_mid=jnp.exp(l_mid))

    def chunk_body(i, carry):
        off = pl.multiple_of(i * C, C)
        rows = [row_inputs(bb, off) for bb in range(NB)]
        rk = rk_ref[...]
        chains = [(bb, h) for bb in range(NB) for h in range(N_HEADS)]
        X = range(len(chains))
        col = lambda name: [rows[bb][name][:, hsl[h]] for bb, h in chains]
        kkh = [x * lax.rsqrt(jnp.maximum(jnp.sum(x * x, -1, keepdims=True), 1e-24)) for x in col('kk_raw')]
        r_h, k_h, v_h, a_h = col('r'), col('k'), col('v'), col('a')
        e_a, e_r, e_k, e_s, w_tot, e_mid = col('e_a'), col('e_r'), col('e_k'), col('e_s'), col('w_tot'), col('e_mid')
        b_h = [kkh[c] * a_h[c] for c in X]
        ar = [jnp.concatenate([-kkh[c] * e_a[c], r_h[c] * e_r[c]], axis=0) for c in X]
        bk = [b_h[c] * e_k[c] for c in X]
        kd = [k_h[c] * e_k[c] for c in X]
        s_old = [state_ref[bb, h] for bb, h in chains]
        g_b = [nt(ar[c], bk[c]) for c in X]
        g_k = [nt(ar[c], kd[c]) for c in X]
        x0 = [nt(ar[c], s_old[c] * e_mid[c]) for c in X]
        a_mat = [jnp.where(strict, g_b[c][0:C], 0.0) for c in X]
        u = [x0[c][0:C] + _dot(jnp.where(strict, g_k[c][0:C], 0.0), v_h[c]) for c in X]
        for lvl in range(levels):
            au = [_dot(a_mat[c], u[c]) for c in X]
            if lvl + 1 < levels:
                a_mat = [_dot(a_mat[c], a_mat[c]) for c in X]
            u = [u[c] + au[c] for c in X]
        y = [x0[c][C:2 * C] + _dot(jnp.where(incl, g_b[c][C:2 * C], 0.0), u[c])
             + _dot(jnp.where(incl, g_k[c][C:2 * C], 0.0), v_h[c]) for c in X]
        for c, (bb, h) in enumerate(chains):
            uv = jnp.concatenate([u[c], v_h[c]], axis=0)
            bks = jnp.concatenate([b_h[c] * e_s[c], k_h[c] * e_s[c]], axis=0)
            state_ref[bb, h] = s_old[c] * w_tot[c] + tn(uv, bks)
        outs = []
        for c, (bb, h) in enumerate(chains):
            mu = jnp.mean(y[c], -1, keepdims=True)
            yc = y[c] - mu
            var = jnp.mean(yc * yc, -1, keepdims=True)
            bonus = jnp.sum(r_h[c] * k_h[c] * rk[:, hsl[h]], -1, keepdims=True) * v_h[c]
            outs.append((yc * lax.rsqrt(var + GN_EPS), bonus))
        for bb in range(NB):
            mine = outs[bb * N_HEADS:(bb + 1) * N_HEADS]
            yn = jnp.concatenate([o[0] for o in mine], axis=1)
            bonus = jnp.concatenate([o[1] for o in mine], axis=1)
            out = (yn * gng_ref[...] + gnb_ref[...] + bonus) * rows[bb]['g']
            o_ref[bb, pl.ds(off, C), :] = out.astype(o_ref.dtype)
        return carry

    lax.fori_loop(0, T // C, chunk_body, 0)
    sfin_ref[...] = state_ref[...]


RWKV_ROWS_PER_STEP = 2


def _rwkv(rkv, lmid, rkv0, s0, mu_rkv, w0, w2, a0, a2, g2, k_k, k_a, r_k, gn_g, gn_b):
    B, T, _ = rkv.shape
    nb = RWKV_ROWS_PER_STEP if B % RWKV_ROWS_PER_STEP == 0 else 1
    chunk = _largest_tile(T, 64, mult=16)
    levels = max(1, math.ceil(math.log2(chunk)))
    seq = lambda w: pl.BlockSpec((nb, T, w), lambda b: (b, 0, 0))
    st = pl.BlockSpec((nb, N_HEADS, HEAD_DIM, HEAD_DIM), lambda b: (b, 0, 0, 0))
    vec = lambda a: _full(a.shape)
    params = (mu_rkv, w0, w2, a0, a2, g2, k_k, k_a, r_k, gn_g, gn_b)
    return pl.pallas_call(
        functools.partial(_rwkv_kernel, chunk=chunk, levels=levels),
        grid=(B // nb,),
        in_specs=[seq(3 * HW), seq(256), pl.BlockSpec((nb, 1, 3 * HW), lambda b: (b, 0, 0)), st]
                 + [vec(p) for p in params],
        out_specs=[seq(HW), st],
        out_shape=[jax.ShapeDtypeStruct((B, T, HW), BF16),
                   jax.ShapeDtypeStruct((B, N_HEADS, HEAD_DIM, HEAD_DIM), F32)],
        scratch_shapes=[pltpu.VMEM((nb, N_HEADS, HEAD_DIM, HEAD_DIM), F32), pltpu.VMEM((nb, 1, 3 * HW), F32)],
        compiler_params=_params("arbitrary"),
        name="rwkv7",
    )(rkv, lmid, rkv0, s0, *params)


def _merge_kernel(fox_ref, rw_ref, gate_ref, h_ref, wa_ref, wb_ref, wo_ref, g_ref, b_ref, wrh_ref, wrl_ref, br_ref,
                  h1_ref, h1p_ref, idx_ref, gt_ref, rank_ref, cnt_ref, carry_ref, *, dn_alpha):
    i = pl.program_id(0)
    tm, D = h_ref.shape

    @pl.when(i == 0)
    def _():
        carry_ref[...] = jnp.zeros_like(carry_ref)

    gates = gate_ref[...].astype(F32)
    merged = gates[:, 0:D] * _dot(fox_ref[...], wa_ref[...]) + gates[:, D:2 * D] * _dot(rw_ref[...], wb_ref[...])
    z = dn_alpha * h_ref[...] + _dot(merged.astype(BF16), wo_ref[...])
    h1 = _layer_norm(z, g_ref[...], b_ref[...])
    h1_ref[...] = h1
    h1p_ref[...] = _pack_bf16_pair(h1)
    hi = h1.astype(BF16)
    lo = (h1 - hi.astype(F32)).astype(BF16)
    logits = _dot(hi, wrh_ref[...]) + _dot(hi, wrl_ref[...]) + _dot(lo, wrh_ref[...]) + br_ref[...]
    lane = lax.broadcasted_iota(I32, (tm, LANES), 1)
    lane_f = lane.astype(F32)
    cur = logits
    vals, idxs = [], []
    for _ in range(TOP_K):
        m = jnp.max(cur, -1, keepdims=True)
        ix = jnp.min(jnp.where(cur == m, lane_f, float(LANES)), -1, keepdims=True)
        vals.append(m)
        idxs.append(ix)
        cur = jnp.where(lane_f == ix, -3e38, cur)
    exps = [jnp.exp(vk - vals[0]) for vk in vals]
    denom = exps[0] + exps[1] + exps[2] + exps[3]
    onehot = jnp.zeros((tm, LANES), F32)
    for ix in idxs:
        onehot = onehot + jnp.where(lane_f == ix, 1.0, 0.0)
    r_i = lax.broadcasted_iota(I32, (tm, tm), 0)
    c_i = lax.broadcasted_iota(I32, (tm, tm), 1)
    tri = jnp.where(r_i > c_i, 1.0, 0.0).astype(BF16)
    before = _dot(tri, onehot.astype(BF16)) + carry_ref[...]
    idx_out = jnp.zeros((tm, LANES), F32)
    gt_out = jnp.zeros((tm, LANES), F32)
    rank_out = jnp.zeros((tm, LANES), F32)
    for kx in range(TOP_K):
        rank_k = jnp.sum(jnp.where(lane_f == idxs[kx], before, 0.0), -1, keepdims=True)
        idx_out = jnp.where(lane == kx, idxs[kx], idx_out)
        gt_out = jnp.where(lane == kx, exps[kx] / denom, gt_out)
        rank_out = jnp.where(lane == kx, rank_k, rank_out)
    idx_ref[...] = idx_out.astype(I32)
    gt_ref[...] = gt_out
    rank_ref[...] = rank_out.astype(I32)
    total = carry_ref[...] + jnp.sum(onehot, 0, keepdims=True)
    carry_ref[...] = total
    cnt_ref[...] = total.astype(I32)


def _merge_route(fox, rw, gates, h, wa, wb, wo, ln_g, ln_b, wr_hi, wr_lo, br, dn_alpha):
    N, D = h.shape
    tm = _largest_tile(N, 256)
    tile = lambda w: pl.BlockSpec((tm, w), lambda i: (i, 0))
    return pl.pallas_call(
        functools.partial(_merge_kernel, dn_alpha=dn_alpha),
        grid=(N // tm,),
        in_specs=[tile(HW), tile(HW), tile(2 * D), tile(D), _full(wa.shape), _full(wb.shape), _full(wo.shape),
                  _full((1, D)), _full((1, D)), _full(wr_hi.shape), _full(wr_lo.shape), _full(br.shape)],
        out_specs=[tile(D), tile(D // 2), tile(LANES), tile(LANES), tile(LANES), _full((1, LANES))],
        out_shape=[jax.ShapeDtypeStruct((N, D), F32), jax.ShapeDtypeStruct((N, D // 2), U32),
                   jax.ShapeDtypeStruct((N, LANES), I32),
                   jax.ShapeDtypeStruct((N, LANES), F32), jax.ShapeDtypeStruct((N, LANES), I32),
                   jax.ShapeDtypeStruct((1, LANES), I32)],
        scratch_shapes=[pltpu.VMEM((1, LANES), F32)],
        compiler_params=_params("arbitrary"),
        name="merge_route",
    )(fox, rw, gates, h, wa, wb, wo, ln_g, ln_b, wr_hi, wr_lo, br)


PERM_W = 256


def _deinterleave_to_bf16(w_ref, g_ref, l_ref):
    half = PERM_W // 2
    ii = lax.broadcasted_iota(I32, (PERM_W, PERM_W), 0)
    jj = lax.broadcasted_iota(I32, (PERM_W, PERM_W), 1)
    src = jnp.where(jj < half, 2 * jj, 2 * (jj - half) + 1)
    perm = jnp.where(ii == src, 1.0, 0.0).astype(BF16)
    for c in range(w_ref.shape[1] // PERM_W):
        w = w_ref[:, c * PERM_W:(c + 1) * PERM_W].astype(BF16)
        out = _dot(w, perm)
        g_ref[:, c * half:(c + 1) * half] = out[:, :half].astype(BF16)
        l_ref[:, c * half:(c + 1) * half] = out[:, half:].astype(BF16)


def _dispatch_kernel(pad_ref, dest_ref, x_ref, xs_hbm, zero_ref, sem, *, n_pad):
    i = pl.program_id(0)
    tm = x_ref.shape[0]

    if n_pad:
        @pl.when(i == 0)
        def _():
            zero_ref[...] = jnp.zeros_like(zero_ref)

            def zbody(r, c):
                pltpu.make_async_copy(zero_ref.at[pl.ds(0, 1)], xs_hbm.at[pl.ds(pad_ref[r], 1)], sem.at[1]).start()
                return c
            lax.fori_loop(0, n_pad, zbody, 0, unroll=8)
            for _ in range(n_pad // tm):
                pltpu.make_async_copy(zero_ref, xs_hbm.at[pl.ds(0, tm)], sem.at[1]).wait()

    def body(r, c):
        for kx in range(TOP_K):
            d = dest_ref[0, 0, r * TOP_K + kx]
            pltpu.make_async_copy(x_ref.at[pl.ds(r, 1)], xs_hbm.at[pl.ds(d, 1)], sem.at[0]).start()
        return c
    lax.fori_loop(0, tm, body, 0, unroll=8)
    for _ in range(TOP_K):
        pltpu.make_async_copy(x_ref, xs_hbm.at[pl.ds(0, tm)], sem.at[0]).wait()


def _moe_dispatch(xp, dest, pad_slots, rows):
    N, W = xp.shape
    tm = _largest_tile(N, 256)
    n = N // tm
    n_pad = pad_slots.shape[0]
    assert n_pad % tm == 0, (n_pad, tm)
    grid_spec = pltpu.PrefetchScalarGridSpec(
        num_scalar_prefetch=1,
        grid=(n,),
        in_specs=[pl.BlockSpec((1, 1, tm * TOP_K), lambda i, pad: (i, 0, 0), memory_space=pltpu.SMEM),
                  pl.BlockSpec((tm, W), lambda i, pad: (i, 0))],
        out_specs=pl.BlockSpec(memory_space=pl.ANY),
        scratch_shapes=[pltpu.VMEM((tm, W), U32), pltpu.SemaphoreType.DMA((2,))],
    )
    return pl.pallas_call(
        functools.partial(_dispatch_kernel, n_pad=n_pad),
        grid_spec=grid_spec,
        out_shape=jax.ShapeDtypeStruct((rows, W), U32),
        compiler_params=_params("arbitrary"),
        name="moe_dispatch",
    )(pad_slots, dest.reshape(n, 1, tm * TOP_K), xp)


def _moe_kernel(be_ref, nused_ref, xs_ref, w1_ref, b1g_ref, b1l_ref, w2_ref, b2_ref, y_ref, w1g_s, w1l_s, w2_s):
    j = pl.program_id(0)
    nused = nused_ref[0]
    last = jnp.maximum(nused - 1, 0)
    e_now = be_ref[jnp.minimum(j, last)]
    e_before = be_ref[jnp.minimum(jnp.maximum(j - 1, 0), last)]

    @pl.when(jnp.logical_or(j == 0, e_now != e_before))
    def _():
        _deinterleave_to_bf16(w1_ref, w1g_s, w1l_s)
        w2_s[...] = w2_ref[...].astype(BF16)

    @pl.when(j < nused)
    def _():
        lo, hi = _unpack_bf16_pair(xs_ref[...])
        x = jnp.concatenate([lo, hi], axis=1).astype(BF16)
        glu = jnp.minimum(_dot(x, w1g_s[...]) + b1g_ref[...], SWIGLU_LIMIT)
        lin = jnp.clip(_dot(x, w1l_s[...]) + b1l_ref[...], -SWIGLU_LIMIT, SWIGLU_LIMIT)
        act = glu * _sigmoid(SWIGLU_ALPHA * glu) * (lin + 1.0)
        y_ref[...] = _pack_bf16_pair(_dot(act.astype(BF16), w2_s[...]) + b2_ref[...])

    @pl.when(j >= nused)
    def _():
        y_ref[...] = jnp.zeros_like(y_ref)


def _moe_experts(xs, blk_e, nused, w1, b1g, b1l, w2, b2, bm):
    rows, W = xs.shape
    nb = rows // bm
    D, F = w1.shape[1], w1.shape[2] // 2
    last = lambda j, be, nu: jnp.minimum(j, jnp.maximum(nu[0] - 1, 0))
    wspec = lambda k, n: pl.BlockSpec((None, k, n), lambda j, be, nu: (be[last(j, be, nu)], 0, 0))
    grid_spec = pltpu.PrefetchScalarGridSpec(
        num_scalar_prefetch=2,
        grid=(nb,),
        in_specs=[pl.BlockSpec((bm, W), lambda j, be, nu: (j, 0)),
                  wspec(D, 2 * F), wspec(1, F), wspec(1, F), wspec(F, D), wspec(1, D)],
        out_specs=pl.BlockSpec((bm, W), lambda j, be, nu: (j, 0)),
        scratch_shapes=[pltpu.VMEM((D, F), BF16), pltpu.VMEM((D, F), BF16), pltpu.VMEM((F, D), BF16)],
    )
    return pl.pallas_call(
        _moe_kernel,
        grid_spec=grid_spec,
        out_shape=jax.ShapeDtypeStruct((rows, W), U32),
        compiler_params=_params("arbitrary"),
        name="moe_experts",
    )(blk_e, nused, xs, w1, b1g, b1l, w2, b2)


def _combine_gather_start(dest_ref, y_hbm, buf, sem, slot, tm):
    def body(r, c):
        for kx in range(TOP_K):
            d = dest_ref[0, 0, r * TOP_K + kx]
            pltpu.make_async_copy(y_hbm.at[pl.ds(d, 1)], buf.at[slot, kx, pl.ds(r, 1)], sem.at[slot]).start()
        return c
    lax.fori_loop(0, tm, body, 0, unroll=8)


def _combine_kernel(dest_ref, destn_ref, gt_ref, h1_ref, g_ref, b_ref, y_hbm, o_ref, buf, sem, *, dn_alpha):
    i = pl.program_id(0)
    n = pl.num_programs(0)
    tm = h1_ref.shape[0]
    slot = i % 2

    @pl.when(i == 0)
    def _():
        _combine_gather_start(dest_ref, y_hbm, buf, sem, 0, tm)

    @pl.when(i + 1 < n)
    def _():
        _combine_gather_start(destn_ref, y_hbm, buf, sem, 1 - slot, tm)

    for kx in range(TOP_K):
        pltpu.make_async_copy(y_hbm.at[pl.ds(0, tm)], buf.at[slot, kx], sem.at[slot]).wait()
    gt = gt_ref[...]
    lo, hi = _unpack_bf16_pair(buf[slot, 0])
    moe_lo, moe_hi = gt[:, 0:1] * lo, gt[:, 0:1] * hi
    for kx in range(1, TOP_K):
        lo, hi = _unpack_bf16_pair(buf[slot, kx])
        moe_lo = moe_lo + gt[:, kx:kx + 1] * lo
        moe_hi = moe_hi + gt[:, kx:kx + 1] * hi
    moe = jnp.concatenate([moe_lo, moe_hi], axis=1)
    o_ref[...] = _layer_norm(dn_alpha * h1_ref[...] + moe, g_ref[...], b_ref[...])


def _moe_combine(dest, gate, h1, ln_g, ln_b, yb, dn_alpha):
    N, D = h1.shape
    tm = _largest_tile(N, 256)
    n = N // tm
    dest3 = dest.reshape(n, 1, tm * TOP_K)
    tile = lambda w: pl.BlockSpec((tm, w), lambda i: (i, 0))
    return pl.pallas_call(
        functools.partial(_combine_kernel, dn_alpha=dn_alpha),
        grid=(n,),
        in_specs=[
            pl.BlockSpec((1, 1, tm * TOP_K), lambda i: (i, 0, 0), memory_space=pltpu.SMEM),
            pl.BlockSpec((1, 1, tm * TOP_K), lambda i: (jnp.minimum(i + 1, n - 1), 0, 0), memory_space=pltpu.SMEM),
            tile(LANES), tile(D), _full((1, D)), _full((1, D)),
            pl.BlockSpec(memory_space=pl.ANY),
        ],
        out_specs=tile(D),
        out_shape=jax.ShapeDtypeStruct((N, D), F32),
        scratch_shapes=[pltpu.VMEM((2, TOP_K, tm, yb.shape[1]), U32), pltpu.SemaphoreType.DMA((2,))],
        compiler_params=_params("arbitrary"),
        name="moe_combine",
    )(dest3, dest3, gate, h1, ln_g, ln_b, yb)


MOE_BLOCK_ROWS = 512


def _route_tables(top_idx, rank, counts, n_experts):
    n_tok = top_idx.shape[0]
    n_asg = n_tok * TOP_K
    bm = min(MOE_BLOCK_ROWS, max(8, 1 << int(math.log2(max(1, n_asg // n_experts)))))
    nb = -(-n_asg // bm) + n_experts
    padded = (counts + bm - 1) // bm * bm
    pends = jnp.cumsum(padded)
    dest = (pends - padded)[top_idx] + rank
    blk_start = jnp.arange(nb, dtype=I32) * bm
    blk_e = jnp.minimum(jnp.sum(pends[None, :] <= blk_start[:, None], axis=1), n_experts - 1).astype(I32)
    nused = (pends[-1] // bm).astype(I32).reshape(1)
    n_pad = nb * bm - n_asg
    gap = padded - counts
    gap_end = jnp.cumsum(gap)
    i = jnp.arange(n_pad, dtype=I32)
    e = jnp.sum(gap_end[None, :] <= i[:, None], axis=1)
    ec = jnp.minimum(e, n_experts - 1)
    in_group = (pends - padded + counts)[ec] + i - (gap_end - gap)[ec]
    pad_slots = jnp.where(e < n_experts, in_group, pends[-1] + i - gap_end[-1]).astype(I32)
    return dest.astype(I32), pad_slots, blk_e, nused, bm, nb * bm


def _stream(x, prev_row, s0, past_k, past_v, past_logf, wts):
    B, T, D = x.shape
    dn_alpha = wts['dn_alpha']
    h, q, k, v, rkv, gates, lmid, logf, rkv0 = _inproj(
        x, prev_row, wts['ln0_g'], wts['ln0_b'], wts['wm'], wts['wff'], wts['bff'], wts['mu3'],
        wts['w1'], wts['a1'], wts['g1'])
    if past_k is None:
        k_aug, v_bf = _fox_prep(logf, k, v, wts['sel_c'])
        fox = _fox_attention(q, k_aug, v_bf, wts['sel_q'], 0)
    else:
        P = past_k.shape[1]
        past_pad = jnp.pad(past_logf.astype(F32), ((0, 0), (0, 0), (0, LANES - N_HEADS)))
        k_aug, v_bf = _fox_prep(jnp.concatenate([past_pad, logf], axis=1),
                                jnp.concatenate([past_k.reshape(B, P, HW), k], axis=1),
                                jnp.concatenate([past_v.reshape(B, P, HW), v], axis=1), wts['sel_c'])
        fox = _fox_attention(q, k_aug, v_bf, wts['sel_q'], P)
    rw, s_fin = _rwkv(rkv, lmid, rkv0, s0, wts['mu_rkv'], wts['w0'], wts['w2'], wts['a0'], wts['a2'], wts['g2'],
                      wts['k_k'], wts['k_a'], wts['r_k'], wts['gn_g'], wts['gn_b'])
    N = B * T
    h1, h1p, top_idx, gate, rank, counts = _merge_route(
        fox.reshape(N, HW), rw.reshape(N, HW), gates.reshape(N, 2 * D), h.reshape(N, D),
        wts['w_up_a'], wts['w_up_b'], wts['w_out'], wts['ln1_g'], wts['ln1_b'],
        wts['wr_hi'], wts['wr_lo'], wts['br'], dn_alpha)
    n_experts = wts['n_experts']
    dest, pad_slots, blk_e, nused, bm, rows = _route_tables(top_idx[:, :TOP_K], rank[:, :TOP_K],
                                                            counts[0, :n_experts], n_experts)
    xs = _moe_dispatch(h1p, dest, pad_slots, rows)
    yb = _moe_experts(xs, blk_e, nused, wts['we1'], wts['b1g'], wts['b1l'], wts['we2'], wts['be2'], bm)
    y = _moe_combine(dest, gate, h1, wts['ln2_g'], wts['ln2_b'], yb, dn_alpha)
    new_k = k.reshape(B, T, N_HEADS, HEAD_DIM)
    new_v = v.reshape(B, T, N_HEADS, HEAD_DIM)
    return y.reshape(B, T, D), new_k, new_v, logf[:, :, :N_HEADS], s_fin, h[:, T - 1:T, :]


def kernel(x_prompt, x_sample, cache_fox_k, cache_fox_v, cache_fox_logf, state_rwkv, state_shift, meta, ln0_g, ln0_b, w_in, b_forget, mu_w, mu_a, mu_g, mu_rkv, w0, w1, w2, a0, a1, a2, g1, g2, k_k, k_a, r_k, gn_g, gn_b, w_up_a, w_up_b, w_out, ln1_g, ln1_b, w_router, b_router, w_e1, b_e1, w_e2, b_e2, ln2_g, ln2_b):
    depth, D, in_cols = w_in.shape
    assert depth == 1 and D == 1024 and in_cols == 6 * HW + N_HEADS + 2 * D
    n_experts = w_router.shape[2]
    assert n_experts <= LANES
    B = x_prompt.shape[0]
    l = 0
    w = w_in[l]
    off_ff = 3 * HW
    row = lambda a: a.reshape(1, -1).astype(F32)
    wr = jnp.pad(w_router[l], ((0, 0), (0, LANES - n_experts)))
    wr_hi = wr.astype(BF16)
    sel_q, sel_c = _aug_select_matrices()
    wts = dict(
        sel_q=sel_q, sel_c=sel_c,
        dn_alpha=float((2 * depth) ** 0.25), n_experts=n_experts,
        ln0_g=row(ln0_g), ln0_b=row(ln0_b),
        wm=jnp.concatenate([w[:, :off_ff], w[:, off_ff + N_HEADS:]], axis=1).astype(BF16),
        wff=jnp.pad(w[:, off_ff:off_ff + N_HEADS], ((0, 0), (0, LANES - N_HEADS))).astype(BF16),
        bff=jnp.pad(row(b_forget[l]), ((0, 0), (0, LANES - N_HEADS))),
        mu3=jnp.stack([mu_w[l], mu_a[l], mu_g[l]], axis=0),
        w1=w1[l].astype(BF16), a1=a1[l].astype(BF16), g1=g1[l].astype(BF16),
        mu_rkv=row(mu_rkv[l]), w0=row(w0[l]), w2=w2[l].astype(BF16), a0=row(a0[l]), a2=a2[l].astype(BF16),
        g2=g2[l].astype(BF16), k_k=row(k_k[l]), k_a=row(k_a[l]), r_k=row(r_k[l]), gn_g=row(gn_g[l]),
        gn_b=row(gn_b[l]),
        w_up_a=w_up_a[l].astype(BF16), w_up_b=w_up_b[l].astype(BF16), w_out=w_out[l].astype(BF16),
        ln1_g=row(ln1_g[l]), ln1_b=row(ln1_b[l]),
        wr_hi=wr_hi, wr_lo=(wr - wr_hi.astype(F32)).astype(BF16),
        br=jnp.pad(row(b_router[l]), ((0, 0), (0, LANES - n_experts)), constant_values=NEG_BIG),
        we1=w_e1[l],
        b1g=b_e1[l][:, None, 0::2], b1l=b_e1[l][:, None, 1::2],
        we2=w_e2[l], be2=b_e2[l][:, None, :],
        ln2_g=row(ln2_g[l]), ln2_b=row(ln2_b[l]),
    )
    meta_b = jnp.broadcast_to(meta, (B, N_META, D)).astype(x_prompt.dtype)
    xp = jnp.concatenate([meta_b, x_prompt], axis=1)
    zero_row = jnp.zeros((B, 1, D), F32)
    zero_state = jnp.zeros((B, N_HEADS, HEAD_DIM, HEAD_DIM), F32)
    y_p, k_p, v_p, lf_p, s_p, sh_p = _stream(xp, zero_row, zero_state, None, None, None, wts)
    y_s, k_s, v_s, lf_s, s_s, sh_s = _stream(x_sample, state_shift[l], state_rwkv[l], cache_fox_k[l],
                                             cache_fox_v[l], cache_fox_logf[l], wts)
    ex = lambda a: a[None]
    return (y_p[:, N_META:], y_s, ex(k_p), ex(v_p), ex(lf_p), ex(s_p), ex(sh_p),
            ex(k_s), ex(v_s), ex(lf_s), ex(s_s), ex(sh_s))
```

```python
import functools
import math

import jax
import jax.numpy as jnp
from jax import lax
from jax.experimental import pallas as pl
from jax.experimental.pallas import tpu as pltpu

F32 = jnp.float32
BF16 = jnp.bfloat16
I32 = jnp.int32
U32 = jnp.uint32

N_META = 16
HEAD_DIM = 64
N_HEADS = 8
HW = N_HEADS * HEAD_DIM
TOP_K = 4
SWIGLU_LIMIT = 7.0
SWIGLU_ALPHA = 1.702
LN_EPS = 1e-5
GN_EPS = 64e-5
LANES = 128
NEG_BIG = -1e30
VMEM_LIMIT_BYTES = 56 * 1024 * 1024
HIGHEST = lax.Precision.HIGHEST

NT_DIMS = (((1,), (1,)), ((), ()))
TN_DIMS = (((0,), (0,)), ((), ()))


def _params(*sem):
    return pltpu.CompilerParams(dimension_semantics=sem, vmem_limit_bytes=VMEM_LIMIT_BYTES)


def _largest_tile(n, cap, mult=8):
    best = None
    for d in range(mult, min(n, cap) + 1, mult):
        if n % d == 0:
            best = d
    assert best is not None, (n, cap, mult)
    return best


def _sigmoid(x):
    return 1.0 / (1.0 + jnp.exp(-x))


def _softplus(x):
    return jnp.maximum(x, 0.0) + jnp.log1p(jnp.exp(-jnp.abs(x)))


def _layer_norm(x, g, b):
    mu = jnp.mean(x, -1, keepdims=True)
    xc = x - mu
    var = jnp.mean(xc * xc, -1, keepdims=True)
    return xc * lax.rsqrt(var + LN_EPS) * g + b


def _dot(a, b):
    return jnp.dot(a, b, preferred_element_type=F32)


def _full(shape):
    n = len(shape)
    return pl.BlockSpec(shape, lambda *_: (0,) * n)


def _pack_bf16_pair(x):
    w = x.shape[1] // 2
    bits = lambda t: lax.bitcast_convert_type(t.astype(BF16).astype(F32), U32)
    return (bits(x[:, :w]) >> 16) | (bits(x[:, w:]) & jnp.uint32(0xFFFF0000))


def _unpack_bf16_pair(u):
    return lax.bitcast_convert_type(u << 16, F32), lax.bitcast_convert_type(u & jnp.uint32(0xFFFF0000), F32)


C_Q, C_K, C_V, C_RKV, C_GA, C_END = 0, HW, 2 * HW, 3 * HW, 6 * HW, 6 * HW + 2048


def _inproj_kernel(x_ref, prev_ref, g_ref, b_ref, wm_ref, wff_ref, bff_ref, mu_ref, w1_ref, a1_ref, g1_ref,
                   h_ref, q_ref, k_ref, v_ref, rkv_ref, gate_ref, lmid_ref, logf_ref, rkv0_ref,
                   carry_ref):
    t = pl.program_id(1)
    tt = x_ref.shape[0]
    h = _layer_norm(x_ref[...], g_ref[...], b_ref[...])
    h_ref[...] = h

    @pl.when(t == 0)
    def _():
        prev = prev_ref[...]
        carry_ref[...] = prev
        p8 = jnp.broadcast_to(prev, (8, prev.shape[1])).astype(BF16)
        rkv0_ref[...] = _dot(p8, wm_ref[:, C_RKV:C_GA])[0:1]

    rows = lax.broadcasted_iota(I32, h.shape, 0)
    hprev = jnp.where(rows == 0, carry_ref[...], pltpu.roll(h, 1, axis=0))
    carry_ref[...] = h[tt - 1:tt, :]
    dx = hprev - h
    hb = h.astype(BF16)
    q_ref[...] = _dot(hb, wm_ref[:, C_Q:C_K]).astype(BF16)
    k_ref[...] = _dot(hb, wm_ref[:, C_K:C_V])
    v_ref[...] = _dot(hb, wm_ref[:, C_V:C_RKV])
    rkv_ref[...] = _dot(hb, wm_ref[:, C_RKV:C_GA]).astype(BF16)
    gate_ref[...] = _sigmoid(_dot(hb, wm_ref[:, C_GA:C_END])).astype(BF16)
    ff = _dot(hb, wff_ref[...]) + bff_ref[...]
    logf_ref[...] = -_softplus(-ff)
    mu = mu_ref[...]
    lmid_ref[:, 0:64] = _dot((h + dx * mu[0:1]).astype(BF16), w1_ref[...])
    lmid_ref[:, 64:128] = _dot((h + dx * mu[1:2]).astype(BF16), a1_ref[...])
    lmid_ref[:, 128:256] = _dot((h + dx * mu[2:3]).astype(BF16), g1_ref[...])


def _inproj(x, prev_row, ln_g, ln_b, wm, wff, bff, mu3, w1, a1, g1):
    B, T, D = x.shape
    tt = _largest_tile(T, 384)
    nt = T // tt
    tile = lambda w: pl.BlockSpec((None, tt, w), lambda b, t: (b, t, 0))
    row = lambda w: pl.BlockSpec((None, 1, w), lambda b, t: (b, 0, 0))
    out_shape = [
        jax.ShapeDtypeStruct((B, T, D), F32),
        jax.ShapeDtypeStruct((B, T, HW), BF16),
        jax.ShapeDtypeStruct((B, T, HW), F32),
        jax.ShapeDtypeStruct((B, T, HW), F32),
        jax.ShapeDtypeStruct((B, T, 3 * HW), BF16),
        jax.ShapeDtypeStruct((B, T, 2 * D), BF16),
        jax.ShapeDtypeStruct((B, T, 256), F32),
        jax.ShapeDtypeStruct((B, T, LANES), F32),
        jax.ShapeDtypeStruct((B, 1, 3 * HW), F32),
    ]
    return pl.pallas_call(
        _inproj_kernel,
        grid=(B, nt),
        in_specs=[tile(D), row(D), _full((1, D)), _full((1, D)), _full(wm.shape), _full(wff.shape),
                  _full(bff.shape), _full(mu3.shape), _full(w1.shape), _full(a1.shape), _full(g1.shape)],
        out_specs=[tile(D), tile(HW), tile(HW), tile(HW), tile(3 * HW), tile(2 * D), tile(256), tile(LANES),
                   row(3 * HW)],
        out_shape=out_shape,
        scratch_shapes=[pltpu.VMEM((1, D), F32)],
        compiler_params=_params("arbitrary", "arbitrary"),
        name="inproj",
    )(x, prev_row, ln_g, ln_b, wm, wff, bff, mu3, w1, a1, g1)


HEAD_PAD = 2 * HEAD_DIM
C_SPLIT = 3


def _aug_select_matrices():
    rows = jnp.arange(HW)
    sel_k = jnp.zeros((HW, N_HEADS * HEAD_PAD), F32).at[rows, (rows // HEAD_DIM) * HEAD_PAD + rows % HEAD_DIM].set(1.0)
    p = jnp.repeat(jnp.arange(C_SPLIT), N_HEADS)
    h = jnp.tile(jnp.arange(N_HEADS), C_SPLIT)
    sel_c = jnp.zeros((LANES, N_HEADS * HEAD_PAD), F32).at[p * N_HEADS + h, h * HEAD_PAD + HEAD_DIM + p].set(1.0)
    return (sel_k * HEAD_DIM ** -0.5).astype(BF16), sel_c.astype(BF16)


def _split3(x):
    hi = x.astype(BF16)
    r1 = x - hi.astype(F32)
    mid = r1.astype(BF16)
    return hi, mid, (r1 - mid.astype(F32)).astype(BF16)


def _fox_prep_kernel(lf_ref, k_ref, v_ref, selc_ref, ka_ref, vt_ref, carry_ref):
    t = pl.program_id(1)
    tt = lf_ref.shape[0]

    @pl.when(t == 0)
    def _():
        carry_ref[...] = jnp.zeros_like(carry_ref)

    r = lax.broadcasted_iota(I32, (tt, tt), 0)
    c = lax.broadcasted_iota(I32, (tt, tt), 1)
    tri = jnp.where(r >= c, 1.0, 0.0).astype(BF16)
    cs3 = _dot(tri, jnp.concatenate(_split3(lf_ref[...]), axis=1))
    cs = cs3[:, 0:LANES] + cs3[:, LANES:2 * LANES] + cs3[:, 2 * LANES:3 * LANES] + carry_ref[...]
    carry_ref[...] = cs[tt - 1:tt, :]
    hi, mid, lo = _split3(-cs)
    is_head = lax.broadcasted_iota(I32, (tt, LANES), 1) < N_HEADS
    keep = lambda part: jnp.where(is_head, part.astype(F32), 0.0)
    packed = keep(hi) + pltpu.roll(keep(mid), N_HEADS, axis=1) + pltpu.roll(keep(lo), 2 * N_HEADS, axis=1)
    kc = _dot(packed.astype(BF16), selc_ref[...])
    k = k_ref[...]
    pad = jnp.zeros((tt, HEAD_PAD - HEAD_DIM), F32)
    for h in range(N_HEADS):
        hp = slice(h * HEAD_PAD, (h + 1) * HEAD_PAD)
        ka_ref[:, hp] = (jnp.concatenate([k[:, h * HEAD_DIM:(h + 1) * HEAD_DIM], pad], axis=1) + kc[:, hp]).astype(BF16)
    ii = lax.broadcasted_iota(I32, (HW, HW), 0)
    jj = lax.broadcasted_iota(I32, (HW, HW), 1)
    eye = jnp.where(ii == jj, 1.0, 0.0).astype(BF16)
    vt_ref[...] = lax.dot_general(eye, v_ref[...].astype(BF16), NT_DIMS, preferred_element_type=F32).astype(BF16)


def _fox_prep(logf, k, v, sel_c):
    B, T, _ = k.shape
    tt = _largest_tile(T, 384)
    nt = T // tt
    tile = lambda w: pl.BlockSpec((None, tt, w), lambda b, t: (b, t, 0))
    return pl.pallas_call(
        _fox_prep_kernel,
        grid=(B, nt),
        in_specs=[tile(LANES), tile(HW), tile(HW), _full(sel_c.shape)],
        out_specs=[tile(N_HEADS * HEAD_PAD), pl.BlockSpec((None, None, HW, tt), lambda b, t: (b, t, 0, 0))],
        out_shape=[jax.ShapeDtypeStruct((B, T, N_HEADS * HEAD_PAD), BF16),
                   jax.ShapeDtypeStruct((B, nt, HW, tt), BF16)],
        scratch_shapes=[pltpu.VMEM((1, LANES), F32)],
        compiler_params=_params("arbitrary", "arbitrary"),
        name="fox_prep",
    )(logf, k, v, sel_c)


def _fox_kernel(qi_ref, ki_ref, last_ref, q_ref, ka_ref, vt_ref, selq_ref, o_ref, qa_ref, m_ref, l_ref, acc_ref,
                *, q0, tq, tk):
    p = pl.program_id(1)
    qi = qi_ref[p]
    ki = ki_ref[p]

    @pl.when(ki == 0)
    def _():
        m_ref[...] = jnp.full_like(m_ref, NEG_BIG)
        l_ref[...] = jnp.zeros_like(l_ref)
        acc_ref[...] = jnp.zeros_like(acc_ref)
        lane = lax.broadcasted_iota(I32, qa_ref.shape, 1) % HEAD_PAD
        ones = jnp.where(jnp.logical_and(lane >= HEAD_DIM, lane < HEAD_DIM + C_SPLIT), 1.0, 0.0)
        qa_ref[...] = (_dot(q_ref[...], selq_ref[...]) + ones).astype(BF16)

    first_q = q0 + qi * tq
    tile_first = ki * tk
    tile_last = tile_first + tk - 1

    def scores(h):
        hp = slice(h * HEAD_PAD, (h + 1) * HEAD_PAD)
        return lax.dot_general(ka_ref[:, hp], qa_ref[:, hp], NT_DIMS, preferred_element_type=F32)

    def tile_update(masked):
        if masked:
            key_pos = tile_first + lax.broadcasted_iota(I32, (tk, tq), 0)
            qry_pos = first_q + lax.broadcasted_iota(I32, (tk, tq), 1)
            bias = jnp.where(qry_pos >= key_pos, 0.0, NEG_BIG)
        m_all = m_ref[...]
        l_all = l_ref[...]
        m_rows, l_rows = [], []
        s_next = scores(0)
        for h in range(N_HEADS):
            hs = slice(h * HEAD_DIM, (h + 1) * HEAD_DIM)
            s = s_next
            if h + 1 < N_HEADS:
                s_next = scores(h + 1)
            if masked:
                s = s + bias
            m_prev = m_all[h:h + 1, :]
            m_new = jnp.maximum(m_prev, jnp.max(s, 0, keepdims=True))
            alpha = jnp.exp(m_prev - m_new)
            p = jnp.exp(s - m_new)
            l_rows.append(alpha * l_all[h:h + 1, :] + jnp.sum(p, 0, keepdims=True))
            m_rows.append(m_new)
            acc_ref[hs, :] = alpha * acc_ref[hs, :] + _dot(vt_ref[hs, :], p.astype(BF16))
        m_ref[...] = jnp.concatenate(m_rows, axis=0)
        l_ref[...] = jnp.concatenate(l_rows, axis=0)

    @pl.when(jnp.logical_and(tile_first <= first_q + tq - 1, tile_last > first_q))
    def _():
        tile_update(True)

    @pl.when(tile_last <= first_q)
    def _():
        tile_update(False)

    @pl.when(last_ref[p] == 1)
    def _():
        on = jnp.concatenate(
            [acc_ref[h * HEAD_DIM:(h + 1) * HEAD_DIM, :] / l_ref[h:h + 1, :] for h in range(N_HEADS)], axis=0)
        eye = jnp.where(lax.broadcasted_iota(I32, (tq, tq), 0) == lax.broadcasted_iota(I32, (tq, tq), 1),
                        1.0, 0.0).astype(BF16)
        o_ref[...] = lax.dot_general(eye, on.astype(BF16), NT_DIMS, preferred_element_type=F32).astype(o_ref.dtype)


def _fox_attention(q, k_aug, v_t, sel_q, q0):
    B, Tq, _ = q.shape
    Tk = k_aug.shape[1]
    WA = N_HEADS * HEAD_PAD
    tq = _largest_tile(Tq, 384)
    nk, tk = v_t.shape[1], v_t.shape[3]
    assert nk * tk == Tk
    nq = Tq // tq
    pairs = [(qi, ki) for qi in range(nq) for ki in range(min((q0 + (qi + 1) * tq - 1) // tk, nk - 1) + 1)]
    qi_tab = jnp.array([p[0] for p in pairs], I32)
    ki_tab = jnp.array([p[1] for p in pairs], I32)
    last_tab = jnp.array([int(i + 1 == len(pairs) or pairs[i + 1][0] != pairs[i][0]) for i in range(len(pairs))], I32)
    qspec = lambda w: pl.BlockSpec((None, tq, w), lambda b, p, qt, kt, lt: (b, qt[p], 0))
    grid_spec = pltpu.PrefetchScalarGridSpec(
        num_scalar_prefetch=3,
        grid=(B, len(pairs)),
        in_specs=[qspec(HW),
                  pl.BlockSpec((None, tk, WA), lambda b, p, qt, kt, lt: (b, kt[p], 0)),
                  pl.BlockSpec((None, None, HW, tk), lambda b, p, qt, kt, lt: (b, kt[p], 0, 0)),
                  pl.BlockSpec(sel_q.shape, lambda b, p, qt, kt, lt: (0, 0))],
        out_specs=qspec(HW),
        scratch_shapes=[pltpu.VMEM((tq, WA), BF16), pltpu.VMEM((N_HEADS, tq), F32),
                        pltpu.VMEM((N_HEADS, tq), F32), pltpu.VMEM((HW, tq), F32)],
    )
    return pl.pallas_call(
        functools.partial(_fox_kernel, q0=q0, tq=tq, tk=tk),
        grid_spec=grid_spec,
        out_shape=jax.ShapeDtypeStruct((B, Tq, HW), BF16),
        compiler_params=_params("arbitrary", "arbitrary"),
        name="fox_attention",
    )(qi_tab, ki_tab, last_tab, q, k_aug, v_t, sel_q)


def _rwkv_kernel(rkv_ref, lmid_ref, rkv0_ref, s0_ref, mu_ref, w0_ref, w2_ref, a0_ref, a2_ref, g2_ref,
                 kk_ref, ka_ref, rk_ref, gng_ref, gnb_ref, o_ref, sfin_ref, state_ref, carry_ref, *, chunk, levels):
    C = chunk
    NB, T = rkv_ref.shape[0], rkv_ref.shape[1]
    state_ref[...] = s0_ref[...]
    carry_ref[...] = rkv0_ref[...]
    row_w = lax.broadcasted_iota(I32, (C, 3 * HW), 0)
    row_h = lax.broadcasted_iota(I32, (C, HW), 0)
    r_i = lax.broadcasted_iota(I32, (C, C), 0)
    c_i = lax.broadcasted_iota(I32, (C, C), 1)
    strict = r_i > c_i
    incl = r_i >= c_i
    mid = C // 2 - 1 if C > 1 else 0
    hsl = [slice(h * HEAD_DIM, (h + 1) * HEAD_DIM) for h in range(N_HEADS)]
    nt = lambda x, y: lax.dot_general(x, y, NT_DIMS, preferred_element_type=F32)
    tn = lambda x, y: lax.dot_general(x, y, TN_DIMS, preferred_element_type=F32)

    def row_inputs(bb, off):
        x = rkv_ref[bb, pl.ds(off, C), :].astype(F32)
        prev = jnp.where(row_w == 0, carry_ref[bb], pltpu.roll(x, 1, axis=0))
        carry_ref[bb] = x[C - 1:C, :]
        x = x + (prev - x) * mu_ref[...]
        r, k0, v = x[:, 0:HW], x[:, HW:2 * HW], x[:, 2 * HW:3 * HW]
        lm = lmid_ref[bb, pl.ds(off, C), :]
        w_pre = w0_ref[...] + _dot(jnp.tanh(lm[:, 0:64]).astype(BF16), w2_ref[...])
        a = _sigmoid(a0_ref[...] + _dot(lm[:, 64:128].astype(BF16), a2_ref[...]))
        g = _dot(_sigmoid(lm[:, 128:256]).astype(BF16), g2_ref[...])
        w_log = -_softplus(-w_pre) - 0.5
        logdec = -jnp.exp(w_log)
        L = logdec
        sh = 1
        while sh < C:
            L = L + jnp.where(row_h >= sh, pltpu.roll(L, sh, axis=0), 0.0)
            sh *= 2
        l_mid = L[mid:mid + 1, :]
        l_tot = L[C - 1:C, :]
        return dict(r=r, v=v, a=a, g=g, kk_raw=k0 * kk_ref[...], k=k0 * (1.0 + (a - 1.0) * ka_ref[...]),
                    e_a=jnp.exp(L - logdec - l_mid), e_r=jnp.exp(L - l_mid), e_k=jnp.exp(l_mid - L),
                    e_s=jnp.exp(l_tot - L), w_tot=jnp.exp(l_tot), e_mid=jnp.exp(l_mid))

    gap = jnp.zeros((C, LANES - C), F32)
    gap2 = jnp.zeros((LANES - C, HEAD_DIM), F32)
    lane_pair = lambda left, right: jnp.concatenate([left, gap, right], axis=1)

    def chunk_body(i, carry):
        off = pl.multiple_of(i * C, C)
        rows = [row_inputs(bb, off) for bb in range(NB)]
        rk = rk_ref[...]
        chains = [(bb, h) for bb in range(NB) for h in range(N_HEADS)]
        X = range(len(chains))
        col = lambda name: [rows[bb][name][:, hsl[h]] for bb, h in chains]
        kkh = [x * lax.rsqrt(jnp.maximum(jnp.sum(x * x, -1, keepdims=True), 1e-24)) for x in col('kk_raw')]
        r_h, k_h, v_h, a_h = col('r'), col('k'), col('v'), col('a')
        e_a, e_r, e_k, e_s, w_tot, e_mid = col('e_a'), col('e_r'), col('e_k'), col('e_s'), col('w_tot'), col('e_mid')
        b_h = [kkh[c] * a_h[c] for c in X]
        ar = [jnp.concatenate([-kkh[c] * e_a[c], r_h[c] * e_r[c]], axis=0) for c in X]
        bkd = [jnp.concatenate([b_h[c] * e_k[c], gap2, k_h[c] * e_k[c]], axis=0) for c in X]
        s_old = [state_ref[bb, h] for bb, h in chains]
        gram = [nt(ar[c], bkd[c]) for c in X]
        x0 = [nt(ar[c], s_old[c] * e_mid[c]) for c in X]
        g_b = [gram[c][:, 0:C] for c in X]
        g_k = [gram[c][:, LANES:LANES + C] for c in X]
        a_mat = [jnp.where(strict, g_b[c][0:C], 0.0) for c in X]
        kv = [_dot(jnp.concatenate([jnp.where(strict, g_k[c][0:C], 0.0), jnp.where(incl, g_k[c][C:2 * C], 0.0)],
                                   axis=0), v_h[c]) for c in X]
        u = [x0[c][0:C] + kv[c][0:C] for c in X]
        for lvl in range(levels):
            if lvl + 1 < levels:
                prod = [_dot(a_mat[c], lane_pair(a_mat[c], u[c])) for c in X]
                a_mat = [prod[c][:, 0:C] for c in X]
                u = [u[c] + prod[c][:, LANES:LANES + HEAD_DIM] for c in X]
            else:
                u = [u[c] + _dot(a_mat[c], u[c]) for c in X]
        y = [x0[c][C:2 * C] + _dot(jnp.where(incl, g_b[c][C:2 * C], 0.0), u[c]) + kv[c][C:2 * C] for c in X]
        for c, (bb, h) in enumerate(chains):
            uv = jnp.concatenate([u[c], v_h[c]], axis=0)
            bks = jnp.concatenate([b_h[c] * e_s[c], k_h[c] * e_s[c]], axis=0)
            state_ref[bb, h] = s_old[c] * w_tot[c] + tn(uv, bks)
        outs = []
        for c, (bb, h) in enumerate(chains):
            mu = jnp.mean(y[c], -1, keepdims=True)
            yc = y[c] - mu
            var = jnp.mean(yc * yc, -1, keepdims=True)
            bonus = jnp.sum(r_h[c] * k_h[c] * rk[:, hsl[h]], -1, keepdims=True) * v_h[c]
            outs.append((yc * lax.rsqrt(var + GN_EPS), bonus))
        for bb in range(NB):
            mine = outs[bb * N_HEADS:(bb + 1) * N_HEADS]
            yn = jnp.concatenate([o[0] for o in mine], axis=1)
            bonus = jnp.concatenate([o[1] for o in mine], axis=1)
            out = (yn * gng_ref[...] + gnb_ref[...] + bonus) * rows[bb]['g']
            o_ref[bb, pl.ds(off, C), :] = out.astype(o_ref.dtype)
        return carry

    lax.fori_loop(0, T // C, chunk_body, 0)
    sfin_ref[...] = state_ref[...]


RWKV_ROWS_PER_STEP = 2


def _rwkv(rkv, lmid, rkv0, s0, mu_rkv, w0, w2, a0, a2, g2, k_k, k_a, r_k, gn_g, gn_b):
    B, T, _ = rkv.shape
    nb = RWKV_ROWS_PER_STEP if B % RWKV_ROWS_PER_STEP == 0 else 1
    chunk = _largest_tile(T, 64, mult=16)
    levels = max(1, math.ceil(math.log2(chunk)))
    seq =lambda w: pl.BlockSpec((nb, T, w), lambda b: (b, 0, 0))
    st = pl.BlockSpec((nb, N_HEADS, HEAD_DIM, HEAD_DIM), lambda b: (b, 0, 0, 0))
    vec = lambda a: _full(a.shape)
    params = (mu_rkv, w0, w2, a0, a2, g2, k_k, k_a, r_k, gn_g, gn_b)
    return pl.pallas_call(
        functools.partial(_rwkv_kernel, chunk=chunk, levels=levels),
        grid=(B // nb,),
        in_specs=[seq(3 * HW), seq(256), pl.BlockSpec((nb, 1, 3 * HW), lambda b: (b, 0, 0)), st]
                 + [vec(p) for p in params],
        out_specs=[seq(HW), st],
        out_shape=[jax.ShapeDtypeStruct((B, T, HW), BF16),
                   jax.ShapeDtypeStruct((B, N_HEADS, HEAD_DIM, HEAD_DIM), F32)],
        scratch_shapes=[pltpu.VMEM((nb, N_HEADS, HEAD_DIM, HEAD_DIM), F32), pltpu.VMEM((nb, 1, 3 * HW), F32)],
        compiler_params=_params("arbitrary"),
        name="rwkv7",
    )(rkv, lmid, rkv0, s0, *params)


def _merge_kernel(fox_ref, rw_ref, gate_ref, h_ref, wa_ref, wb_ref, wo_ref, g_ref, b_ref, wrh_ref, wrl_ref, br_ref,
                  h1_ref, h1p_ref, idx_ref, gt_ref, rank_ref, cnt_ref, carry_ref, *, dn_alpha):
    i = pl.program_id(0)
    tm, D = h_ref.shape

    @pl.when(i == 0)
    def _():
        carry_ref[...] = jnp.zeros_like(carry_ref)

    gates = gate_ref[...].astype(F32)
    merged = gates[:, 0:D] * _dot(fox_ref[...], wa_ref[...]) + gates[:, D:2 * D] * _dot(rw_ref[...], wb_ref[...])
    z = dn_alpha * h_ref[...] + _dot(merged.astype(BF16), wo_ref[...])
    h1 = _layer_norm(z, g_ref[...], b_ref[...])
    h1_ref[...] = h1
    h1p_ref[...] = _pack_bf16_pair(h1)
    hi = h1.astype(BF16)
    lo = (h1 - hi.astype(F32)).astype(BF16)
    logits = _dot(hi, wrh_ref[...]) + _dot(hi, wrl_ref[...]) + _dot(lo, wrh_ref[...]) + br_ref[...]
    lane = lax.broadcasted_iota(I32, (tm, LANES), 1)
    lane_f = lane.astype(F32)
    cur = logits
    vals, idxs = [], []
    for _ in range(TOP_K):
        m = jnp.max(cur, -1, keepdims=True)
        ix = jnp.min(jnp.where(cur == m, lane_f, float(LANES)), -1, keepdims=True)
        vals.append(m)
        idxs.append(ix)
        cur = jnp.where(lane_f == ix, -3e38, cur)
    exps = [jnp.exp(vk - vals[0]) for vk in vals]
    denom = exps[0] + exps[1] + exps[2] + exps[3]
    onehot = jnp.zeros((tm, LANES), F32)
    for ix in idxs:
        onehot = onehot + jnp.where(lane_f == ix, 1.0, 0.0)
    r_i = lax.broadcasted_iota(I32, (tm, tm), 0)
    c_i = lax.broadcasted_iota(I32, (tm, tm), 1)
    tri = jnp.where(r_i > c_i, 1.0, 0.0).astype(BF16)
    before = _dot(tri, onehot.astype(BF16)) + carry_ref[...]
    idx_out = jnp.zeros((tm, LANES), F32)
    gt_out = jnp.zeros((tm, LANES), F32)
    rank_out = jnp.zeros((tm, LANES), F32)
    for kx in range(TOP_K):
        rank_k = jnp.sum(jnp.where(lane_f == idxs[kx], before, 0.0), -1, keepdims=True)
        idx_out = jnp.where(lane == kx, idxs[kx], idx_out)
        gt_out = jnp.where(lane == kx, exps[kx] / denom, gt_out)
        rank_out = jnp.where(lane == kx, rank_k, rank_out)
    idx_ref[...] = idx_out.astype(I32)
    gt_ref[...] = gt_out
    rank_ref[...] = rank_out.astype(I32)
    total = carry_ref[...] + jnp.sum(onehot, 0, keepdims=True)
    carry_ref[...] = total
    cnt_ref[...] = total.astype(I32)


def _merge_route(fox, rw, gates, h, wa, wb, wo, ln_g, ln_b, wr_hi, wr_lo, br, dn_alpha):
    N, D = h.shape
    tm = _largest_tile(N, 384)
    tile = lambda w: pl.BlockSpec((tm, w), lambda i: (i, 0))
    return pl.pallas_call(
        functools.partial(_merge_kernel, dn_alpha=dn_alpha),
        grid=(N // tm,),
        in_specs=[tile(HW), tile(HW), tile(2 * D), tile(D), _full(wa.shape), _full(wb.shape), _full(wo.shape),
                  _full((1, D)), _full((1, D)), _full(wr_hi.shape), _full(wr_lo.shape), _full(br.shape)],
        out_specs=[tile(D), tile(D // 2), tile(LANES), tile(LANES), tile(LANES), _full((1, LANES))],
        out_shape=[jax.ShapeDtypeStruct((N, D), F32), jax.ShapeDtypeStruct((N, D // 2), U32),
                   jax.ShapeDtypeStruct((N, LANES), I32),
                   jax.ShapeDtypeStruct((N, LANES), F32), jax.ShapeDtypeStruct((N, LANES), I32),
                   jax.ShapeDtypeStruct((1, LANES), I32)],
        scratch_shapes=[pltpu.VMEM((1, LANES), F32)],
        compiler_params=_params("arbitrary"),
        name="merge_route",
    )(fox, rw, gates, h, wa, wb, wo, ln_g, ln_b, wr_hi, wr_lo, br)


PERM_W = 256


def _deinterleave_to_bf16(w_ref, g_ref, l_ref):
    half = PERM_W // 2
    ii = lax.broadcasted_iota(I32, (PERM_W, PERM_W), 0)
    jj = lax.broadcasted_iota(I32, (PERM_W, PERM_W), 1)
    src = jnp.where(jj < half, 2 * jj, 2 * (jj - half) + 1)
    perm = jnp.where(ii == src, 1.0, 0.0).astype(BF16)
    for c in range(w_ref.shape[1] // PERM_W):
        w = w_ref[:, c * PERM_W:(c + 1) * PERM_W].astype(BF16)
        out = _dot(w, perm)
        g_ref[:, c * half:(c + 1) * half] = out[:, :half].astype(BF16)
        l_ref[:, c * half:(c + 1) * half] = out[:, half:].astype(BF16)


def _dispatch_kernel(pad_ref, dest_ref, x_ref, xs_hbm, zero_ref, sem, *, n_pad):
    i = pl.program_id(0)
    tm = x_ref.shape[0]

    if n_pad:
        @pl.when(i == 0)
        def _():
            zero_ref[...] = jnp.zeros_like(zero_ref)

            def zbody(r, c):
                pltpu.make_async_copy(zero_ref.at[pl.ds(0, 1)], xs_hbm.at[pl.ds(pad_ref[r], 1)], sem.at[1]).start()
                return c
            lax.fori_loop(0, n_pad, zbody, 0, unroll=8)
            for _ in range(n_pad // tm):
                pltpu.make_async_copy(zero_ref, xs_hbm.at[pl.ds(0, tm)], sem.at[1]).wait()

    def body(r, c):
        for kx in range(TOP_K):
            d = dest_ref[0, 0, r * TOP_K + kx]
            pltpu.make_async_copy(x_ref.at[pl.ds(r, 1)], xs_hbm.at[pl.ds(d, 1)], sem.at[0]).start()
        return c
    lax.fori_loop(0, tm, body, 0, unroll=8)
    for _ in range(TOP_K):
        pltpu.make_async_copy(x_ref, xs_hbm.at[pl.ds(0, tm)], sem.at[0]).wait()


def _moe_dispatch(xp, dest, pad_slots, rows):
    N, W = xp.shape
    tm = _largest_tile(N, 256)
    n = N // tm
    n_pad = pad_slots.shape[0]
    assert n_pad % tm == 0, (n_pad, tm)
    grid_spec = pltpu.PrefetchScalarGridSpec(
        num_scalar_prefetch=1,
        grid=(n,),
        in_specs=[pl.BlockSpec((1, 1, tm * TOP_K), lambda i, pad: (i, 0, 0), memory_space=pltpu.SMEM),
                  pl.BlockSpec((tm, W), lambda i, pad: (i, 0))],
        out_specs=pl.BlockSpec(memory_space=pl.ANY),
        scratch_shapes=[pltpu.VMEM((tm, W), U32), pltpu.SemaphoreType.DMA((2,))],
    )
    return pl.pallas_call(
        functools.partial(_dispatch_kernel, n_pad=n_pad),
        grid_spec=grid_spec,
        out_shape=jax.ShapeDtypeStruct((rows, W), U32),
        compiler_params=_params("arbitrary"),
        name="moe_dispatch",
    )(pad_slots, dest.reshape(n, 1, tm * TOP_K), xp)


def _moe_kernel(be_ref, nused_ref, xs_ref, w1_ref, b1g_ref, b1l_ref, w2_ref, b2_ref, y_ref, w1g_s, w1l_s, w2_s):
    j = pl.program_id(0)
    nused = nused_ref[0]
    last = jnp.maximum(nused - 1, 0)
    e_now = be_ref[jnp.minimum(j, last)]
    e_before = be_ref[jnp.minimum(jnp.maximum(j - 1, 0), last)]

    @pl.when(jnp.logical_or(j == 0, e_now != e_before))
    def _():
        _deinterleave_to_bf16(w1_ref, w1g_s, w1l_s)
        w2_s[...] = w2_ref[...].astype(BF16)

    @pl.when(j < nused)
    def _():
        lo, hi = _unpack_bf16_pair(xs_ref[...])
        x = jnp.concatenate([lo, hi], axis=1).astype(BF16)
        glu = jnp.minimum(_dot(x, w1g_s[...]) + b1g_ref[...], SWIGLU_LIMIT)
        lin = jnp.clip(_dot(x, w1l_s[...]) + b1l_ref[...], -SWIGLU_LIMIT, SWIGLU_LIMIT)
        act = glu * _sigmoid(SWIGLU_ALPHA * glu) * (lin + 1.0)
        y_ref[...] = _pack_bf16_pair(_dot(act.astype(BF16), w2_s[...]) + b2_ref[...])

    @pl.when(j >= nused)
    def _():
        y_ref[...] = jnp.zeros_like(y_ref)


def _moe_experts(xs, blk_e, nused, w1, b1g, b1l, w2, b2, bm):
    rows, W = xs.shape
    nb = rows // bm
    D, F = w1.shape[1], w1.shape[2] // 2
    last = lambda j, be, nu: jnp.minimum(j, jnp.maximum(nu[0] - 1, 0))
    wspec = lambda k, n: pl.BlockSpec((None, k, n), lambda j, be, nu: (be[last(j, be, nu)], 0, 0))
    grid_spec = pltpu.PrefetchScalarGridSpec(
        num_scalar_prefetch=2,
        grid=(nb,),
        in_specs=[pl.BlockSpec((bm, W), lambda j, be, nu: (j, 0)),
                  wspec(D, 2 * F), wspec(1, F), wspec(1, F), wspec(F, D), wspec(1, D)],
        out_specs=pl.BlockSpec((bm, W), lambda j, be, nu: (j, 0)),
        scratch_shapes=[pltpu.VMEM((D, F), BF16), pltpu.VMEM((D, F), BF16), pltpu.VMEM((F, D), BF16)],
    )
    return pl.pallas_call(
        _moe_kernel,
        grid_spec=grid_spec,
        out_shape=jax.ShapeDtypeStruct((rows, W), U32),
        compiler_params=_params("arbitrary"),
        name="moe_experts",
    )(blk_e, nused, xs, w1, b1g, b1l, w2, b2)


def _combine_gather_start(dest_ref, y_hbm, buf, sem, slot, tm):
    def body(r, c):
        for kx in range(TOP_K):
            d = dest_ref[0, 0, r * TOP_K + kx]
            pltpu.make_async_copy(y_hbm.at[pl.ds(d, 1)], buf.at[slot, kx, pl.ds(r, 1)], sem.at[slot]).start()
        return c
    lax.fori_loop(0, tm, body, 0, unroll=8)


def _combine_kernel(dest_ref, destn_ref, gt_ref, h1_ref, g_ref, b_ref, y_hbm, o_hbm, buf, sem, obuf, osem,
                    *, dn_alpha, nt, skip):
    i = pl.program_id(0)
    n = pl.num_programs(0)
    tm = h1_ref.shape[0]
    slot = i % 2
    b = i // nt
    j = i % nt

    def out_wait(rows):
        pltpu.make_async_copy(obuf.at[0, pl.ds(0, rows)], o_hbm.at[0, pl.ds(0, rows)], osem.at[0]).wait()

    @pl.when(i == 0)
    def _():
        _combine_gather_start(dest_ref, y_hbm, buf, sem, 0, tm)

    @pl.when(i + 1 < n)
    def _():
        _combine_gather_start(destn_ref, y_hbm, buf, sem, 1 - slot, tm)

    for kx in range(TOP_K):
        pltpu.make_async_copy(y_hbm.at[pl.ds(0, tm)], buf.at[slot, kx], sem.at[slot]).wait()
    gt = gt_ref[...]
    lo, hi = _unpack_bf16_pair(buf[slot, 0])
    moe_lo, moe_hi = gt[:, 0:1] * lo, gt[:, 0:1] * hi
    for kx in range(1, TOP_K):
        lo, hi = _unpack_bf16_pair(buf[slot, kx])
        moe_lo = moe_lo + gt[:, kx:kx + 1] * lo
        moe_hi = moe_hi + gt[:, kx:kx + 1] * hi
    moe = jnp.concatenate([moe_lo, moe_hi], axis=1)
    obuf[slot] = _layer_norm(dn_alpha * h1_ref[...] + moe, g_ref[...], b_ref[...])

    @pl.when(jnp.logical_and(i > 0, (i - 1) % nt == 0))
    def _():
        out_wait(tm - skip)

    @pl.when(jnp.logical_and(i > 0, (i - 1) % nt != 0))
    def _():
        out_wait(tm)

    @pl.when(j == 0)
    def _():
        pltpu.make_async_copy(obuf.at[slot, pl.ds(skip, tm - skip)], o_hbm.at[b, pl.ds(0, tm - skip)],
                              osem.at[0]).start()

    @pl.when(j != 0)
    def _():
        start = pl.multiple_of(j * tm - skip, 8)
        pltpu.make_async_copy(obuf.at[slot], o_hbm.at[b, pl.ds(start, tm)], osem.at[0]).start()

    @pl.when(i == n - 1)
    def _():
        if nt == 1:
            out_wait(tm - skip)
        else:
            out_wait(tm)


def _moe_combine(dest, gate, h1, ln_g, ln_b, yb, dn_alpha, B, T, skip):
    N, D = h1.shape
    tm = _largest_tile(T, 384)
    nt = T // tm
    n = N // tm
    assert skip % 8 == 0 and skip < tm
    dest3 = dest.reshape(n, 1, tm * TOP_K)
    tile = lambda w: pl.BlockSpec((tm, w), lambda i: (i, 0))
    return pl.pallas_call(
        functools.partial(_combine_kernel, dn_alpha=dn_alpha, nt=nt, skip=skip),
        grid=(n,),
        in_specs=[
            pl.BlockSpec((1, 1, tm * TOP_K), lambda i: (i, 0, 0), memory_space=pltpu.SMEM),
            pl.BlockSpec((1, 1, tm * TOP_K), lambda i: (jnp.minimum(i + 1, n - 1), 0, 0), memory_space=pltpu.SMEM),
            tile(LANES), tile(D), _full((1, D)), _full((1, D)),
            pl.BlockSpec(memory_space=pl.ANY),
        ],
        out_specs=pl.BlockSpec(memory_space=pl.ANY),
        out_shape=jax.ShapeDtypeStruct((B, T - skip, D), F32),
        scratch_shapes=[pltpu.VMEM((2, TOP_K, tm, yb.shape[1]), U32), pltpu.SemaphoreType.DMA((2,)),
                        pltpu.VMEM((2, tm, D), F32), pltpu.SemaphoreType.DMA((1,))],
        compiler_params=_params("arbitrary"),
        name="moe_combine",
    )(dest3, dest3, gate, h1, ln_g, ln_b, yb)


MOE_BLOCK_ROWS = 512


def _route_tables(top_idx, rank, counts, n_experts):
    n_tok = top_idx.shape[0]
    n_asg = n_tok * TOP_K
    bm = min(MOE_BLOCK_ROWS, max(8, 1 << int(math.log2(max(1, n_asg // n_experts)))))
    nb = -(-n_asg // bm) + n_experts
    padded = (counts + bm - 1) // bm * bm
    pends = jnp.cumsum(padded)
    dest = (pends - padded)[top_idx] + rank
    blk_start = jnp.arange(nb, dtype=I32) * bm
    blk_e = jnp.minimum(jnp.sum(pends[None, :] <= blk_start[:, None], axis=1), n_experts - 1).astype(I32)
    nused = (pends[-1] // bm).astype(I32).reshape(1)
    n_pad = nb * bm - n_asg
    gap = padded - counts
    gap_end = jnp.cumsum(gap)
    i = jnp.arange(n_pad, dtype=I32)
    e = jnp.sum(gap_end[None, :] <= i[:, None], axis=1)
    ec = jnp.minimum(e, n_experts - 1)
    in_group = (pends - padded + counts)[ec] + i - (gap_end - gap)[ec]
    pad_slots = jnp.where(e < n_experts, in_group, pends[-1] + i - gap_end[-1]).astype(I32)
    return dest.astype(I32), pad_slots, blk_e, nused, bm, nb * bm


def _stream(x, prev_row, s0, past_k, past_v, past_logf, wts, y_skip):
    B, T, D = x.shape
    dn_alpha = wts['dn_alpha']
    h, q, k, v, rkv, gates, lmid, logf, rkv0 = _inproj(
        x, prev_row, wts['ln0_g'], wts['ln0_b'], wts['wm'], wts['wff'], wts['bff'], wts['mu3'],
        wts['w1'], wts['a1'], wts['g1'])
    if past_k is None:
        k_aug, v_bf = _fox_prep(logf, k, v, wts['sel_c'])
        fox = _fox_attention(q, k_aug, v_bf, wts['sel_q'], 0)
    else:
        P = past_k.shape[1]
        past_pad = jnp.pad(past_logf.astype(F32), ((0, 0), (0, 0), (0, LANES - N_HEADS)))
        k_aug, v_bf = _fox_prep(jnp.concatenate([past_pad, logf], axis=1),
                                jnp.concatenate([past_k.reshape(B, P, HW), k], axis=1),
                                jnp.concatenate([past_v.reshape(B, P, HW), v], axis=1), wts['sel_c'])
        fox = _fox_attention(q, k_aug, v_bf, wts['sel_q'], P)
    rw, s_fin = _rwkv(rkv, lmid, rkv0, s0, wts['mu_rkv'], wts['w0'], wts['w2'], wts['a0'], wts['a2'], wts['g2'],
                      wts['k_k'], wts['k_a'], wts['r_k'], wts['gn_g'], wts['gn_b'])
    N = B * T
    h1, h1p, top_idx, gate, rank, counts = _merge_route(
        fox.reshape(N, HW), rw.reshape(N, HW), gates.reshape(N, 2 * D), h.reshape(N, D),
        wts['w_up_a'], wts['w_up_b'], wts['w_out'], wts['ln1_g'], wts['ln1_b'],
        wts['wr_hi'], wts['wr_lo'], wts['br'], dn_alpha)
    n_experts = wts['n_experts']
    dest, pad_slots, blk_e, nused, bm, rows = _route_tables(top_idx[:, :TOP_K], rank[:, :TOP_K],
                                                            counts[0, :n_experts], n_experts)
    xs = _moe_dispatch(h1p, dest, pad_slots, rows)
    yb = _moe_experts(xs, blk_e, nused, wts['we1'], wts['b1g'], wts['b1l'], wts['we2'], wts['be2'], bm)
    y = _moe_combine(dest, gate, h1, wts['ln2_g'], wts['ln2_b'], yb, dn_alpha, B, T, y_skip)
    new_k = k.reshape(B, T, N_HEADS, HEAD_DIM)
    new_v = v.reshape(B, T, N_HEADS, HEAD_DIM)
    return y, new_k, new_v, logf[:, :, :N_HEADS], s_fin, h[:, T - 1:T, :]


def kernel(x_prompt, x_sample, cache_fox_k, cache_fox_v, cache_fox_logf, state_rwkv, state_shift, meta, ln0_g, ln0_b, w_in, b_forget, mu_w, mu_a, mu_g, mu_rkv, w0, w1, w2, a0, a1, a2, g1, g2, k_k, k_a, r_k, gn_g, gn_b, w_up_a, w_up_b, w_out, ln1_g, ln1_b, w_router, b_router, w_e1, b_e1, w_e2, b_e2, ln2_g, ln2_b):
    depth, D, in_cols = w_in.shape
    assert depth == 1 and D == 1024 and in_cols == 6 * HW + N_HEADS + 2 * D
    n_experts = w_router.shape[2]
    assert n_experts <= LANES
    B = x_prompt.shape[0]
    l = 0
    w = w_in[l]
    off_ff = 3 * HW
    row = lambda a: a.reshape(1, -1).astype(F32)
    wr = jnp.pad(w_router[l], ((0, 0), (0, LANES - n_experts)))
    wr_hi = wr.astype(BF16)
    sel_q, sel_c = _aug_select_matrices()
    wts = dict(
        sel_q=sel_q, sel_c=sel_c,
        dn_alpha=float((2 * depth) ** 0.25), n_experts=n_experts,
        ln0_g=row(ln0_g), ln0_b=row(ln0_b),
        wm=jnp.concatenate([w[:, :off_ff], w[:, off_ff + N_HEADS:]], axis=1).astype(BF16),
        wff=jnp.pad(w[:, off_ff:off_ff + N_HEADS], ((0, 0), (0, LANES - N_HEADS))).astype(BF16),
        bff=jnp.pad(row(b_forget[l]), ((0, 0), (0, LANES - N_HEADS))),
        mu3=jnp.stack([mu_w[l], mu_a[l], mu_g[l]], axis=0),
        w1=w1[l].astype(BF16), a1=a1[l].astype(BF16), g1=g1[l].astype(BF16),
        mu_rkv=row(mu_rkv[l]), w0=row(w0[l]), w2=w2[l].astype(BF16), a0=row(a0[l]), a2=a2[l].astype(BF16),
        g2=g2[l].astype(BF16), k_k=row(k_k[l]), k_a=row(k_a[l]), r_k=row(r_k[l]), gn_g=row(gn_g[l]),
        gn_b=row(gn_b[l]),
        w_up_a=w_up_a[l].astype(BF16), w_up_b=w_up_b[l].astype(BF16), w_out=w_out[l].astype(BF16),
        ln1_g=row(ln1_g[l]), ln1_b=row(ln1_b[l]),
        wr_hi=wr_hi, wr_lo=(wr - wr_hi.astype(F32)).astype(BF16),
        br=jnp.pad(row(b_router[l]), ((0, 0), (0, LANES - n_experts)), constant_values=NEG_BIG),
        we1=w_e1[l],
        b1g=b_e1[l][:, None, 0::2], b1l=b_e1[l][:, None, 1::2],
        we2=w_e2[l], be2=b_e2[l][:, None, :],
        ln2_g=row(ln2_g[l]), ln2_b=row(ln2_b[l]),
    )
    meta_b = jnp.broadcast_to(meta, (B, N_META, D)).astype(x_prompt.dtype)
    xp = jnp.concatenate([meta_b, x_prompt], axis=1)
    zero_row = jnp.zeros((B, 1, D), F32)
    zero_state = jnp.zeros((B, N_HEADS, HEAD_DIM, HEAD_DIM), F32)
    y_p, k_p, v_p, lf_p, s_p, sh_p = _stream(xp, zero_row, zero_state, None, None, None, wts, N_META)
    y_s, k_s, v_s, lf_s, s_s, sh_s = _stream(x_sample, state_shift[l], state_rwkv[l], cache_fox_k[l],
                                             cache_fox_v[l], cache_fox_logf[l], wts, 0)
    ex = lambda a: a[None]
    return (y_p, y_s, ex(k_p), ex(v_p), ex(lf_p), ex(s_p), ex(sh_p),
            ex(k_s), ex(v_s), ex(lf_s), ex(s_s), ex(sh_s))
```

```python
import functools
import math

import jax
import jax.numpy as jnp
import numpy as np
from jax import lax
from jax.experimental import pallas as pl
from jax.experimental.pallas import tpu as pltpu

F32 = jnp.float32
BF16 = jnp.bfloat16
I32 = jnp.int32
U32 = jnp.uint32

N_META = 16
HEAD_DIM = 64
N_HEADS = 8
HW = N_HEADS * HEAD_DIM
TOP_K = 4
SWIGLU_LIMIT = 7.0
SWIGLU_ALPHA = 1.702
LN_EPS = 1e-5
GN_EPS = 64e-5
LANES = 128
NEG_BIG = -1e30
VMEM_LIMIT_BYTES = 56 * 1024 * 1024
HIGHEST = lax.Precision.HIGHEST

NT_DIMS = (((1,), (1,)), ((), ()))
TN_DIMS = (((0,), (0,)), ((), ()))


def _params(*sem):
    return pltpu.CompilerParams(dimension_semantics=sem, vmem_limit_bytes=VMEM_LIMIT_BYTES)


def _largest_tile(n, cap, mult=8):
    best = None
    for d in range(mult, min(n, cap) + 1, mult):
        if n % d == 0:
            best = d
    assert best is not None, (n, cap, mult)
    return best


def _sigmoid(x):
    return 1.0 / (1.0 + jnp.exp(-x))


def _softplus(x):
    return jnp.maximum(x, 0.0) + jnp.log1p(jnp.exp(-jnp.abs(x)))


def _layer_norm(x, g, b):
    mu = jnp.mean(x, -1, keepdims=True)
    xc = x - mu
    var = jnp.mean(xc * xc, -1, keepdims=True)
    return xc * lax.rsqrt(var + LN_EPS) * g + b


def _dot(a, b):
    return jnp.dot(a, b, preferred_element_type=F32)


def _full(shape):
    n = len(shape)
    return pl.BlockSpec(shape, lambda *_: (0,) * n)


def _pack_bf16_pair(x):
    w = x.shape[1] // 2
    bits = lambda t: lax.bitcast_convert_type(t.astype(BF16).astype(F32), U32)
    return (bits(x[:, :w]) >> 16) | (bits(x[:, w:]) & jnp.uint32(0xFFFF0000))


def _unpack_bf16_pair(u):
    return lax.bitcast_convert_type(u << 16, F32), lax.bitcast_convert_type(u & jnp.uint32(0xFFFF0000), F32)


C_Q, C_K, C_V, C_RKV, C_GA, C_END = 0, HW, 2 * HW, 3 * HW, 6 * HW, 6 * HW + 2048


def _inproj_kernel(x_ref, prev_ref, g_ref, b_ref, wm_ref, wff_ref, bff_ref, mu_ref, w1_ref, a1_ref, g1_ref,
                   h_ref, q_ref, k_ref, v_ref, rkv_ref, gate_ref, lmid_ref, logf_ref, rkv0_ref,
                   carry_ref):
    t = pl.program_id(1)
    tt = x_ref.shape[0]
    h = _layer_norm(x_ref[...], g_ref[...], b_ref[...])
    h_ref[...] = h

    @pl.when(t == 0)
    def _():
        prev = prev_ref[...]
        carry_ref[...] = prev
        p8 = jnp.broadcast_to(prev, (8, prev.shape[1])).astype(BF16)
        rkv0_ref[...] = _dot(p8, wm_ref[:, C_RKV:C_GA])[0:1]

    rows = lax.broadcasted_iota(I32, h.shape, 0)
    hprev = jnp.where(rows == 0, carry_ref[...], pltpu.roll(h, 1, axis=0))
    carry_ref[...] = h[tt - 1:tt, :]
    dx = hprev - h
    hb = h.astype(BF16)
    q_ref[...] = _dot(hb, wm_ref[:, C_Q:C_K]).astype(BF16)
    k_ref[...] = _dot(hb, wm_ref[:, C_K:C_V])
    v_ref[...] = _dot(hb, wm_ref[:, C_V:C_RKV])
    rkv_ref[...] = _dot(hb, wm_ref[:, C_RKV:C_GA]).astype(BF16)
    gate_ref[...] = _sigmoid(_dot(hb, wm_ref[:, C_GA:C_END])).astype(BF16)
    ff = _dot(hb, wff_ref[...]) + bff_ref[...]
    logf_ref[...] = -_softplus(-ff)
    mu = mu_ref[...]
    lmid_ref[:, 0:64] = _dot((h + dx * mu[0:1]).astype(BF16), w1_ref[...])
    lmid_ref[:, 64:128] = _dot((h + dx * mu[1:2]).astype(BF16), a1_ref[...])
    lmid_ref[:, 128:256] = _dot((h + dx * mu[2:3]).astype(BF16), g1_ref[...])


def _inproj(x, prev_row, ln_g, ln_b, wm, wff, bff, mu3, w1, a1, g1):
    B, T, D = x.shape
    tt = _largest_tile(T, 384)
    nt = T // tt
    tile = lambda w: pl.BlockSpec((None, tt, w), lambda b, t: (b, t, 0))
    row = lambda w: pl.BlockSpec((None, 1, w), lambda b, t: (b, 0, 0))
    out_shape = [
        jax.ShapeDtypeStruct((B, T, D), F32),
        jax.ShapeDtypeStruct((B, T, HW), BF16),
        jax.ShapeDtypeStruct((B, T, HW), F32),
        jax.ShapeDtypeStruct((B, T, HW), F32),
        jax.ShapeDtypeStruct((B, T, 3 * HW), BF16),
        jax.ShapeDtypeStruct((B, T, 2 * D), BF16),
        jax.ShapeDtypeStruct((B, T, 256), F32),
        jax.ShapeDtypeStruct((B, T, LANES), F32),
        jax.ShapeDtypeStruct((B, 1, 3 * HW), F32),
    ]
    return pl.pallas_call(
        _inproj_kernel,
        grid=(B, nt),
        in_specs=[tile(D), row(D), _full((1, D)), _full((1, D)), _full(wm.shape), _full(wff.shape),
                  _full(bff.shape), _full(mu3.shape), _full(w1.shape), _full(a1.shape), _full(g1.shape)],
        out_specs=[tile(D), tile(HW), tile(HW), tile(HW), tile(3 * HW), tile(2 * D), tile(256), tile(LANES),
                   row(3 * HW)],
        out_shape=out_shape,
        scratch_shapes=[pltpu.VMEM((1, D), F32)],
        compiler_params=_params("arbitrary", "arbitrary"),
        name="inproj",
    )(x, prev_row, ln_g, ln_b, wm, wff, bff, mu3, w1, a1, g1)


HEAD_PAD = 2 * HEAD_DIM
C_SPLIT = 3


def _aug_select_matrices():
    rows = np.arange(HW)
    sel_q = np.zeros((HW, N_HEADS * HEAD_PAD), np.float32)
    sel_q[rows, (rows // HEAD_DIM) * HEAD_PAD + rows % HEAD_DIM] = HEAD_DIM ** -0.5
    p = np.repeat(np.arange(C_SPLIT), N_HEADS)
    h = np.tile(np.arange(N_HEADS), C_SPLIT)
    sel_c = np.zeros((LANES, N_HEADS * HEAD_PAD), np.float32)
    sel_c[p * N_HEADS + h, h * HEAD_PAD + HEAD_DIM + p] = 1.0
    return jnp.asarray(sel_q, BF16), jnp.asarray(sel_c, BF16)


def _split3(x):
    hi = x.astype(BF16)
    r1 = x - hi.astype(F32)
    mid = r1.astype(BF16)
    return hi, mid, (r1 - mid.astype(F32)).astype(BF16)


def _fox_prep_kernel(lf_ref, k_ref, v_ref, selc_ref, ka_ref, vt_ref, carry_ref):
    t = pl.program_id(1)
    tt = lf_ref.shape[0]

    @pl.when(t == 0)
    def _():
        carry_ref[...] = jnp.zeros_like(carry_ref)

    r = lax.broadcasted_iota(I32, (tt, tt), 0)
    c = lax.broadcasted_iota(I32, (tt, tt), 1)
    tri = jnp.where(r >= c, 1.0, 0.0).astype(BF16)
    cs3 = _dot(tri, jnp.concatenate(_split3(lf_ref[...]), axis=1))
    cs = cs3[:, 0:LANES] + cs3[:, LANES:2 * LANES] + cs3[:, 2 * LANES:3 * LANES] + carry_ref[...]
    carry_ref[...] = cs[tt - 1:tt, :]
    hi, mid, lo = _split3(-cs)
    is_head = lax.broadcasted_iota(I32, (tt, LANES), 1) < N_HEADS
    keep = lambda part: jnp.where(is_head, part.astype(F32), 0.0)
    packed = keep(hi) + pltpu.roll(keep(mid), N_HEADS, axis=1) + pltpu.roll(keep(lo), 2 * N_HEADS, axis=1)
    kc = _dot(packed.astype(BF16), selc_ref[...])
    k = k_ref[...]
    pad = jnp.zeros((tt, HEAD_PAD - HEAD_DIM), F32)
    for h in range(N_HEADS):
        hp = slice(h * HEAD_PAD, (h + 1) * HEAD_PAD)
        ka_ref[:, hp] = (jnp.concatenate([k[:, h * HEAD_DIM:(h + 1) * HEAD_DIM], pad], axis=1) + kc[:, hp]).astype(BF16)
    ii = lax.broadcasted_iota(I32, (HW, HW), 0)
    jj = lax.broadcasted_iota(I32, (HW, HW), 1)
    eye = jnp.where(ii == jj, 1.0, 0.0).astype(BF16)
    vt_ref[...] = lax.dot_general(eye, v_ref[...].astype(BF16), NT_DIMS, preferred_element_type=F32).astype(BF16)


def _fox_prep(logf, k, v, sel_c):
    B, T, _ = k.shape
    tt = _largest_tile(T, 384)
    nt = T // tt
    tile = lambda w: pl.BlockSpec((None, tt, w), lambda b, t: (b, t, 0))
    return pl.pallas_call(
        _fox_prep_kernel,
        grid=(B, nt),
        in_specs=[tile(LANES), tile(HW), tile(HW), _full(sel_c.shape)],
        out_specs=[tile(N_HEADS * HEAD_PAD), pl.BlockSpec((None, None, HW, tt), lambda b, t: (b, t, 0, 0))],
        out_shape=[jax.ShapeDtypeStruct((B, T, N_HEADS * HEAD_PAD), BF16),
                   jax.ShapeDtypeStruct((B, nt, HW, tt), BF16)],
        scratch_shapes=[pltpu.VMEM((1, LANES), F32)],
        compiler_params=_params("arbitrary", "arbitrary"),
        name="fox_prep",
    )(logf, k, v, sel_c)


def _fox_kernel(qi_ref, ki_ref, last_ref, q_ref, ka_ref, vt_ref, selq_ref, o_ref, qa_ref, m_ref, l_ref, acc_ref,
                *, q0, tq, tk):
    p = pl.program_id(1)
    qi = qi_ref[p]
    ki = ki_ref[p]

    @pl.when(ki == 0)
    def _():
        m_ref[...] = jnp.full_like(m_ref, NEG_BIG)
        l_ref[...] = jnp.zeros_like(l_ref)
        acc_ref[...] = jnp.zeros_like(acc_ref)
        lane = lax.broadcasted_iota(I32, qa_ref.shape, 1) % HEAD_PAD
        ones = jnp.where(jnp.logical_and(lane >= HEAD_DIM, lane < HEAD_DIM + C_SPLIT), 1.0, 0.0)
        qa_ref[...] = (_dot(q_ref[...], selq_ref[...]) + ones).astype(BF16)

    first_q = q0 + qi * tq
    tile_first = ki * tk
    tile_last = tile_first + tk - 1

    def scores(h):
        hp = slice(h * HEAD_PAD, (h + 1) * HEAD_PAD)
        return lax.dot_general(ka_ref[:, hp], qa_ref[:, hp], NT_DIMS, preferred_element_type=F32)

    def tile_update(masked):
        if masked:
            key_pos = tile_first + lax.broadcasted_iota(I32, (tk, tq), 0)
            qry_pos = first_q + lax.broadcasted_iota(I32, (tk, tq), 1)
            bias = jnp.where(qry_pos >= key_pos, 0.0, NEG_BIG)
        m_all = m_ref[...]
        l_all = l_ref[...]
        m_rows, l_rows = [], []
        s_next = scores(0)
        for h in range(N_HEADS):
            hs = slice(h * HEAD_DIM, (h + 1) * HEAD_DIM)
            s = s_next
            if h + 1 < N_HEADS:
                s_next = scores(h + 1)
            if masked:
                s = s + bias
            m_prev = m_all[h:h + 1, :]
            m_new = jnp.maximum(m_prev, jnp.max(s, 0, keepdims=True))
            alpha = jnp.exp(m_prev - m_new)
            p = jnp.exp(s - m_new)
            l_rows.append(alpha * l_all[h:h + 1, :] + jnp.sum(p, 0, keepdims=True))
            m_rows.append(m_new)
            acc_ref[hs, :] = alpha * acc_ref[hs, :] + _dot(vt_ref[hs, :], p.astype(BF16))
        m_ref[...] = jnp.concatenate(m_rows, axis=0)
        l_ref[...] = jnp.concatenate(l_rows, axis=0)

    @pl.when(jnp.logical_and(tile_first <= first_q + tq - 1, tile_last > first_q))
    def _():
        tile_update(True)

    @pl.when(tile_last <= first_q)
    def _():
        tile_update(False)

    @pl.when(last_ref[p] == 1)
    def _():
        on = jnp.concatenate(
            [acc_ref[h * HEAD_DIM:(h + 1) * HEAD_DIM, :] / l_ref[h:h + 1, :] for h in range(N_HEADS)], axis=0)
        eye = jnp.where(lax.broadcasted_iota(I32, (tq, tq), 0) == lax.broadcasted_iota(I32, (tq, tq), 1),
                        1.0, 0.0).astype(BF16)
        o_ref[...] = lax.dot_general(eye, on.astype(BF16), NT_DIMS, preferred_element_type=F32).astype(o_ref.dtype)


def _fox_attention(q, k_aug, v_t, sel_q, q0):
    B, Tq, _ = q.shape
    Tk = k_aug.shape[1]
    WA = N_HEADS * HEAD_PAD
    tq = _largest_tile(Tq, 384)
    nk, tk = v_t.shape[1], v_t.shape[3]
    assert nk * tk == Tk
    nq = Tq // tq
    pairs = [(qi, ki) for qi in range(nq) for ki in range(min((q0 + (qi + 1) * tq - 1) // tk, nk - 1) + 1)]
    qi_tab = jnp.array([p[0] for p in pairs], I32)
    ki_tab = jnp.array([p[1] for p in pairs], I32)
    last_tab = jnp.array([int(i + 1 == len(pairs) or pairs[i + 1][0] != pairs[i][0]) for i in range(len(pairs))], I32)
    qspec = lambda w: pl.BlockSpec((None, tq, w), lambda b, p, qt, kt, lt: (b, qt[p], 0))
    grid_spec = pltpu.PrefetchScalarGridSpec(
        num_scalar_prefetch=3,
        grid=(B, len(pairs)),
        in_specs=[qspec(HW),
                  pl.BlockSpec((None, tk, WA), lambda b, p, qt, kt, lt: (b, kt[p], 0)),
                  pl.BlockSpec((None, None, HW, tk), lambda b, p, qt, kt, lt: (b, kt[p], 0, 0)),
                  pl.BlockSpec(sel_q.shape, lambda b, p, qt, kt, lt: (0, 0))],
        out_specs=qspec(HW),
        scratch_shapes=[pltpu.VMEM((tq, WA), BF16), pltpu.VMEM((N_HEADS, tq), F32),
                        pltpu.VMEM((N_HEADS, tq), F32), pltpu.VMEM((HW, tq), F32)],
    )
    return pl.pallas_call(
        functools.partial(_fox_kernel, q0=q0, tq=tq, tk=tk),
        grid_spec=grid_spec,
        out_shape=jax.ShapeDtypeStruct((B, Tq, HW), BF16),
        compiler_params=_params("arbitrary", "arbitrary"),
        name="fox_attention",
    )(qi_tab, ki_tab, last_tab, q, k_aug, v_t, sel_q)


def _rwkv_kernel(rkv_ref, lmid_ref, rkv0_ref, s0_ref, mu_ref, w0_ref, w2_ref, a0_ref, a2_ref, g2_ref,
                 kk_ref, ka_ref, rk_ref, gng_ref, gnb_ref, o_ref, sfin_ref, state_ref, carry_ref, *, chunk, levels):
    C = chunk
    NB, T = rkv_ref.shape[0], rkv_ref.shape[1]
    state_ref[...] = s0_ref[...]
    carry_ref[...] = rkv0_ref[...]
    row_w = lax.broadcasted_iota(I32, (C, 3 * HW), 0)
    row_h = lax.broadcasted_iota(I32, (C, HW), 0)
    r_i = lax.broadcasted_iota(I32, (C, C), 0)
    c_i = lax.broadcasted_iota(I32, (C, C), 1)
    strict = r_i > c_i
    incl = r_i >= c_i
    mid = C // 2 - 1 if C > 1 else 0
    hsl = [slice(h * HEAD_DIM, (h + 1) * HEAD_DIM) for h in range(N_HEADS)]
    nt = lambda x, y: lax.dot_general(x, y, NT_DIMS, preferred_element_type=F32)
    tn = lambda x, y: lax.dot_general(x, y, TN_DIMS, preferred_element_type=F32)

    def row_inputs(bb, off):
        x = rkv_ref[bb, pl.ds(off, C), :].astype(F32)
        prev = jnp.where(row_w == 0, carry_ref[bb], pltpu.roll(x, 1, axis=0))
        carry_ref[bb] = x[C - 1:C, :]
        x = x + (prev - x) * mu_ref[...]
        r, k0, v = x[:, 0:HW], x[:, HW:2 * HW], x[:, 2 * HW:3 * HW]
        lm = lmid_ref[bb, pl.ds(off, C), :]
        w_pre = w0_ref[...] + _dot(jnp.tanh(lm[:, 0:64]).astype(BF16), w2_ref[...])
        a = _sigmoid(a0_ref[...] + _dot(lm[:, 64:128].astype(BF16), a2_ref[...]))
        g = _dot(_sigmoid(lm[:, 128:256]).astype(BF16), g2_ref[...])
        w_log = -_softplus(-w_pre) - 0.5
        logdec = -jnp.exp(w_log)
        L = logdec
        sh = 1
        while sh < C:
            L = L + jnp.where(row_h >= sh, pltpu.roll(L, sh, axis=0), 0.0)
            sh *= 2
        l_mid = L[mid:mid + 1, :]
        l_tot = L[C - 1:C, :]
        return dict(r=r, v=v, a=a, g=g, kk_raw=k0 * kk_ref[...], k=k0 * (1.0 + (a - 1.0) * ka_ref[...]),
                    e_a=jnp.exp(L - logdec - l_mid), e_r=jnp.exp(L - l_mid), e_k=jnp.exp(l_mid - L),
                    e_s=jnp.exp(l_tot - L), w_tot=jnp.exp(l_tot), e_mid=jnp.exp(l_mid))

    gap = jnp.zeros((C, LANES - C), F32)
    gap2 = jnp.zeros((LANES - C, HEAD_DIM), F32)
    lane_pair = lambda left, right: jnp.concatenate([left, gap, right], axis=1)

    def chunk_body(i, carry):
        off = pl.multiple_of(i * C, C)
        rows = [row_inputs(bb, off) for bb in range(NB)]
        rk = rk_ref[...]
        chains = [(bb, h) for bb in range(NB) for h in range(N_HEADS)]
        X = range(len(chains))
        col = lambda name: [rows[bb][name][:, hsl[h]] for bb, h in chains]
        kkh = [x * lax.rsqrt(jnp.maximum(jnp.sum(x * x, -1, keepdims=True), 1e-24)) for x in col('kk_raw')]
        r_h, k_h, v_h, a_h = col('r'), col('k'), col('v'), col('a')
        e_a, e_r, e_k, e_s, w_tot, e_mid = col('e_a'), col('e_r'), col('e_k'), col('e_s'), col('w_tot'), col('e_mid')
        b_h = [kkh[c] * a_h[c] for c in X]
        ar = [jnp.concatenate([-kkh[c] * e_a[c], r_h[c] * e_r[c]], axis=0) for c in X]
        bkd = [jnp.concatenate([b_h[c] * e_k[c], gap2, k_h[c] * e_k[c]], axis=0) for c in X]
        s_old = [state_ref[bb, h] for bb, h in chains]
        gram = [nt(ar[c], bkd[c]) for c in X]
        x0 = [nt(ar[c], s_old[c] * e_mid[c]) for c in X]
        g_b = [gram[c][:, 0:C] for c in X]
        g_k = [gram[c][:, LANES:LANES + C] for c in X]
        a_mat = [jnp.where(strict, g_b[c][0:C], 0.0) for c in X]
        kv = [_dot(jnp.concatenate([jnp.where(strict, g_k[c][0:C], 0.0), jnp.where(incl, g_k[c][C:2 * C], 0.0)],
                                   axis=0), v_h[c]) for c in X]
        u = [x0[c][0:C] + kv[c][0:C] for c in X]
        for lvl in range(levels):
            if lvl + 1 < levels:
                prod = [_dot(a_mat[c], lane_pair(a_mat[c], u[c])) for c in X]
                a_mat = [prod[c][:, 0:C] for c in X]
                u = [u[c] + prod[c][:, LANES:LANES + HEAD_DIM] for c in X]
            else:
                u = [u[c] + _dot(a_mat[c], u[c]) for c in X]
        y = [x0[c][C:2 * C] + _dot(jnp.where(incl, g_b[c][C:2 * C], 0.0), u[c]) + kv[c][C:2 * C] for c in X]
        for c, (bb, h) in enumerate(chains):
            uv = jnp.concatenate([u[c], v_h[c]], axis=0)
            bks = jnp.concatenate([b_h[c] * e_s[c], k_h[c] * e_s[c]], axis=0)
            state_ref[bb, h] = s_old[c] * w_tot[c] + tn(uv, bks)
        outs = []
        for c, (bb, h) in enumerate(chains):
            mu = jnp.mean(y[c], -1, keepdims=True)
            yc = y[c] - mu
            var = jnp.mean(yc * yc, -1, keepdims=True)
            bonus = jnp.sum(r_h[c] * k_h[c] * rk[:, hsl[h]], -1, keepdims=True) * v_h[c]
            outs.append((yc * lax.rsqrt(var + GN_EPS), bonus))
        for bb in range(NB):
            mine = outs[bb * N_HEADS:(bb + 1) * N_HEADS]
            yn = jnp.concatenate([o[0] for o in mine], axis=1)
            bonus = jnp.concatenate([o[1] for o in mine], axis=1)
            out = (yn * gng_ref[...] + gnb_ref[...] + bonus) * rows[bb]['g']
            o_ref[bb, pl.ds(off, C), :] = out.astype(o_ref.dtype)
        return carry

    lax.fori_loop(0, T // C, chunk_body, 0)
    sfin_ref[...] = state_ref[...]


RWKV_ROWS_PER_STEP = 2


def _rwkv(rkv, lmid, rkv0, s0, mu_rkv, w0, w2, a0, a2, g2, k_k, k_a, r_k, gn_g, gn_b):
    B, T, _ = rkv.shape
    nb = RWKV_ROWS_PER_STEP if B % RWKV_ROWS_PER_STEP == 0 else 1
    chunk = _largest_tile(T, 64, mult=16)
    levels = max(1, math.ceil(math.log2(chunk)))
    seq =lambda w: pl.BlockSpec((nb, T, w), lambda b: (b, 0, 0))
    st = pl.BlockSpec((nb, N_HEADS, HEAD_DIM, HEAD_DIM), lambda b: (b, 0, 0, 0))
    vec = lambda a: _full(a.shape)
    params = (mu_rkv, w0, w2, a0, a2, g2, k_k, k_a, r_k, gn_g, gn_b)
    return pl.pallas_call(
        functools.partial(_rwkv_kernel, chunk=chunk, levels=levels),
        grid=(B // nb,),
        in_specs=[seq(3 * HW), seq(256), pl.BlockSpec((nb, 1, 3 * HW), lambda b: (b, 0, 0)), st]
                 + [vec(p) for p in params],
        out_specs=[seq(HW), st],
        out_shape=[jax.ShapeDtypeStruct((B, T, HW), BF16),
                   jax.ShapeDtypeStruct((B, N_HEADS, HEAD_DIM, HEAD_DIM), F32)],
        scratch_shapes=[pltpu.VMEM((nb, N_HEADS, HEAD_DIM, HEAD_DIM), F32), pltpu.VMEM((nb, 1, 3 * HW), F32)],
        compiler_params=_params("arbitrary"),
        name="rwkv7",
    )(rkv, lmid, rkv0, s0, *params)


def _merge_kernel(fox_ref, rw_ref, gate_ref, h_ref, wa_ref, wb_ref, wo_ref, g_ref, b_ref, wrh_ref, wrl_ref, br_ref,
                  cnt0_ref, h1_ref, h1p_ref, idx_ref, gt_ref, rank_ref, cnt_ref, carry_ref, *, dn_alpha):
    i = pl.program_id(0)
    tm, D = h_ref.shape

    @pl.when(i == 0)
    def _():
        carry_ref[...] = cnt0_ref[...].astype(F32)

    gates = gate_ref[...].astype(F32)
    merged = gates[:, 0:D] * _dot(fox_ref[...], wa_ref[...]) + gates[:, D:2 * D] * _dot(rw_ref[...], wb_ref[...])
    z = dn_alpha * h_ref[...] + _dot(merged.astype(BF16), wo_ref[...])
    h1 = _layer_norm(z, g_ref[...], b_ref[...])
    h1_ref[...] = h1
    h1p_ref[...] = _pack_bf16_pair(h1)
    hi = h1.astype(BF16)
    lo = (h1 - hi.astype(F32)).astype(BF16)
    logits = _dot(hi, wrh_ref[...]) + _dot(hi, wrl_ref[...]) + _dot(lo, wrh_ref[...]) + br_ref[...]
    lane = lax.broadcasted_iota(I32, (tm, LANES), 1)
    lane_f = lane.astype(F32)
    cur = logits
    vals, idxs = [], []
    for _ in range(TOP_K):
        m = jnp.max(cur, -1, keepdims=True)
        ix = jnp.min(jnp.where(cur == m, lane_f, float(LANES)), -1, keepdims=True)
        vals.append(m)
        idxs.append(ix)
        cur = jnp.where(lane_f == ix, -3e38, cur)
    exps = [jnp.exp(vk - vals[0]) for vk in vals]
    denom = exps[0] + exps[1] + exps[2] + exps[3]
    onehot = jnp.zeros((tm, LANES), F32)
    for ix in idxs:
        onehot = onehot + jnp.where(lane_f == ix, 1.0, 0.0)
    r_i = lax.broadcasted_iota(I32, (tm, tm), 0)
    c_i = lax.broadcasted_iota(I32, (tm, tm), 1)
    tri = jnp.where(r_i > c_i, 1.0, 0.0).astype(BF16)
    before = _dot(tri, onehot.astype(BF16)) + carry_ref[...]
    idx_out = jnp.zeros((tm, LANES), F32)
    gt_out = jnp.zeros((tm, LANES), F32)
    rank_out = jnp.zeros((tm, LANES), F32)
    for kx in range(TOP_K):
        rank_k = jnp.sum(jnp.where(lane_f == idxs[kx], before, 0.0), -1, keepdims=True)
        idx_out = jnp.where(lane == kx, idxs[kx], idx_out)
        gt_out = jnp.where(lane == kx, exps[kx] / denom, gt_out)
        rank_out = jnp.where(lane == kx, rank_k, rank_out)
    idx_ref[...] = idx_out.astype(I32)
    gt_ref[...] = gt_out
    rank_ref[...] = rank_out.astype(I32)
    total = carry_ref[...] + jnp.sum(onehot, 0, keepdims=True)
    carry_ref[...] = total
    cnt_ref[...] = total.astype(I32)


def _merge_route(fox, rw, gates, h, wa, wb, wo, ln_g, ln_b, wr_hi, wr_lo, br, counts_before, dn_alpha):
    N, D = h.shape
    tm = _largest_tile(N, 384)
    tile = lambda w: pl.BlockSpec((tm, w), lambda i: (i, 0))
    return pl.pallas_call(
        functools.partial(_merge_kernel, dn_alpha=dn_alpha),
        grid=(N // tm,),
        in_specs=[tile(HW), tile(HW), tile(2 * D), tile(D), _full(wa.shape), _full(wb.shape), _full(wo.shape),
                  _full((1, D)), _full((1, D)), _full(wr_hi.shape), _full(wr_lo.shape), _full(br.shape),
                  _full((1, LANES))],
        out_specs=[tile(D), tile(D // 2), tile(LANES), tile(LANES), tile(LANES), _full((1, LANES))],
        out_shape=[jax.ShapeDtypeStruct((N, D), F32), jax.ShapeDtypeStruct((N, D // 2), U32),
                   jax.ShapeDtypeStruct((N, LANES), I32),
                   jax.ShapeDtypeStruct((N, LANES), F32), jax.ShapeDtypeStruct((N, LANES), I32),
                   jax.ShapeDtypeStruct((1, LANES), I32)],
        scratch_shapes=[pltpu.VMEM((1, LANES), F32)],
        compiler_params=_params("arbitrary"),
        name="merge_route",
    )(fox, rw, gates, h, wa, wb, wo, ln_g, ln_b, wr_hi, wr_lo, br, counts_before)


PERM_W = 256


def _deinterleave_to_bf16(w_ref, g_ref, l_ref):
    half = PERM_W // 2
    ii = lax.broadcasted_iota(I32, (PERM_W, PERM_W), 0)
    jj = lax.broadcasted_iota(I32, (PERM_W, PERM_W), 1)
    src = jnp.where(jj < half, 2 * jj, 2 * (jj - half) + 1)
    perm = jnp.where(ii == src, 1.0, 0.0).astype(BF16)
    for c in range(w_ref.shape[1] // PERM_W):
        w = w_ref[:, c * PERM_W:(c + 1) * PERM_W].astype(BF16)
        out = _dot(w, perm)
        g_ref[:, c * half:(c + 1) * half] = out[:, :half].astype(BF16)
        l_ref[:, c * half:(c + 1) * half] = out[:, half:].astype(BF16)


def _dispatch_kernel(pad_ref, dest_ref, x_ref, *rest, n_pad):
    xs_hbm, zero_ref, sem = rest[-3:]
    i = pl.program_id(0)
    tm = x_ref.shape[0]

    if n_pad:
        @pl.when(i == 0)
        def _():
            zero_ref[...] = jnp.zeros_like(zero_ref)

            def zbody(r, c):
                pltpu.make_async_copy(zero_ref.at[pl.ds(0, 1)], xs_hbm.at[pl.ds(pad_ref[r], 1)], sem.at[1]).start()
                return c
            lax.fori_loop(0, n_pad, zbody, 0, unroll=8)
            for _ in range(n_pad // tm):
                pltpu.make_async_copy(zero_ref, xs_hbm.at[pl.ds(0, tm)], sem.at[1]).wait()
            if n_pad % tm:
                pltpu.make_async_copy(zero_ref.at[pl.ds(0, n_pad % tm)], xs_hbm.at[pl.ds(0, n_pad % tm)],
                                      sem.at[1]).wait()

    def body(r, c):
        for kx in range(TOP_K):
            d = dest_ref[0, 0, r * TOP_K + kx]
            pltpu.make_async_copy(x_ref.at[pl.ds(r, 1)], xs_hbm.at[pl.ds(d, 1)], sem.at[0]).start()
        return c
    lax.fori_loop(0, tm, body, 0, unroll=8)
    for _ in range(TOP_K):
        pltpu.make_async_copy(x_ref, xs_hbm.at[pl.ds(0, tm)], sem.at[0]).wait()


def _moe_dispatch(xp, dest, pad_slots, rows, extend=None):
    N, W = xp.shape
    tm = _largest_tile(N, 256)
    n = N // tm
    n_pad = 0 if extend is not None else pad_slots.shape[0]
    in_specs = [pl.BlockSpec((1, 1, tm * TOP_K), lambda i, pad: (i, 0, 0), memory_space=pltpu.SMEM),
                pl.BlockSpec((tm, W), lambda i, pad: (i, 0))]
    operands = [pad_slots, dest.reshape(n, 1, tm * TOP_K), xp]
    aliases = {}
    if extend is not None:
        in_specs.append(pl.BlockSpec(memory_space=pl.ANY))
        operands.append(extend)
        aliases = {len(operands) - 1: 0}
    grid_spec = pltpu.PrefetchScalarGridSpec(
        num_scalar_prefetch=1,
        grid=(n,),
        in_specs=in_specs,
        out_specs=pl.BlockSpec(memory_space=pl.ANY),
        scratch_shapes=[pltpu.VMEM((tm, W), U32), pltpu.SemaphoreType.DMA((2,))],
    )
    return pl.pallas_call(
        functools.partial(_dispatch_kernel, n_pad=n_pad),
        grid_spec=grid_spec,
        out_shape=jax.ShapeDtypeStruct((rows, W), U32),
        input_output_aliases=aliases,
        compiler_params=_params("arbitrary"),
        name="moe_dispatch",
    )(*operands)


def _moe_kernel(be_ref, nused_ref, xs_ref, w1_ref, b1g_ref, b1l_ref, w2_ref, b2_ref, y_ref, w1g_s, w1l_s, w2_s):
    j = pl.program_id(0)
    nused = nused_ref[0]
    last = jnp.maximum(nused - 1, 0)
    e_now = be_ref[jnp.minimum(j, last)]
    e_before = be_ref[jnp.minimum(jnp.maximum(j - 1, 0), last)]

    @pl.when(jnp.logical_or(j == 0, e_now != e_before))
    def _():
        _deinterleave_to_bf16(w1_ref, w1g_s, w1l_s)
        w2_s[...] = w2_ref[...].astype(BF16)

    @pl.when(j < nused)
    def _():
        lo, hi = _unpack_bf16_pair(xs_ref[...])
        x = jnp.concatenate([lo, hi], axis=1).astype(BF16)
        glu = jnp.minimum(_dot(x, w1g_s[...]) + b1g_ref[...], SWIGLU_LIMIT)
        lin = jnp.clip(_dot(x, w1l_s[...]) + b1l_ref[...], -SWIGLU_LIMIT, SWIGLU_LIMIT)
        act = glu * _sigmoid(SWIGLU_ALPHA * glu) * (lin + 1.0)
        y_ref[...] = _pack_bf16_pair(_dot(act.astype(BF16), w2_s[...]) + b2_ref[...])

    @pl.when(j >= nused)
    def _():
        y_ref[...] = jnp.zeros_like(y_ref)


def _moe_experts(xs, blk_e, nused, w1, b1g, b1l, w2, b2, bm):
    rows, W = xs.shape
    nb = rows // bm
    D, F = w1.shape[1], w1.shape[2] // 2
    last = lambda j, be, nu: jnp.minimum(j, jnp.maximum(nu[0] - 1, 0))
    wspec = lambda k, n: pl.BlockSpec((None, k, n), lambda j, be, nu: (be[last(j, be, nu)], 0, 0))
    grid_spec = pltpu.PrefetchScalarGridSpec(
        num_scalar_prefetch=2,
        grid=(nb,),
        in_specs=[pl.BlockSpec((bm, W), lambda j, be, nu: (j, 0)),
                  wspec(D, 2 * F), wspec(1, F), wspec(1, F), wspec(F, D), wspec(1, D)],
        out_specs=pl.BlockSpec((bm, W), lambda j, be, nu: (j, 0)),
        scratch_shapes=[pltpu.VMEM((D, F), BF16), pltpu.VMEM((D, F), BF16), pltpu.VMEM((F, D), BF16)],
    )
    return pl.pallas_call(
        _moe_kernel,
        grid_spec=grid_spec,
        out_shape=jax.ShapeDtypeStruct((rows, W), U32),
        compiler_params=_params("arbitrary"),
        name="moe_experts",
    )(blk_e, nused, xs, w1, b1g, b1l, w2, b2)


def _combine_gather_start(dest_ref, y_hbm, buf, sem, slot, tm):
    def body(r, c):
        for kx in range(TOP_K):
            d = dest_ref[0, 0, r * TOP_K + kx]
            pltpu.make_async_copy(y_hbm.at[pl.ds(d, 1)], buf.at[slot, kx, pl.ds(r, 1)], sem.at[slot]).start()
        return c
    lax.fori_loop(0, tm, body, 0, unroll=8)


def _combine_kernel(dest_ref, destn_ref, gt_ref, h1_ref, g_ref, b_ref, y_hbm, o_hbm, buf, sem, obuf, osem,
                    *, dn_alpha, nt, skip):
    i = pl.program_id(0)
    n = pl.num_programs(0)
    tm = h1_ref.shape[0]
    slot = i % 2
    b = i // nt
    j = i % nt

    def out_wait(rows):
        pltpu.make_async_copy(obuf.at[0, pl.ds(0, rows)], o_hbm.at[0, pl.ds(0, rows)], osem.at[0]).wait()

    @pl.when(i == 0)
    def _():
        _combine_gather_start(dest_ref, y_hbm, buf, sem, 0, tm)

    @pl.when(i + 1 < n)
    def _():
        _combine_gather_start(destn_ref, y_hbm, buf, sem, 1 - slot, tm)

    for kx in range(TOP_K):
        pltpu.make_async_copy(y_hbm.at[pl.ds(0, tm)], buf.at[slot, kx], sem.at[slot]).wait()
    gt = gt_ref[...]
    lo, hi = _unpack_bf16_pair(buf[slot, 0])
    moe_lo, moe_hi = gt[:, 0:1] * lo, gt[:, 0:1] * hi
    for kx in range(1, TOP_K):
        lo, hi = _unpack_bf16_pair(buf[slot, kx])
        moe_lo = moe_lo + gt[:, kx:kx + 1] * lo
        moe_hi = moe_hi + gt[:, kx:kx + 1] * hi
    moe = jnp.concatenate([moe_lo, moe_hi], axis=1)
    obuf[slot] = _layer_norm(dn_alpha * h1_ref[...] + moe, g_ref[...], b_ref[...])

    @pl.when(jnp.logical_and(i > 0, (i - 1) % nt == 0))
    def _():
        out_wait(tm - skip)

    @pl.when(jnp.logical_and(i > 0, (i - 1) % nt != 0))
    def _():
        out_wait(tm)

    @pl.when(j == 0)
    def _():
        pltpu.make_async_copy(obuf.at[slot, pl.ds(skip, tm - skip)], o_hbm.at[b, pl.ds(0, tm - skip)],
                              osem.at[0]).start()

    @pl.when(j != 0)
    def _():
        start = pl.multiple_of(j * tm - skip, 8)
        pltpu.make_async_copy(obuf.at[slot], o_hbm.at[b, pl.ds(start, tm)], osem.at[0]).start()

    @pl.when(i == n - 1)
    def _():
        if nt == 1:
            out_wait(tm - skip)
        else:
            out_wait(tm)


def _moe_combine(dest, gate, h1, ln_g, ln_b, yb, dn_alpha, B, T, skip):
    N, D = h1.shape
    tm = _largest_tile(T, 384)
    nt = T // tm
    n = N // tm
    assert skip % 8 == 0 and skip < tm
    dest3 = dest.reshape(n, 1, tm * TOP_K)
    tile = lambda w: pl.BlockSpec((tm, w), lambda i: (i, 0))
    return pl.pallas_call(
        functools.partial(_combine_kernel, dn_alpha=dn_alpha, nt=nt, skip=skip),
        grid=(n,),
        in_specs=[
            pl.BlockSpec((1, 1, tm * TOP_K), lambda i: (i, 0, 0), memory_space=pltpu.SMEM),
            pl.BlockSpec((1, 1, tm * TOP_K), lambda i: (jnp.minimum(i + 1, n - 1), 0, 0), memory_space=pltpu.SMEM),
            tile(LANES), tile(D), _full((1, D)), _full((1, D)),
            pl.BlockSpec(memory_space=pl.ANY),
        ],
        out_specs=pl.BlockSpec(memory_space=pl.ANY),
        out_shape=jax.ShapeDtypeStruct((B, T - skip, D), F32),
        scratch_shapes=[pltpu.VMEM((2, TOP_K, tm, yb.shape[1]), U32), pltpu.SemaphoreType.DMA((2,)),
                        pltpu.VMEM((2, tm, D), F32), pltpu.SemaphoreType.DMA((1,))],
        compiler_params=_params("arbitrary"),
        name="moe_combine",
    )(dest3, dest3, gate, h1, ln_g, ln_b, yb)


MOE_BLOCK_ROWS = 512


def _route_tables(n_asg, counts, n_experts):
    bm = min(MOE_BLOCK_ROWS, max(8, 1 << int(math.log2(max(1, n_asg // n_experts)))))
    nb = -(-n_asg // bm) + n_experts
    padded = (counts + bm - 1) // bm * bm
    pends = jnp.cumsum(padded)
    starts = (pends - padded).astype(I32)
    blk_start = jnp.arange(nb, dtype=I32) * bm
    blk_e = jnp.minimum(jnp.sum(pends[None, :] <= blk_start[:, None], axis=1), n_experts - 1).astype(I32)
    nused = (pends[-1] // bm).astype(I32).reshape(1)
    n_pad = nb * bm - n_asg
    gap = padded - counts
    gap_end = jnp.cumsum(gap)
    i = jnp.arange(n_pad, dtype=I32)
    e = jnp.sum(gap_end[None, :] <= i[:, None], axis=1)
    ec = jnp.minimum(e, n_experts - 1)
    in_group = (pends - padded + counts)[ec] + i - (gap_end - gap)[ec]
    pad_slots = jnp.where(e < n_experts, in_group, pends[-1] + i - gap_end[-1]).astype(I32)
    return starts, pad_slots, blk_e, nused, bm, nb * bm


def _mixers(x, prev_row, s0, past_k, past_v, past_logf, wts):
    B, T, D = x.shape
    h, q, k, v, rkv, gates, lmid, logf, rkv0 = _inproj(
        x, prev_row, wts['ln0_g'], wts['ln0_b'], wts['wm'], wts['wff'], wts['bff'], wts['mu3'],
        wts['w1'], wts['a1'], wts['g1'])
    if past_k is None:
        k_aug, v_bf = _fox_prep(logf, k, v, wts['sel_c'])
        fox = _fox_attention(q, k_aug, v_bf, wts['sel_q'], 0)
    else:
        P = past_k.shape[1]
        past_pad = jnp.pad(past_logf.astype(F32), ((0, 0), (0, 0), (0, LANES - N_HEADS)))
        k_aug, v_bf = _fox_prep(jnp.concatenate([past_pad, logf], axis=1),
                                jnp.concatenate([past_k.reshape(B, P, HW), k], axis=1),
                                jnp.concatenate([past_v.reshape(B, P, HW), v], axis=1), wts['sel_c'])
        fox = _fox_attention(q, k_aug, v_bf, wts['sel_q'], P)
    rw, s_fin = _rwkv(rkv, lmid, rkv0, s0, wts['mu_rkv'], wts['w0'], wts['w2'], wts['a0'], wts['a2'], wts['g2'],
                      wts['k_k'], wts['k_a'], wts['r_k'], wts['gn_g'], wts['gn_b'])
    N = B * T
    tokens = dict(fox=fox.reshape(N, HW), rw=rw.reshape(N, HW), gates=gates.reshape(N, 2 * D), h=h.reshape(N, D))
    new_k = k.reshape(B, T, N_HEADS, HEAD_DIM)
    new_v = v.reshape(B, T, N_HEADS, HEAD_DIM)
    return tokens, (new_k, new_v, logf[:, :, :N_HEADS], s_fin, h[:, T - 1:T, :])


def _merge_moe(streams, wts):
    dn_alpha, n_experts = wts['dn_alpha'], wts['n_experts']
    counts = jnp.zeros((1, LANES), I32)
    routed = []
    for tokens, B, T, y_skip in streams:
        h1, h1p, top_idx, gate, rank, counts = _merge_route(
            tokens['fox'], tokens['rw'], tokens['gates'], tokens['h'],
            wts['w_up_a'], wts['w_up_b'], wts['w_out'], wts['ln1_g'], wts['ln1_b'],
            wts['wr_hi'], wts['wr_lo'], wts['br'], counts, dn_alpha)
        routed.append((h1, h1p, top_idx[:, :TOP_K], gate, rank[:, :TOP_K]))
    n_asg = sum(r[0].shape[0] for r in routed) * TOP_K
    starts, pad_slots, blk_e, nused, bm, rows = _route_tables(n_asg, counts[0, :n_experts], n_experts)
    dests = [(starts[top_idx] + rank).astype(I32) for _, _, top_idx, _, rank in routed]
    zero_slots = jnp.concatenate([pad_slots] + [d.reshape(-1) for d in dests[1:]])
    xs = None
    for (h1, h1p, _, _, _), dest in zip(routed, dests):
        xs = _moe_dispatch(h1p, dest, zero_slots, rows, extend=xs)
    yb = _moe_experts(xs, blk_e, nused, wts['we1'], wts['b1g'], wts['b1l'], wts['we2'], wts['be2'], bm)
    return [_moe_combine(dest, gate, h1, wts['ln2_g'], wts['ln2_b'], yb, dn_alpha, B, T, y_skip)
            for (h1, _, _, gate, _), dest, (_, B, T, y_skip) in zip(routed, dests, streams)]


def kernel(x_prompt, x_sample, cache_fox_k, cache_fox_v, cache_fox_logf, state_rwkv, state_shift, meta, ln0_g, ln0_b, w_in, b_forget, mu_w, mu_a, mu_g, mu_rkv, w0, w1, w2, a0, a1, a2, g1, g2, k_k, k_a, r_k, gn_g, gn_b, w_up_a, w_up_b, w_out, ln1_g, ln1_b, w_router, b_router, w_e1, b_e1, w_e2, b_e2, ln2_g, ln2_b):
    depth, D, in_cols = w_in.shape
    assert depth == 1 and D == 1024 and in_cols == 6 * HW + N_HEADS + 2 * D
    n_experts = w_router.shape[2]
    assert n_experts <= LANES
    B = x_prompt.shape[0]
    l = 0
    w = w_in[l]
    off_ff = 3 * HW
    row = lambda a: a.reshape(1, -1).astype(F32)
    wr = jnp.pad(w_router[l], ((0, 0), (0, LANES - n_experts)))
    wr_hi = wr.astype(BF16)
    sel_q, sel_c = _aug_select_matrices()
    wts = dict(
        sel_q=sel_q, sel_c=sel_c,
        dn_alpha=float((2 * depth) ** 0.25), n_experts=n_experts,
        ln0_g=row(ln0_g), ln0_b=row(ln0_b),
        wm=jnp.concatenate([w[:, :off_ff], w[:, off_ff + N_HEADS:]], axis=1).astype(BF16),
        wff=jnp.pad(w[:, off_ff:off_ff + N_HEADS], ((0, 0), (0, LANES - N_HEADS))).astype(BF16),
        bff=jnp.pad(row(b_forget[l]), ((0, 0), (0, LANES - N_HEADS))),
        mu3=jnp.stack([mu_w[l], mu_a[l], mu_g[l]], axis=0),
        w1=w1[l].astype(BF16), a1=a1[l].astype(BF16), g1=g1[l].astype(BF16),
        mu_rkv=row(mu_rkv[l]), w0=row(w0[l]), w2=w2[l].astype(BF16), a0=row(a0[l]), a2=a2[l].astype(BF16),
        g2=g2[l].astype(BF16), k_k=row(k_k[l]), k_a=row(k_a[l]), r_k=row(r_k[l]), gn_g=row(gn_g[l]),
        gn_b=row(gn_b[l]),
        w_up_a=w_up_a[l].astype(BF16), w_up_b=w_up_b[l].astype(BF16), w_out=w_out[l].astype(BF16),
        ln1_g=row(ln1_g[l]), ln1_b=row(ln1_b[l]),
        wr_hi=wr_hi, wr_lo=(wr - wr_hi.astype(F32)).astype(BF16),
        br=jnp.pad(row(b_router[l]), ((0, 0), (0, LANES - n_experts)), constant_values=NEG_BIG),
        we1=w_e1[l],
        b1g=b_e1[l][:, None, 0::2], b1l=b_e1[l][:, None, 1::2],
        we2=w_e2[l], be2=b_e2[l][:, None, :],
        ln2_g=row(ln2_g[l]), ln2_b=row(ln2_b[l]),
    )
    meta_b = jnp.broadcast_to(meta, (B, N_META, D)).astype(x_prompt.dtype)
    xp = jnp.concatenate([meta_b, x_prompt], axis=1)
    zero_row = jnp.zeros((B, 1, D), F32)
    zero_state = jnp.zeros((B, N_HEADS, HEAD_DIM, HEAD_DIM), F32)
    tok_p, (k_p, v_p, lf_p, s_p, sh_p) = _mixers(xp, zero_row, zero_state, None, None, None, wts)
    tok_s, (k_s, v_s, lf_s, s_s, sh_s) = _mixers(x_sample, state_shift[l], state_rwkv[l], cache_fox_k[l],
                                                 cache_fox_v[l], cache_fox_logf[l], wts)
    y_p, y_s = _merge_moe([(tok_p, B, xp.shape[1], N_META), (tok_s,) + x_sample.shape[:2] + (0,)], wts)
    ex = lambda a: a[None]
    return (y_p, y_s, ex(k_p), ex(v_p), ex(lf_p), ex(s_p), ex(sh_p),
            ex(k_s), ex(v_s), ex(lf_s), ex(s_s), ex(sh_s))
```

```python
import functools
import math

import jax
import jax.numpy as jnp
import numpy as np
from jax import lax
from jax.experimental import pallas as pl
from jax.experimental.pallas import tpu as pltpu

F32 = jnp.float32
BF16 = jnp.bfloat16
I32 = jnp.int32
U32 = jnp.uint32

N_META = 16
HEAD_DIM = 64
N_HEADS = 8
HW = N_HEADS * HEAD_DIM
TOP_K = 4
SWIGLU_LIMIT = 7.0
SWIGLU_ALPHA = 1.702
LN_EPS = 1e-5
GN_EPS = 64e-5
LANES = 128
NEG_BIG = -1e30
VMEM_LIMIT_BYTES = 56 * 1024 * 1024
HIGHEST = lax.Precision.HIGHEST

NT_DIMS = (((1,), (1,)), ((), ()))
TN_DIMS = (((0,), (0,)), ((), ()))


def _params(*sem):
    return pltpu.CompilerParams(dimension_semantics=sem, vmem_limit_bytes=VMEM_LIMIT_BYTES)


def _largest_tile(n, cap, mult=8):
    best = None
    for d in range(mult, min(n, cap) + 1, mult):
        if n % d == 0:
            best = d
    assert best is not None, (n, cap, mult)
    return best


def _sigmoid(x):
    return 1.0 / (1.0 + jnp.exp(-x))


def _softplus(x):
    return jnp.maximum(x, 0.0) + jnp.log1p(jnp.exp(-jnp.abs(x)))


def _layer_norm(x, g, b):
    mu = jnp.mean(x, -1, keepdims=True)
    xc = x - mu
    var = jnp.mean(xc * xc, -1, keepdims=True)
    return xc * lax.rsqrt(var + LN_EPS) * g + b


def _dot(a, b):
    return jnp.dot(a, b, preferred_element_type=F32)


def _full(shape):
    n = len(shape)
    return pl.BlockSpec(shape, lambda *_: (0,) * n)


def _pack_bf16_pair(x):
    w = x.shape[1] // 2
    bits = lambda t: lax.bitcast_convert_type(t.astype(BF16).astype(F32), U32)
    return (bits(x[:, :w]) >> 16) | (bits(x[:, w:]) & jnp.uint32(0xFFFF0000))


def _unpack_bf16_pair(u):
    return lax.bitcast_convert_type(u << 16, F32), lax.bitcast_convert_type(u & jnp.uint32(0xFFFF0000), F32)


C_Q, C_K, C_V, C_RKV, C_GA, C_END = 0, HW, 2 * HW, 3 * HW, 6 * HW, 6 * HW + 2048


def _inproj_kernel(x_ref, pre_ref, prev_ref, g_ref, b_ref, wm_ref, wff_ref, bff_ref, mu_ref, w1_ref, a1_ref, g1_ref,
                   h_ref, q_ref, k_ref, v_ref, rkv_ref, gate_ref, lmid_ref, logf_ref, rkv0_ref,
                   carry_ref, *, n_pre):
    t = pl.program_id(1)
    x = x_ref[0] if n_pre else x_ref[...]
    tt = x.shape[0]
    if n_pre:
        x = jnp.where(t == 0, jnp.concatenate([pre_ref[...], x[0:tt - n_pre]], axis=0), x)
    h = _layer_norm(x, g_ref[...], b_ref[...])
    h_ref[...] = h

    @pl.when(t == 0)
    def _():
        prev = prev_ref[...]
        carry_ref[...] = prev
        p8 = jnp.broadcast_to(prev, (8, prev.shape[1])).astype(BF16)
        rkv0_ref[...] = _dot(p8, wm_ref[:, C_RKV:C_GA])[0:1]

    rows = lax.broadcasted_iota(I32, h.shape, 0)
    hprev = jnp.where(rows == 0, carry_ref[...], pltpu.roll(h, 1, axis=0))
    carry_ref[...] = h[tt - 1:tt, :]
    dx = hprev - h
    hb = h.astype(BF16)
    q_ref[...] = _dot(hb, wm_ref[:, C_Q:C_K]).astype(BF16)
    k_ref[...] = _dot(hb, wm_ref[:, C_K:C_V])
    v_ref[...] = _dot(hb, wm_ref[:, C_V:C_RKV])
    rkv_ref[...] = _dot(hb, wm_ref[:, C_RKV:C_GA]).astype(BF16)
    gate_ref[...] = _sigmoid(_dot(hb, wm_ref[:, C_GA:C_END])).astype(BF16)
    ff = _dot(hb, wff_ref[...]) + bff_ref[...]
    logf_ref[...] = -_softplus(-ff)
    mu = mu_ref[...]
    lmid_ref[:, 0:64] = _dot((h + dx * mu[0:1]).astype(BF16), w1_ref[...])
    lmid_ref[:, 64:128] = _dot((h + dx * mu[1:2]).astype(BF16), a1_ref[...])
    lmid_ref[:, 128:256] = _dot((h + dx * mu[2:3]).astype(BF16), g1_ref[...])


def _inproj(x, prefix, prev_row, ln_g, ln_b, wm, wff, bff, mu3, w1, a1, g1):
    B, Tx, D = x.shape
    n_pre = prefix.shape[0]
    T = Tx + n_pre
    tt = _largest_tile(T, 384)
    nt = T // tt
    assert n_pre % 8 == 0 and n_pre < tt
    tile = lambda w: pl.BlockSpec((None, tt, w), lambda b, t: (b, t, 0))
    if n_pre:
        x_spec = pl.BlockSpec((pl.Element(1), pl.Element(tt), pl.Element(D)),
                              lambda b, t: (b, pl.multiple_of(jnp.maximum(t * tt - n_pre, 0), 8), 0))
        pre_in = prefix
    else:
        x_spec = tile(D)
        pre_in = jnp.zeros((8, D), x.dtype)
    row = lambda w: pl.BlockSpec((None, 1, w), lambda b, t: (b, 0, 0))
    out_shape = [
        jax.ShapeDtypeStruct((B, T, D), F32),
        jax.ShapeDtypeStruct((B, T, HW), BF16),
        jax.ShapeDtypeStruct((B, T, HW), F32),
        jax.ShapeDtypeStruct((B, T, HW), F32),
        jax.ShapeDtypeStruct((B, T, 3 * HW), BF16),
        jax.ShapeDtypeStruct((B, T, 2 * D), BF16),
        jax.ShapeDtypeStruct((B, T, 256), F32),
        jax.ShapeDtypeStruct((B, T, LANES), F32),
        jax.ShapeDtypeStruct((B, 1, 3 * HW), F32),
    ]
    return pl.pallas_call(
        functools.partial(_inproj_kernel, n_pre=n_pre),
        grid=(B, nt),
        in_specs=[x_spec, _full(pre_in.shape), row(D), _full((1, D)), _full((1, D)), _full(wm.shape), _full(wff.shape),
                  _full(bff.shape), _full(mu3.shape), _full(w1.shape), _full(a1.shape), _full(g1.shape)],
        out_specs=[tile(D), tile(HW), tile(HW), tile(HW), tile(3 * HW), tile(2 * D), tile(256), tile(LANES),
                   row(3 * HW)],
        out_shape=out_shape,
        scratch_shapes=[pltpu.VMEM((1, D), F32)],
        compiler_params=_params("arbitrary", "arbitrary"),
        name="inproj",
    )(x, pre_in, prev_row, ln_g, ln_b, wm, wff, bff, mu3, w1, a1, g1)


HEAD_PAD = 2 * HEAD_DIM
C_SPLIT = 3


def _aug_select_matrices():
    rows = np.arange(HW)
    sel_q = np.zeros((HW, N_HEADS * HEAD_PAD), np.float32)
    sel_q[rows, (rows // HEAD_DIM) * HEAD_PAD + rows % HEAD_DIM] = HEAD_DIM ** -0.5
    p = np.repeat(np.arange(C_SPLIT), N_HEADS)
    h = np.tile(np.arange(N_HEADS), C_SPLIT)
    sel_c = np.zeros((LANES, N_HEADS * HEAD_PAD), np.float32)
    sel_c[p * N_HEADS + h, h * HEAD_PAD + HEAD_DIM + p] = 1.0
    return jnp.asarray(sel_q, BF16), jnp.asarray(sel_c, BF16)


def _split3(x):
    hi = x.astype(BF16)
    r1 = x - hi.astype(F32)
    mid = r1.astype(BF16)
    return hi, mid, (r1 - mid.astype(F32)).astype(BF16)


def _fox_prep_kernel(lf_ref, k_ref, v_ref, *rest, n_new):
    if n_new:
        nlf_ref, nk_ref, nv_ref, selc_ref, ka_ref, vt_ref, carry_ref = rest
    else:
        selc_ref, ka_ref, vt_ref, carry_ref = rest
    t = pl.program_id(1)
    last = t == pl.num_programs(1) - 1

    def rows_of(ref, new_ref):
        if not n_new:
            return ref[...]
        x = ref[0]
        return jnp.where(last, jnp.concatenate([x[n_new:], new_ref[...]], axis=0), x)

    lf_in = rows_of(lf_ref, nlf_ref if n_new else None)
    k_in = rows_of(k_ref, nk_ref if n_new else None)
    v_in = rows_of(v_ref, nv_ref if n_new else None)
    tt = lf_in.shape[0]

    @pl.when(t == 0)
    def _():
        carry_ref[...] = jnp.zeros_like(carry_ref)

    r = lax.broadcasted_iota(I32, (tt, tt), 0)
    c = lax.broadcasted_iota(I32, (tt, tt), 1)
    tri = jnp.where(r >= c, 1.0, 0.0).astype(BF16)
    cs3 = _dot(tri, jnp.concatenate(_split3(lf_in), axis=1))
    cs = cs3[:, 0:LANES] + cs3[:, LANES:2 * LANES] + cs3[:, 2 * LANES:3 * LANES] + carry_ref[...]
    carry_ref[...] = cs[tt - 1:tt, :]
    hi, mid, lo = _split3(-cs)
    is_head = lax.broadcasted_iota(I32, (tt, LANES), 1) < N_HEADS
    keep = lambda part: jnp.where(is_head, part.astype(F32), 0.0)
    packed = keep(hi) + pltpu.roll(keep(mid), N_HEADS, axis=1) + pltpu.roll(keep(lo), 2 * N_HEADS, axis=1)
    kc = _dot(packed.astype(BF16), selc_ref[...])
    k = k_in
    pad = jnp.zeros((tt, HEAD_PAD - HEAD_DIM), F32)
    for h in range(N_HEADS):
        hp = slice(h * HEAD_PAD, (h + 1) * HEAD_PAD)
        ka_ref[:, hp] = (jnp.concatenate([k[:, h * HEAD_DIM:(h + 1) * HEAD_DIM], pad], axis=1) + kc[:, hp]).astype(BF16)
    ii = lax.broadcasted_iota(I32, (HW, HW), 0)
    jj = lax.broadcasted_iota(I32, (HW, HW), 1)
    eye = jnp.where(ii == jj, 1.0, 0.0).astype(BF16)
    vt_ref[...] = lax.dot_general(eye, v_in.astype(BF16), NT_DIMS, preferred_element_type=F32).astype(BF16)


def _fox_prep(logf, k, v, sel_c, new=None):
    B, P, _ = k.shape
    n_new = 0 if new is None else new[1].shape[1]
    T = P + n_new
    tt = _largest_tile(T, 384)
    nt = T // tt
    tile = lambda w: pl.BlockSpec((None, tt, w), lambda b, t: (b, t, 0))
    if n_new:
        assert n_new % 8 == 0 and n_new < tt <= P and (P - tt) % 8 == 0
        win = lambda w: pl.BlockSpec((pl.Element(1), pl.Element(tt), pl.Element(w)),
                                     lambda b, t: (b, pl.multiple_of(jnp.minimum(t * tt, P - tt), 8), 0))
        fresh = lambda w: pl.BlockSpec((None, n_new, w), lambda b, t: (b, 0, 0))
        in_specs = [win(LANES), win(HW), win(HW), fresh(LANES), fresh(HW), fresh(HW), _full(sel_c.shape)]
        operands = (logf, k, v) + tuple(new) + (sel_c,)
    else:
        in_specs = [tile(LANES), tile(HW), tile(HW), _full(sel_c.shape)]
        operands = (logf, k, v, sel_c)
    return pl.pallas_call(
        functools.partial(_fox_prep_kernel, n_new=n_new),
        grid=(B, nt),
        in_specs=in_specs,
        out_specs=[tile(N_HEADS * HEAD_PAD), pl.BlockSpec((None, None, HW, tt), lambda b, t: (b, t, 0, 0))],
        out_shape=[jax.ShapeDtypeStruct((B, T, N_HEADS * HEAD_PAD), BF16),
                   jax.ShapeDtypeStruct((B, nt, HW, tt), BF16)],
        scratch_shapes=[pltpu.VMEM((1, LANES), F32)],
        compiler_params=_params("arbitrary", "arbitrary"),
        name="fox_prep",
    )(*operands)


def _fox_kernel(qi_ref, ki_ref, last_ref, q_ref, ka_ref, vt_ref, selq_ref, o_ref, qa_ref, m_ref, l_ref, acc_ref,
                *, q0, tq, tk):
    p = pl.program_id(1)
    qi = qi_ref[p]
    ki = ki_ref[p]

    @pl.when(ki == 0)
    def _():
        m_ref[...] = jnp.full_like(m_ref, NEG_BIG)
        l_ref[...] = jnp.zeros_like(l_ref)
        acc_ref[...] = jnp.zeros_like(acc_ref)
        lane = lax.broadcasted_iota(I32, qa_ref.shape, 1) % HEAD_PAD
        ones = jnp.where(jnp.logical_and(lane >= HEAD_DIM, lane < HEAD_DIM + C_SPLIT), 1.0, 0.0)
        qa_ref[...] = (_dot(q_ref[...], selq_ref[...]) + ones).astype(BF16)

    first_q = q0 + qi * tq
    tile_first = ki * tk
    tile_last = tile_first + tk - 1

    def scores(h):
        hp = slice(h * HEAD_PAD, (h + 1) * HEAD_PAD)
        return lax.dot_general(ka_ref[:, hp], qa_ref[:, hp], NT_DIMS, preferred_element_type=F32)

    def tile_update(masked):
        if masked:
            key_pos = tile_first + lax.broadcasted_iota(I32, (tk, tq), 0)
            qry_pos = first_q + lax.broadcasted_iota(I32, (tk, tq), 1)
            bias = jnp.where(qry_pos >= key_pos, 0.0, NEG_BIG)
        m_all = m_ref[...]
        l_all = l_ref[...]
        m_rows, l_rows = [], []
        s_next = scores(0)
        for h in range(N_HEADS):
            hs = slice(h * HEAD_DIM, (h + 1) * HEAD_DIM)
            s = s_next
            if h + 1 < N_HEADS:
                s_next = scores(h + 1)
            if masked:
                s = s + bias
            m_prev = m_all[h:h + 1, :]
            m_new = jnp.maximum(m_prev, jnp.max(s, 0, keepdims=True))
            alpha = jnp.exp(m_prev - m_new)
            p = jnp.exp(s - m_new)
            l_rows.append(alpha * l_all[h:h + 1, :] + jnp.sum(p, 0, keepdims=True))
            m_rows.append(m_new)
            acc_ref[hs, :] = alpha * acc_ref[hs, :] + _dot(vt_ref[hs, :], p.astype(BF16))
        m_ref[...] = jnp.concatenate(m_rows, axis=0)
        l_ref[...] = jnp.concatenate(l_rows, axis=0)

    @pl.when(jnp.logical_and(tile_first <= first_q + tq - 1, tile_last > first_q))
    def _():
        tile_update(True)

    @pl.when(tile_last <= first_q)
    def _():
        tile_update(False)

    @pl.when(last_ref[p] == 1)
    def _():
        on = jnp.concatenate(
            [acc_ref[h * HEAD_DIM:(h + 1) * HEAD_DIM, :] / l_ref[h:h + 1, :] for h in range(N_HEADS)], axis=0)
        eye = jnp.where(lax.broadcasted_iota(I32, (tq, tq), 0) == lax.broadcasted_iota(I32, (tq, tq), 1),
                        1.0, 0.0).astype(BF16)
        o_ref[...] = lax.dot_general(eye, on.astype(BF16), NT_DIMS, preferred_element_type=F32).astype(o_ref.dtype)


def _fox_attention(q, k_aug, v_t, sel_q, q0):
    B, Tq, _ = q.shape
    Tk = k_aug.shape[1]
    WA = N_HEADS * HEAD_PAD
    tq = _largest_tile(Tq, 384)
    nk, tk = v_t.shape[1], v_t.shape[3]
    assert nk * tk == Tk
    nq = Tq // tq
    pairs = [(qi, ki) for qi in range(nq) for ki in range(min((q0 + (qi + 1) * tq - 1) // tk, nk - 1) + 1)]
    qi_tab = jnp.array([p[0] for p in pairs], I32)
    ki_tab = jnp.array([p[1] for p in pairs], I32)
    last_tab = jnp.array([int(i + 1 == len(pairs) or pairs[i + 1][0] != pairs[i][0]) for i in range(len(pairs))], I32)
    qspec = lambda w: pl.BlockSpec((None, tq, w), lambda b, p, qt, kt, lt: (b, qt[p], 0))
    grid_spec = pltpu.PrefetchScalarGridSpec(
        num_scalar_prefetch=3,
        grid=(B, len(pairs)),
        in_specs=[qspec(HW),
                  pl.BlockSpec((None, tk, WA), lambda b, p, qt, kt, lt: (b, kt[p], 0)),
                  pl.BlockSpec((None, None, HW, tk), lambda b, p, qt, kt, lt: (b, kt[p], 0, 0)),
                  pl.BlockSpec(sel_q.shape, lambda b, p, qt, kt, lt: (0, 0))],
        out_specs=qspec(HW),
        scratch_shapes=[pltpu.VMEM((tq, WA), BF16), pltpu.VMEM((N_HEADS, tq), F32),
                        pltpu.VMEM((N_HEADS, tq), F32), pltpu.VMEM((HW, tq), F32)],
    )
    return pl.pallas_call(
        functools.partial(_fox_kernel, q0=q0, tq=tq, tk=tk),
        grid_spec=grid_spec,
        out_shape=jax.ShapeDtypeStruct((B, Tq, HW), BF16),
        compiler_params=_params("arbitrary", "arbitrary"),
        name="fox_attention",
    )(qi_tab, ki_tab, last_tab, q, k_aug, v_t, sel_q)


def _rwkv_kernel(rkv_ref, lmid_ref, rkv0_ref, s0_ref, mu_ref, w0_ref, w2_ref, a0_ref, a2_ref, g2_ref,
                 kk_ref, ka_ref, rk_ref, gng_ref, gnb_ref, o_ref, sfin_ref, state_ref, carry_ref, *, chunk, levels):
    C = chunk
    NB, T = rkv_ref.shape[0], rkv_ref.shape[1]
    state_ref[...] = s0_ref[...]
    carry_ref[...] = rkv0_ref[...]
    row_w = lax.broadcasted_iota(I32, (C, 3 * HW), 0)
    row_h = lax.broadcasted_iota(I32, (C, HW), 0)
    r_i = lax.broadcasted_iota(I32, (C, C), 0)
    c_i = lax.broadcasted_iota(I32, (C, C), 1)
    strict = r_i > c_i
    incl = r_i >= c_i
    mid = C // 2 - 1 if C > 1 else 0
    hsl = [slice(h * HEAD_DIM, (h + 1) * HEAD_DIM) for h in range(N_HEADS)]
    nt = lambda x, y: lax.dot_general(x, y, NT_DIMS, preferred_element_type=F32)
    tn = lambda x, y: lax.dot_general(x, y, TN_DIMS, preferred_element_type=F32)

    def row_inputs(bb, off):
        x = rkv_ref[bb, pl.ds(off, C), :].astype(F32)
        prev = jnp.where(row_w == 0, carry_ref[bb], pltpu.roll(x, 1, axis=0))
        carry_ref[bb] = x[C - 1:C, :]
        x = x + (prev - x) * mu_ref[...]
        r, k0, v = x[:, 0:HW], x[:, HW:2 * HW], x[:, 2 * HW:3 * HW]
        lm = lmid_ref[bb, pl.ds(off, C), :]
        w_pre = w0_ref[...] + _dot(jnp.tanh(lm[:, 0:64]).astype(BF16), w2_ref[...])
        a = _sigmoid(a0_ref[...] + _dot(lm[:, 64:128].astype(BF16), a2_ref[...]))
        g = _dot(_sigmoid(lm[:, 128:256]).astype(BF16), g2_ref[...])
        w_log = -_softplus(-w_pre) - 0.5
        logdec = -jnp.exp(w_log)
        L = logdec
        sh = 1
        while sh < C:
            L = L + jnp.where(row_h >= sh, pltpu.roll(L, sh, axis=0), 0.0)
            sh *= 2
        l_mid = L[mid:mid + 1, :]
        l_tot = L[C - 1:C, :]
        return dict(r=r, v=v, a=a, g=g, kk_raw=k0 * kk_ref[...], k=k0 * (1.0 + (a - 1.0) * ka_ref[...]),
                    e_a=jnp.exp(L - logdec - l_mid), e_r=jnp.exp(L - l_mid), e_k=jnp.exp(l_mid - L),
                    e_s=jnp.exp(l_tot - L), w_tot=jnp.exp(l_tot), e_mid=jnp.exp(l_mid))

    gap = jnp.zeros((C, LANES - C), F32)
    gap2 = jnp.zeros((LANES - C, HEAD_DIM), F32)
    lane_pair = lambda left, right: jnp.concatenate([left, gap, right], axis=1)

    def chunk_body(i, carry):
        off = pl.multiple_of(i * C, C)
        rows = [row_inputs(bb, off) for bb in range(NB)]
        rk = rk_ref[...]
        chains = [(bb, h) for bb in range(NB) for h in range(N_HEADS)]
        X = range(len(chains))
        col = lambda name: [rows[bb][name][:, hsl[h]] for bb, h in chains]
        kkh = [x * lax.rsqrt(jnp.maximum(jnp.sum(x * x, -1, keepdims=True), 1e-24)) for x in col('kk_raw')]
        r_h, k_h, v_h, a_h = col('r'), col('k'), col('v'), col('a')
        e_a, e_r, e_k, e_s, w_tot, e_mid = col('e_a'), col('e_r'), col('e_k'), col('e_s'), col('w_tot'), col('e_mid')
        b_h = [kkh[c] * a_h[c] for c in X]
        ar = [jnp.concatenate([-kkh[c] * e_a[c], r_h[c] * e_r[c]], axis=0) for c in X]
        bkd = [jnp.concatenate([b_h[c] * e_k[c], gap2, k_h[c] * e_k[c]], axis=0) for c in X]
        s_old = [state_ref[bb, h] for bb, h in chains]
        gram = [nt(ar[c], bkd[c]) for c in X]
        x0 = [nt(ar[c], s_old[c] * e_mid[c]) for c in X]
        g_b = [gram[c][:, 0:C] for c in X]
        g_k = [gram[c][:, LANES:LANES + C] for c in X]
        a_mat = [jnp.where(strict, g_b[c][0:C], 0.0) for c in X]
        kv = [_dot(jnp.concatenate([jnp.where(strict, g_k[c][0:C], 0.0), jnp.where(incl, g_k[c][C:2 * C], 0.0)],
                                   axis=0), v_h[c]) for c in X]
        u = [x0[c][0:C] + kv[c][0:C] for c in X]
        for lvl in range(levels):
            if lvl + 1 < levels:
                prod = [_dot(a_mat[c], lane_pair(a_mat[c], u[c])) for c in X]
                a_mat = [prod[c][:, 0:C] for c in X]
                u = [u[c] + prod[c][:, LANES:LANES + HEAD_DIM] for c in X]
            else:
                u = [u[c] + _dot(a_mat[c], u[c]) for c in X]
        y = [x0[c][C:2 * C] + _dot(jnp.where(incl, g_b[c][C:2 * C], 0.0), u[c]) + kv[c][C:2 * C] for c in X]
        for c, (bb, h) in enumerate(chains):
            uv = jnp.concatenate([u[c], v_h[c]], axis=0)
            bks = jnp.concatenate([b_h[c] * e_s[c], k_h[c] * e_s[c]], axis=0)
            state_ref[bb, h] = s_old[c] * w_tot[c] + tn(uv, bks)
        outs = []
        for c, (bb, h) in enumerate(chains):
            mu = jnp.mean(y[c], -1, keepdims=True)
            yc = y[c] - mu
            var = jnp.mean(yc * yc, -1, keepdims=True)
            bonus = jnp.sum(r_h[c] * k_h[c] * rk[:, hsl[h]], -1, keepdims=True) * v_h[c]
            outs.append((yc * lax.rsqrt(var + GN_EPS), bonus))
        for bb in range(NB):
            mine = outs[bb * N_HEADS:(bb + 1) * N_HEADS]
            yn = jnp.concatenate([o[0] for o in mine], axis=1)
            bonus = jnp.concatenate([o[1] for o in mine], axis=1)
            out = (yn * gng_ref[...] + gnb_ref[...] + bonus) * rows[bb]['g']
            o_ref[bb, pl.ds(off, C), :] = out.astype(o_ref.dtype)
        return carry

    lax.fori_loop(0, T // C, chunk_body, 0)
    sfin_ref[...] = state_ref[...]


RWKV_ROWS_PER_STEP = 2


def _rwkv(rkv, lmid, rkv0, s0, mu_rkv, w0, w2, a0, a2, g2, k_k, k_a, r_k, gn_g, gn_b):
    B, T, _ = rkv.shape
    nb = RWKV_ROWS_PER_STEP if B % RWKV_ROWS_PER_STEP == 0 else 1
    chunk = _largest_tile(T, 64, mult=16)
    levels = max(1, math.ceil(math.log2(chunk)))
    seq =lambda w: pl.BlockSpec((nb, T, w), lambda b: (b, 0, 0))
    st = pl.BlockSpec((nb, N_HEADS, HEAD_DIM, HEAD_DIM), lambda b: (b, 0, 0, 0))
    vec = lambda a: _full(a.shape)
    params = (mu_rkv, w0, w2, a0, a2, g2, k_k, k_a, r_k, gn_g, gn_b)
    return pl.pallas_call(
        functools.partial(_rwkv_kernel, chunk=chunk, levels=levels),
        grid=(B // nb,),
        in_specs=[seq(3 * HW), seq(256), pl.BlockSpec((nb, 1, 3 * HW), lambda b: (b, 0, 0)), st]
                 + [vec(p) for p in params],
        out_specs=[seq(HW), st],
        out_shape=[jax.ShapeDtypeStruct((B, T, HW), BF16),
                   jax.ShapeDtypeStruct((B, N_HEADS, HEAD_DIM, HEAD_DIM), F32)],
        scratch_shapes=[pltpu.VMEM((nb, N_HEADS, HEAD_DIM, HEAD_DIM), F32), pltpu.VMEM((nb, 1, 3 * HW), F32)],
        compiler_params=_params("arbitrary"),
        name="rwkv7",
    )(rkv, lmid, rkv0, s0, *params)


def _merge_kernel(fox_ref, rw_ref, gate_ref, h_ref, wa_ref, wb_ref, wo_ref, g_ref, b_ref, wrh_ref, wrl_ref, br_ref,
                  cnt0_ref, h1_ref, h1p_ref, idx_ref, gt_ref, rank_ref, cnt_ref, carry_ref, *, dn_alpha):
    i = pl.program_id(0)
    tm, D = h_ref.shape

    @pl.when(i == 0)
    def _():
        carry_ref[...] = cnt0_ref[...].astype(F32)

    gates = gate_ref[...].astype(F32)
    merged = gates[:, 0:D] * _dot(fox_ref[...], wa_ref[...]) + gates[:, D:2 * D] * _dot(rw_ref[...], wb_ref[...])
    z = dn_alpha * h_ref[...] + _dot(merged.astype(BF16), wo_ref[...])
    h1 = _layer_norm(z, g_ref[...], b_ref[...])
    h1_ref[...] = h1
    h1p_ref[...] = _pack_bf16_pair(h1)
    hi = h1.astype(BF16)
    lo = (h1 - hi.astype(F32)).astype(BF16)
    logits = _dot(hi, wrh_ref[...]) + _dot(hi, wrl_ref[...]) + _dot(lo, wrh_ref[...]) + br_ref[...]
    lane = lax.broadcasted_iota(I32, (tm, LANES), 1)
    lane_f = lane.astype(F32)
    cur = logits
    vals, idxs = [], []
    for _ in range(TOP_K):
        m = jnp.max(cur, -1, keepdims=True)
        ix = jnp.min(jnp.where(cur == m, lane_f, float(LANES)), -1, keepdims=True)
        vals.append(m)
        idxs.append(ix)
        cur = jnp.where(lane_f == ix, -3e38, cur)
    exps = [jnp.exp(vk - vals[0]) for vk in vals]
    denom = exps[0] + exps[1] + exps[2] + exps[3]
    onehot = jnp.zeros((tm, LANES), F32)
    for ix in idxs:
        onehot = onehot + jnp.where(lane_f == ix, 1.0, 0.0)
    r_i = lax.broadcasted_iota(I32, (tm, tm), 0)
    c_i = lax.broadcasted_iota(I32, (tm, tm), 1)
    tri = jnp.where(r_i > c_i, 1.0, 0.0).astype(BF16)
    before = _dot(tri, onehot.astype(BF16)) + carry_ref[...]
    idx_out = jnp.zeros((tm, LANES), F32)
    gt_out = jnp.zeros((tm, LANES), F32)
    rank_out = jnp.zeros((tm, LANES), F32)
    for kx in range(TOP_K):
        rank_k = jnp.sum(jnp.where(lane_f == idxs[kx], before, 0.0), -1, keepdims=True)
        idx_out = jnp.where(lane == kx, idxs[kx], idx_out)
        gt_out = jnp.where(lane == kx, exps[kx] / denom, gt_out)
        rank_out = jnp.where(lane == kx, rank_k, rank_out)
    idx_ref[...] = idx_out.astype(I32)
    gt_ref[...] = gt_out
    rank_ref[...] = rank_out.astype(I32)
    total = carry_ref[...] + jnp.sum(onehot, 0, keepdims=True)
    carry_ref[...] = total
    cnt_ref[...] = total.astype(I32)


def _merge_route(fox, rw, gates, h, wa, wb, wo, ln_g, ln_b, wr_hi, wr_lo, br, counts_before, dn_alpha):
    N, D = h.shape
    tm = _largest_tile(N, 384)
    tile = lambda w: pl.BlockSpec((tm, w), lambda i: (i, 0))
    return pl.pallas_call(
        functools.partial(_merge_kernel, dn_alpha=dn_alpha),
        grid=(N // tm,),
        in_specs=[tile(HW), tile(HW), tile(2 * D), tile(D), _full(wa.shape), _full(wb.shape), _full(wo.shape),
                  _full((1, D)), _full((1, D)), _full(wr_hi.shape), _full(wr_lo.shape), _full(br.shape),
                  _full((1, LANES))],
        out_specs=[tile(D), tile(D // 2), tile(LANES), tile(LANES), tile(LANES), _full((1, LANES))],
        out_shape=[jax.ShapeDtypeStruct((N, D), F32), jax.ShapeDtypeStruct((N, D // 2), U32),
                   jax.ShapeDtypeStruct((N, LANES), I32),
                   jax.ShapeDtypeStruct((N, LANES), F32), jax.ShapeDtypeStruct((N, LANES), I32),
                   jax.ShapeDtypeStruct((1, LANES), I32)],
        scratch_shapes=[pltpu.VMEM((1, LANES), F32)],
        compiler_params=_params("arbitrary"),
        name="merge_route",
    )(fox, rw, gates, h, wa, wb, wo, ln_g, ln_b, wr_hi, wr_lo, br, counts_before)


PERM_W = 256


def _deinterleave_to_bf16(w_ref, g_ref, l_ref):
    half = PERM_W // 2
    ii = lax.broadcasted_iota(I32, (PERM_W, PERM_W), 0)
    jj = lax.broadcasted_iota(I32, (PERM_W, PERM_W), 1)
    src = jnp.where(jj < half, 2 * jj, 2 * (jj - half) + 1)
    perm = jnp.where(ii == src, 1.0, 0.0).astype(BF16)
    for c in range(w_ref.shape[1] // PERM_W):
        w = w_ref[:, c * PERM_W:(c + 1) * PERM_W].astype(BF16)
        out = _dot(w, perm)
        g_ref[:, c * half:(c + 1) * half] = out[:, :half].astype(BF16)
        l_ref[:, c * half:(c + 1) * half] = out[:, half:].astype(BF16)


def _dispatch_kernel(pad_ref, dest_ref, x_ref, *rest, n_pad):
    xs_hbm, zero_ref, sem = rest[-3:]
    i = pl.program_id(0)
    tm = x_ref.shape[0]

    if n_pad:
        @pl.when(i == 0)
        def _():
            zero_ref[...] = jnp.zeros_like(zero_ref)

            def zbody(r, c):
                pltpu.make_async_copy(zero_ref.at[pl.ds(0, 1)], xs_hbm.at[pl.ds(pad_ref[r], 1)], sem.at[1]).start()
                return c
            lax.fori_loop(0, n_pad, zbody, 0, unroll=8)
            for _ in range(n_pad // tm):
                pltpu.make_async_copy(zero_ref, xs_hbm.at[pl.ds(0, tm)], sem.at[1]).wait()
            if n_pad % tm:
                pltpu.make_async_copy(zero_ref.at[pl.ds(0, n_pad % tm)], xs_hbm.at[pl.ds(0, n_pad % tm)],
                                      sem.at[1]).wait()

    def body(r, c):
        for kx in range(TOP_K):
            d = dest_ref[0, 0, r * TOP_K + kx]
            pltpu.make_async_copy(x_ref.at[pl.ds(r, 1)], xs_hbm.at[pl.ds(d, 1)], sem.at[0]).start()
        return c
    lax.fori_loop(0, tm, body, 0, unroll=8)
    for _ in range(TOP_K):
        pltpu.make_async_copy(x_ref, xs_hbm.at[pl.ds(0, tm)], sem.at[0]).wait()


def _moe_dispatch(xp, dest, pad_slots, rows, extend=None):
    N, W = xp.shape
    tm = _largest_tile(N, 256)
    n = N // tm
    n_pad = 0 if extend is not None else pad_slots.shape[0]
    in_specs = [pl.BlockSpec((1, 1, tm * TOP_K), lambda i, pad: (i, 0, 0), memory_space=pltpu.SMEM),
                pl.BlockSpec((tm, W), lambda i, pad: (i, 0))]
    operands = [pad_slots, dest.reshape(n, 1, tm * TOP_K), xp]
    aliases = {}
    if extend is not None:
        in_specs.append(pl.BlockSpec(memory_space=pl.ANY))
        operands.append(extend)
        aliases = {len(operands) - 1: 0}
    grid_spec = pltpu.PrefetchScalarGridSpec(
        num_scalar_prefetch=1,
        grid=(n,),
        in_specs=in_specs,
        out_specs=pl.BlockSpec(memory_space=pl.ANY),
        scratch_shapes=[pltpu.VMEM((tm, W), U32), pltpu.SemaphoreType.DMA((2,))],
    )
    return pl.pallas_call(
        functools.partial(_dispatch_kernel, n_pad=n_pad),
        grid_spec=grid_spec,
        out_shape=jax.ShapeDtypeStruct((rows, W), U32),
        input_output_aliases=aliases,
        compiler_params=_params("arbitrary"),
        name="moe_dispatch",
    )(*operands)


def _moe_kernel(be_ref, nused_ref, xs_ref, w1_ref, b1g_ref, b1l_ref, w2_ref, b2_ref, y_ref, w1g_s, w1l_s, w2_s):
    j = pl.program_id(0)
    nused = nused_ref[0]
    last = jnp.maximum(nused - 1, 0)
    e_now = be_ref[jnp.minimum(j, last)]
    e_before = be_ref[jnp.minimum(jnp.maximum(j - 1, 0), last)]

    @pl.when(jnp.logical_or(j == 0, e_now != e_before))
    def _():
        _deinterleave_to_bf16(w1_ref, w1g_s, w1l_s)
        w2_s[...] = w2_ref[...].astype(BF16)

    @pl.when(j < nused)
    def _():
        lo, hi = _unpack_bf16_pair(xs_ref[...])
        x = jnp.concatenate([lo, hi], axis=1).astype(BF16)
        glu = jnp.minimum(_dot(x, w1g_s[...]) + b1g_ref[...], SWIGLU_LIMIT)
        lin = jnp.clip(_dot(x, w1l_s[...]) + b1l_ref[...], -SWIGLU_LIMIT, SWIGLU_LIMIT)
        act = glu * _sigmoid(SWIGLU_ALPHA * glu) * (lin + 1.0)
        y_ref[...] = _pack_bf16_pair(_dot(act.astype(BF16), w2_s[...]) + b2_ref[...])

    @pl.when(j >= nused)
    def _():
        y_ref[...] = jnp.zeros_like(y_ref)


def _moe_experts(xs, blk_e, nused, w1, b1g, b1l, w2, b2, bm):
    rows, W = xs.shape
    nb = rows // bm
    D, F = w1.shape[1], w1.shape[2] // 2
    last = lambda j, be, nu: jnp.minimum(j, jnp.maximum(nu[0] - 1, 0))
    wspec = lambda k, n: pl.BlockSpec((None, k, n), lambda j, be, nu: (be[last(j, be, nu)], 0, 0))
    grid_spec = pltpu.PrefetchScalarGridSpec(
        num_scalar_prefetch=2,
        grid=(nb,),
        in_specs=[pl.BlockSpec((bm, W), lambda j, be, nu: (j, 0)),
                  wspec(D, 2 * F), wspec(1, F), wspec(1, F), wspec(F, D), wspec(1, D)],
        out_specs=pl.BlockSpec((bm, W), lambda j, be, nu: (j, 0)),
        scratch_shapes=[pltpu.VMEM((D, F), BF16), pltpu.VMEM((D, F), BF16), pltpu.VMEM((F, D), BF16)],
    )
    return pl.pallas_call(
        _moe_kernel,
        grid_spec=grid_spec,
        out_shape=jax.ShapeDtypeStruct((rows, W), U32),
        compiler_params=_params("arbitrary"),
        name="moe_experts",
    )(blk_e, nused, xs, w1, b1g, b1l, w2, b2)


def _combine_gather_start(dest_ref, y_hbm, buf, sem, slot, tm):
    def body(r, c):
        for kx in range(TOP_K):
            d = dest_ref[0, 0, r * TOP_K + kx]
            pltpu.make_async_copy(y_hbm.at[pl.ds(d, 1)], buf.at[slot, kx, pl.ds(r, 1)], sem.at[slot]).start()
        return c
    lax.fori_loop(0, tm, body, 0, unroll=8)


def _combine_kernel(dest_ref, destn_ref, gt_ref, h1_ref, g_ref, b_ref, y_hbm, o_hbm, buf, sem, obuf, osem,
                    *, dn_alpha, nt, skip):
    i = pl.program_id(0)
    n = pl.num_programs(0)
    tm = h1_ref.shape[0]
    slot = i % 2
    b = i // nt
    j = i % nt

    def out_wait(rows):
        pltpu.make_async_copy(obuf.at[0, pl.ds(0, rows)], o_hbm.at[0, pl.ds(0, rows)], osem.at[0]).wait()

    @pl.when(i == 0)
    def _():
        _combine_gather_start(dest_ref, y_hbm, buf, sem, 0, tm)

    @pl.when(i + 1 < n)
    def _():
        _combine_gather_start(destn_ref, y_hbm, buf, sem, 1 - slot, tm)

    for kx in range(TOP_K):
        pltpu.make_async_copy(y_hbm.at[pl.ds(0, tm)], buf.at[slot, kx], sem.at[slot]).wait()
    gt = gt_ref[...]
    lo, hi = _unpack_bf16_pair(buf[slot, 0])
    moe_lo, moe_hi = gt[:, 0:1] * lo, gt[:, 0:1] * hi
    for kx in range(1, TOP_K):
        lo, hi = _unpack_bf16_pair(buf[slot, kx])
        moe_lo = moe_lo + gt[:, kx:kx + 1] * lo
        moe_hi = moe_hi + gt[:, kx:kx + 1] * hi
    moe = jnp.concatenate([moe_lo, moe_hi], axis=1)
    obuf[slot] = _layer_norm(dn_alpha * h1_ref[...] + moe, g_ref[...], b_ref[...])

    @pl.when(jnp.logical_and(i > 0, (i - 1) % nt == 0))
    def _():
        out_wait(tm - skip)

    @pl.when(jnp.logical_and(i > 0, (i - 1) % nt != 0))
    def _():
        out_wait(tm)

    @pl.when(j == 0)
    def _():
        pltpu.make_async_copy(obuf.at[slot, pl.ds(skip, tm - skip)], o_hbm.at[b, pl.ds(0, tm - skip)],
                              osem.at[0]).start()

    @pl.when(j != 0)
    def _():
        start = pl.multiple_of(j * tm - skip, 8)
        pltpu.make_async_copy(obuf.at[slot], o_hbm.at[b, pl.ds(start, tm)], osem.at[0]).start()

    @pl.when(i == n - 1)
    def _():
        if nt == 1:
            out_wait(tm - skip)
        else:
            out_wait(tm)


def _moe_combine(dest, gate, h1, ln_g, ln_b, yb, dn_alpha, B, T, skip):
    N, D = h1.shape
    tm = _largest_tile(T, 384)
    nt = T // tm
    n = N // tm
    assert skip % 8 == 0 and skip < tm
    dest3 = dest.reshape(n, 1, tm * TOP_K)
    tile = lambda w: pl.BlockSpec((tm, w), lambda i: (i, 0))
    return pl.pallas_call(
        functools.partial(_combine_kernel, dn_alpha=dn_alpha, nt=nt, skip=skip),
        grid=(n,),
        in_specs=[
            pl.BlockSpec((1, 1, tm * TOP_K), lambda i: (i, 0, 0), memory_space=pltpu.SMEM),
            pl.BlockSpec((1, 1, tm * TOP_K), lambda i: (jnp.minimum(i + 1, n - 1), 0, 0), memory_space=pltpu.SMEM),
            tile(LANES), tile(D), _full((1, D)), _full((1, D)),
            pl.BlockSpec(memory_space=pl.ANY),
        ],
        out_specs=pl.BlockSpec(memory_space=pl.ANY),
        out_shape=jax.ShapeDtypeStruct((B, T - skip, D), F32),
        scratch_shapes=[pltpu.VMEM((2, TOP_K, tm, yb.shape[1]), U32), pltpu.SemaphoreType.DMA((2,)),
                        pltpu.VMEM((2, tm, D), F32), pltpu.SemaphoreType.DMA((1,))],
        compiler_params=_params("arbitrary"),
        name="moe_combine",
    )(dest3, dest3, gate, h1, ln_g, ln_b, yb)


MOE_BLOCK_ROWS = 512


def _route_tables(n_asg, counts, n_experts):
    bm = min(MOE_BLOCK_ROWS, max(8, 1 << int(math.log2(max(1, n_asg // n_experts)))))
    nb = -(-n_asg // bm) + n_experts
    padded = (counts + bm - 1) // bm * bm
    pends = jnp.cumsum(padded)
    starts = (pends - padded).astype(I32)
    blk_start = jnp.arange(nb, dtype=I32) * bm
    blk_e = jnp.minimum(jnp.sum(pends[None, :] <= blk_start[:, None], axis=1), n_experts - 1).astype(I32)
    nused = (pends[-1] // bm).astype(I32).reshape(1)
    n_pad = nb * bm - n_asg
    gap = padded - counts
    gap_end = jnp.cumsum(gap)
    i = jnp.arange(n_pad, dtype=I32)
    e = jnp.sum(gap_end[None, :] <= i[:, None], axis=1)
    ec = jnp.minimum(e, n_experts - 1)
    in_group = (pends - padded + counts)[ec] + i - (gap_end - gap)[ec]
    pad_slots = jnp.where(e < n_experts, in_group, pends[-1] + i - gap_end[-1]).astype(I32)
    return starts, pad_slots, blk_e, nused, bm, nb * bm


def _mixers(x, prefix, prev_row, s0, past_k, past_v, past_logf, wts):
    B, D = x.shape[0], x.shape[2]
    T = x.shape[1] + prefix.shape[0]
    h, q, k, v, rkv, gates, lmid, logf, rkv0 = _inproj(
        x, prefix, prev_row, wts['ln0_g'], wts['ln0_b'], wts['wm'], wts['wff'], wts['bff'], wts['mu3'],
        wts['w1'], wts['a1'], wts['g1'])
    if past_k is None:
        k_aug, v_bf = _fox_prep(logf, k, v, wts['sel_c'])
        fox = _fox_attention(q, k_aug, v_bf, wts['sel_q'], 0)
    else:
        P = past_k.shape[1]
        past_pad = jnp.pad(past_logf.astype(F32), ((0, 0), (0, 0), (0, LANES - N_HEADS)))
        k_aug, v_bf = _fox_prep(past_pad, past_k.reshape(B, P, HW), past_v.reshape(B, P, HW), wts['sel_c'],
                                new=(logf, k, v))
        fox = _fox_attention(q, k_aug, v_bf, wts['sel_q'], P)
    rw, s_fin = _rwkv(rkv, lmid, rkv0, s0, wts['mu_rkv'], wts['w0'], wts['w2'], wts['a0'], wts['a2'], wts['g2'],
                      wts['k_k'], wts['k_a'], wts['r_k'], wts['gn_g'], wts['gn_b'])
    N = B * T
    tokens = dict(fox=fox.reshape(N, HW), rw=rw.reshape(N, HW), gates=gates.reshape(N, 2 * D), h=h.reshape(N, D))
    new_k = k.reshape(B, T, N_HEADS, HEAD_DIM)
    new_v = v.reshape(B, T, N_HEADS, HEAD_DIM)
    return tokens, (new_k, new_v, logf[:, :, :N_HEADS], s_fin, h[:, T - 1:T, :])


def _merge_moe(streams, wts):
    dn_alpha, n_experts = wts['dn_alpha'], wts['n_experts']
    counts = jnp.zeros((1, LANES), I32)
    routed = []
    for tokens, B, T, y_skip in streams:
        h1, h1p, top_idx, gate, rank, counts = _merge_route(
            tokens['fox'], tokens['rw'], tokens['gates'], tokens['h'],
            wts['w_up_a'], wts['w_up_b'], wts['w_out'], wts['ln1_g'], wts['ln1_b'],
            wts['wr_hi'], wts['wr_lo'], wts['br'], counts, dn_alpha)
        routed.append((h1, h1p, top_idx[:, :TOP_K], gate, rank[:, :TOP_K]))
    n_asg = sum(r[0].shape[0] for r in routed) * TOP_K
    starts, pad_slots, blk_e, nused, bm, rows = _route_tables(n_asg, counts[0, :n_experts], n_experts)
    dests = [(starts[top_idx] + rank).astype(I32) for _, _, top_idx, _, rank in routed]
    zero_slots = jnp.concatenate([pad_slots] + [d.reshape(-1) for d in dests[1:]])
    xs = None
    for (h1, h1p, _, _, _), dest in zip(routed, dests):
        xs = _moe_dispatch(h1p, dest, zero_slots, rows, extend=xs)
    yb = _moe_experts(xs, blk_e, nused, wts['we1'], wts['b1g'], wts['b1l'], wts['we2'], wts['be2'], bm)
    return [_moe_combine(dest, gate, h1, wts['ln2_g'], wts['ln2_b'], yb, dn_alpha, B, T, y_skip)
            for (h1, _, _, gate, _), dest, (_, B, T, y_skip) in zip(routed, dests, streams)]


def kernel(x_prompt, x_sample, cache_fox_k, cache_fox_v, cache_fox_logf, state_rwkv, state_shift, meta, ln0_g, ln0_b, w_in, b_forget, mu_w, mu_a, mu_g, mu_rkv, w0, w1, w2, a0, a1, a2, g1, g2, k_k, k_a, r_k, gn_g, gn_b, w_up_a, w_up_b, w_out, ln1_g, ln1_b, w_router, b_router, w_e1, b_e1, w_e2, b_e2, ln2_g, ln2_b):
    depth, D, in_cols = w_in.shape
    assert depth == 1 and D == 1024 and in_cols == 6 * HW + N_HEADS + 2 * D
    n_experts = w_router.shape[2]
    assert n_experts <= LANES
    B = x_prompt.shape[0]
    l = 0
    w = w_in[l]
    off_ff = 3 * HW
    row = lambda a: a.reshape(1, -1).astype(F32)
    wr = jnp.pad(w_router[l], ((0, 0), (0, LANES - n_experts)))
    wr_hi = wr.astype(BF16)
    sel_q, sel_c = _aug_select_matrices()
    wts = dict(
        sel_q=sel_q, sel_c=sel_c,
        dn_alpha=float((2 * depth) ** 0.25), n_experts=n_experts,
        ln0_g=row(ln0_g), ln0_b=row(ln0_b),
        wm=jnp.concatenate([w[:, :off_ff], w[:, off_ff + N_HEADS:]], axis=1).astype(BF16),
        wff=jnp.pad(w[:, off_ff:off_ff + N_HEADS], ((0, 0), (0, LANES - N_HEADS))).astype(BF16),
        bff=jnp.pad(row(b_forget[l]), ((0, 0), (0, LANES - N_HEADS))),
        mu3=jnp.stack([mu_w[l], mu_a[l], mu_g[l]], axis=0),
        w1=w1[l].astype(BF16), a1=a1[l].astype(BF16), g1=g1[l].astype(BF16),
        mu_rkv=row(mu_rkv[l]), w0=row(w0[l]), w2=w2[l].astype(BF16), a0=row(a0[l]), a2=a2[l].astype(BF16),
        g2=g2[l].astype(BF16), k_k=row(k_k[l]), k_a=row(k_a[l]), r_k=row(r_k[l]), gn_g=row(gn_g[l]),
        gn_b=row(gn_b[l]),
        w_up_a=w_up_a[l].astype(BF16), w_up_b=w_up_b[l].astype(BF16), w_out=w_out[l].astype(BF16),
        ln1_g=row(ln1_g[l]), ln1_b=row(ln1_b[l]),
        wr_hi=wr_hi, wr_lo=(wr - wr_hi.astype(F32)).astype(BF16),
        br=jnp.pad(row(b_router[l]), ((0, 0), (0, LANES - n_experts)), constant_values=NEG_BIG),
        we1=w_e1[l],
        b1g=b_e1[l][:, None, 0::2], b1l=b_e1[l][:, None, 1::2],
        we2=w_e2[l], be2=b_e2[l][:, None, :],
        ln2_g=row(ln2_g[l]), ln2_b=row(ln2_b[l]),
    )
    zero_row = jnp.zeros((B, 1, D), F32)
    zero_state = jnp.zeros((B, N_HEADS, HEAD_DIM, HEAD_DIM), F32)
    no_prefix = jnp.zeros((0, D), x_sample.dtype)
    tok_p, (k_p, v_p, lf_p, s_p, sh_p) = _mixers(x_prompt, meta.astype(x_prompt.dtype), zero_row, zero_state,
                                                 None, None, None, wts)
    tok_s, (k_s, v_s, lf_s, s_s, sh_s) = _mixers(x_sample, no_prefix, state_shift[l], state_rwkv[l], cache_fox_k[l],
                                                 cache_fox_v[l], cache_fox_logf[l], wts)
    y_p, y_s = _merge_moe([(tok_p, B, x_prompt.shape[1] + N_META, N_META), (tok_s,) + x_sample.shape[:2] + (0,)],
                          wts)
    ex = lambda a: a[None]
    return (y_p, y_s, ex(k_p), ex(v_p), ex(lf_p), ex(s_p), ex(sh_p),
            ex(k_s), ex(v_s), ex(lf_s), ex(s_s), ex(sh_s))
```

```python
import functools
import math

import jax
import jax.numpy as jnp
import numpy as np
from jax import lax
from jax.experimental import pallas as pl
from jax.experimental.pallas import tpu as pltpu

F32 = jnp.float32
BF16 = jnp.bfloat16
I32 = jnp.int32
U32 = jnp.uint32

N_META = 16
HEAD_DIM = 64
N_HEADS = 8
HW = N_HEADS * HEAD_DIM
TOP_K = 4
SWIGLU_LIMIT = 7.0
SWIGLU_ALPHA = 1.702
LN_EPS = 1e-5
GN_EPS = 64e-5
LANES = 128
NEG_BIG = -1e30
VMEM_LIMIT_BYTES = 56 * 1024 * 1024
HIGHEST = lax.Precision.HIGHEST

NT_DIMS = (((1,), (1,)), ((), ()))
TN_DIMS = (((0,), (0,)), ((), ()))


def _params(*sem):
    return pltpu.CompilerParams(dimension_semantics=sem, vmem_limit_bytes=VMEM_LIMIT_BYTES)


def _largest_tile(n, cap, mult=8):
    best = None
    for d in range(mult, min(n, cap) + 1, mult):
        if n % d == 0:
            best = d
    assert best is not None, (n, cap, mult)
    return best


def _sigmoid(x):
    return 1.0 / (1.0 + jnp.exp(-x))


def _softplus(x):
    return jnp.maximum(x, 0.0) + jnp.log1p(jnp.exp(-jnp.abs(x)))


def _layer_norm(x, g, b):
    mu = jnp.mean(x, -1, keepdims=True)
    xc = x - mu
    var = jnp.mean(xc * xc, -1, keepdims=True)
    return xc * lax.rsqrt(var + LN_EPS) * g + b


def _dot(a, b):
    return jnp.dot(a, b, preferred_element_type=F32)


def _full(shape):
    n = len(shape)
    return pl.BlockSpec(shape, lambda *_: (0,) * n)


def _pack_bf16_pair(x):
    w = x.shape[1] // 2
    bits = lambda t: lax.bitcast_convert_type(t.astype(BF16).astype(F32), U32)
    return (bits(x[:, :w]) >> 16) | (bits(x[:, w:]) & jnp.uint32(0xFFFF0000))


def _unpack_bf16_pair(u):
    return lax.bitcast_convert_type(u << 16, F32), lax.bitcast_convert_type(u & jnp.uint32(0xFFFF0000), F32)


C_Q, C_K, C_V, C_RKV, C_GA, C_END = 0, HW, 2 * HW, 3 * HW, 6 * HW, 6 * HW + 2048


def _inproj_kernel(x_ref, pre_ref, prev_ref, g_ref, b_ref, wm_ref, wff_ref, bff_ref, mu_ref, w1_ref, a1_ref, g1_ref,
                   h_ref, q_ref, k_ref, v_ref, rkv_ref, gate_ref, lmid_ref, logf_ref, rkv0_ref,
                   carry_ref, *, n_pre):
    t = pl.program_id(1)
    x = x_ref[0] if n_pre else x_ref[...]
    tt = x.shape[0]
    if n_pre:
        x = jnp.where(t == 0, jnp.concatenate([pre_ref[...], x[0:tt - n_pre]], axis=0), x)
    h = _layer_norm(x, g_ref[...], b_ref[...])
    h_ref[...] = h

    @pl.when(t == 0)
    def _():
        prev = prev_ref[...]
        carry_ref[...] = prev
        p8 = jnp.broadcast_to(prev, (8, prev.shape[1])).astype(BF16)
        rkv0_ref[...] = _dot(p8, wm_ref[:, C_RKV:C_GA])[0:1]

    rows = lax.broadcasted_iota(I32, h.shape, 0)
    hprev = jnp.where(rows == 0, carry_ref[...], pltpu.roll(h, 1, axis=0))
    carry_ref[...] = h[tt - 1:tt, :]
    dx = hprev - h
    hb = h.astype(BF16)
    q_ref[...] = _dot(hb, wm_ref[:, C_Q:C_K]).astype(BF16)
    k_ref[...] = _dot(hb, wm_ref[:, C_K:C_V])
    v_ref[...] = _dot(hb, wm_ref[:, C_V:C_RKV])
    rkv_ref[...] = _dot(hb, wm_ref[:, C_RKV:C_GA]).astype(BF16)
    gate_ref[...] = _sigmoid(_dot(hb, wm_ref[:, C_GA:C_END])).astype(BF16)
    ff = _dot(hb, wff_ref[...]) + bff_ref[...]
    logf_ref[...] = -_softplus(-ff)
    mu = mu_ref[...]
    lmid_ref[:, 0:64] = _dot((h + dx * mu[0:1]).astype(BF16), w1_ref[...])
    lmid_ref[:, 64:128] = _dot((h + dx * mu[1:2]).astype(BF16), a1_ref[...])
    lmid_ref[:, 128:256] = _dot((h + dx * mu[2:3]).astype(BF16), g1_ref[...])


def _inproj(x, prefix, prev_row, ln_g, ln_b, wm, wff, bff, mu3, w1, a1, g1):
    B, Tx, D = x.shape
    n_pre = prefix.shape[0]
    T = Tx + n_pre
    tt = _largest_tile(T, 384)
    nt = T // tt
    assert n_pre % 8 == 0 and n_pre < tt
    tile = lambda w: pl.BlockSpec((None, tt, w), lambda b, t: (b, t, 0))
    if n_pre:
        x_spec = pl.BlockSpec((pl.Element(1), pl.Element(tt), pl.Element(D)),
                              lambda b, t: (b, pl.multiple_of(jnp.maximum(t * tt - n_pre, 0), 8), 0))
        pre_in = prefix
    else:
        x_spec = tile(D)
        pre_in = jnp.zeros((8, D), x.dtype)
    row = lambda w: pl.BlockSpec((None, 1, w), lambda b, t: (b, 0, 0))
    out_shape = [
        jax.ShapeDtypeStruct((B, T, D), F32),
        jax.ShapeDtypeStruct((B, T, HW), BF16),
        jax.ShapeDtypeStruct((B, T, HW), F32),
        jax.ShapeDtypeStruct((B, T, HW), F32),
        jax.ShapeDtypeStruct((B, T, 3 * HW), BF16),
        jax.ShapeDtypeStruct((B, T, 2 * D), BF16),
        jax.ShapeDtypeStruct((B, T, 256), F32),
        jax.ShapeDtypeStruct((B, T, LANES), F32),
        jax.ShapeDtypeStruct((B, 1, 3 * HW), F32),
    ]
    return pl.pallas_call(
        functools.partial(_inproj_kernel, n_pre=n_pre),
        grid=(B, nt),
        in_specs=[x_spec, _full(pre_in.shape), row(D), _full((1, D)), _full((1, D)), _full(wm.shape), _full(wff.shape),
                  _full(bff.shape), _full(mu3.shape), _full(w1.shape), _full(a1.shape), _full(g1.shape)],
        out_specs=[tile(D), tile(HW), tile(HW), tile(HW), tile(3 * HW), tile(2 * D), tile(256), tile(LANES),
                   row(3 * HW)],
        out_shape=out_shape,
        scratch_shapes=[pltpu.VMEM((1, D), F32)],
        compiler_params=_params("arbitrary", "arbitrary"),
        name="inproj",
    )(x, pre_in, prev_row, ln_g, ln_b, wm, wff, bff, mu3, w1, a1, g1)


HEAD_PAD = 2 * HEAD_DIM
C_SPLIT = 3


def _aug_select_matrices():
    rows = np.arange(HW)
    sel_q = np.zeros((HW, N_HEADS * HEAD_PAD), np.float32)
    sel_q[rows, (rows // HEAD_DIM) * HEAD_PAD + rows % HEAD_DIM] = HEAD_DIM ** -0.5
    p = np.repeat(np.arange(C_SPLIT), N_HEADS)
    h = np.tile(np.arange(N_HEADS), C_SPLIT)
    sel_c = np.zeros((LANES, N_HEADS * HEAD_PAD), np.float32)
    sel_c[p * N_HEADS + h, h * HEAD_PAD + HEAD_DIM + p] = 1.0
    return jnp.asarray(sel_q, BF16), jnp.asarray(sel_c, BF16)


def _split3(x):
    hi = x.astype(BF16)
    r1 = x - hi.astype(F32)
    mid = r1.astype(BF16)
    return hi, mid, (r1 - mid.astype(F32)).astype(BF16)


def _fox_prep_kernel(lf_ref, k_ref, v_ref, *rest, n_new):
    if n_new:
        nlf_ref, nk_ref, nv_ref, selc_ref, ka_ref, vt_ref, carry_ref = rest
    else:
        selc_ref, ka_ref, vt_ref, carry_ref = rest
    t = pl.program_id(1)
    last = t == pl.num_programs(1) - 1

    def rows_of(ref, new_ref):
        if not n_new:
            return ref[...]
        x = ref[0]
        return jnp.where(last, jnp.concatenate([x[n_new:], new_ref[...]], axis=0), x)

    lf_in = rows_of(lf_ref, nlf_ref if n_new else None)
    k_in = rows_of(k_ref, nk_ref if n_new else None)
    v_in = rows_of(v_ref, nv_ref if n_new else None)
    tt = lf_in.shape[0]

    @pl.when(t == 0)
    def _():
        carry_ref[...] = jnp.zeros_like(carry_ref)

    r = lax.broadcasted_iota(I32, (tt, tt), 0)
    c = lax.broadcasted_iota(I32, (tt, tt), 1)
    tri = jnp.where(r >= c, 1.0, 0.0).astype(BF16)
    cs3 = _dot(tri, jnp.concatenate(_split3(lf_in), axis=1))
    cs = cs3[:, 0:LANES] + cs3[:, LANES:2 * LANES] + cs3[:, 2 * LANES:3 * LANES] + carry_ref[...]
    carry_ref[...] = cs[tt - 1:tt, :]
    hi, mid, lo = _split3(-cs)
    is_head = lax.broadcasted_iota(I32, (tt, LANES), 1) < N_HEADS
    keep = lambda part: jnp.where(is_head, part.astype(F32), 0.0)
    packed = keep(hi) + pltpu.roll(keep(mid), N_HEADS, axis=1) + pltpu.roll(keep(lo), 2 * N_HEADS, axis=1)
    kc = _dot(packed.astype(BF16), selc_ref[...])
    k = k_in
    pad = jnp.zeros((tt, HEAD_PAD - HEAD_DIM), F32)
    for h in range(N_HEADS):
        hp = slice(h * HEAD_PAD, (h + 1) * HEAD_PAD)
        ka_ref[:, hp] = (jnp.concatenate([k[:, h * HEAD_DIM:(h + 1) * HEAD_DIM], pad], axis=1) + kc[:, hp]).astype(BF16)
    ii = lax.broadcasted_iota(I32, (HW, HW), 0)
    jj = lax.broadcasted_iota(I32, (HW, HW), 1)
    eye = jnp.where(ii == jj, 1.0, 0.0).astype(BF16)
    vt_ref[...] = lax.dot_general(eye, v_in.astype(BF16), NT_DIMS, preferred_element_type=F32).astype(BF16)


def _fox_prep(logf, k, v, sel_c, new=None):
    B, P, _ = k.shape
    n_new = 0 if new is None else new[1].shape[1]
    T = P + n_new
    tt = _largest_tile(T, 384)
    nt = T // tt
    tile = lambda w: pl.BlockSpec((None, tt, w), lambda b, t: (b, t, 0))
    if n_new:
        assert n_new % 8 == 0 and n_new < tt <= P and (P - tt) % 8 == 0
        win = lambda w: pl.BlockSpec((pl.Element(1), pl.Element(tt), pl.Element(w)),
                                     lambda b, t: (b, pl.multiple_of(jnp.minimum(t * tt, P - tt), 8), 0))
        fresh = lambda w: pl.BlockSpec((None, n_new, w), lambda b, t: (b, 0, 0))
        in_specs = [win(LANES), win(HW), win(HW), fresh(LANES), fresh(HW), fresh(HW), _full(sel_c.shape)]
        operands = (logf, k, v) + tuple(new) + (sel_c,)
    else:
        in_specs = [tile(LANES), tile(HW), tile(HW), _full(sel_c.shape)]
        operands = (logf, k, v, sel_c)
    return pl.pallas_call(
        functools.partial(_fox_prep_kernel, n_new=n_new),
        grid=(B, nt),
        in_specs=in_specs,
        out_specs=[tile(N_HEADS * HEAD_PAD), pl.BlockSpec((None, None, HW, tt), lambda b, t: (b, t, 0, 0))],
        out_shape=[jax.ShapeDtypeStruct((B, T, N_HEADS * HEAD_PAD), BF16),
                   jax.ShapeDtypeStruct((B, nt, HW, tt), BF16)],
        scratch_shapes=[pltpu.VMEM((1, LANES), F32)],
        compiler_params=_params("arbitrary", "arbitrary"),
        name="fox_prep",
    )(*operands)


def _fox_kernel(qi_ref, ki_ref, last_ref, q_ref, ka_ref, vt_ref, selq_ref, o_ref, qa_ref, m_ref, l_ref, acc_ref,
                *, q0, tq, tk):
    p = pl.program_id(1)
    qi = qi_ref[p]
    ki = ki_ref[p]

    @pl.when(ki == 0)
    def _():
        m_ref[...] = jnp.full_like(m_ref, NEG_BIG)
        l_ref[...] = jnp.zeros_like(l_ref)
        acc_ref[...] = jnp.zeros_like(acc_ref)
        lane = lax.broadcasted_iota(I32, qa_ref.shape, 1) % HEAD_PAD
        ones = jnp.where(jnp.logical_and(lane >= HEAD_DIM, lane < HEAD_DIM + C_SPLIT), 1.0, 0.0)
        qa_ref[...] = (_dot(q_ref[...], selq_ref[...]) + ones).astype(BF16)

    first_q = q0 + qi * tq
    tile_first = ki * tk
    tile_last = tile_first + tk - 1

    def scores(h):
        hp = slice(h * HEAD_PAD, (h + 1) * HEAD_PAD)
        return lax.dot_general(ka_ref[:, hp], qa_ref[:, hp], NT_DIMS, preferred_element_type=F32)

    def tile_update(masked):
        if masked:
            key_pos = tile_first + lax.broadcasted_iota(I32, (tk, tq), 0)
            qry_pos = first_q + lax.broadcasted_iota(I32, (tk, tq), 1)
            bias = jnp.where(qry_pos >= key_pos, 0.0, NEG_BIG)
        m_all = m_ref[...]
        l_all = l_ref[...]
        m_rows, l_rows = [], []
        s_next = scores(0)
        for h in range(N_HEADS):
            hs = slice(h * HEAD_DIM, (h + 1) * HEAD_DIM)
            s = s_next
            if h + 1 < N_HEADS:
                s_next = scores(h + 1)
            if masked:
                s = s + bias
            m_prev = m_all[h:h + 1, :]
            m_new = jnp.maximum(m_prev, jnp.max(s, 0, keepdims=True))
            alpha = jnp.exp(m_prev - m_new)
            p = jnp.exp(s - m_new)
            l_rows.append(alpha * l_all[h:h + 1, :] + jnp.sum(p, 0, keepdims=True))
            m_rows.append(m_new)
            acc_ref[hs, :] = alpha * acc_ref[hs, :] + _dot(vt_ref[hs, :], p.astype(BF16))
        m_ref[...] = jnp.concatenate(m_rows, axis=0)
        l_ref[...] = jnp.concatenate(l_rows, axis=0)

    @pl.when(jnp.logical_and(tile_first <= first_q + tq - 1, tile_last > first_q))
    def _():
        tile_update(True)

    @pl.when(tile_last <= first_q)
    def _():
        tile_update(False)

    @pl.when(last_ref[p] == 1)
    def _():
        on = jnp.concatenate(
            [acc_ref[h * HEAD_DIM:(h + 1) * HEAD_DIM, :] / l_ref[h:h + 1, :] for h in range(N_HEADS)], axis=0)
        eye = jnp.where(lax.broadcasted_iota(I32, (tq, tq), 0) == lax.broadcasted_iota(I32, (tq, tq), 1),
                        1.0, 0.0).astype(BF16)
        o_ref[...] = lax.dot_general(eye, on.astype(BF16), NT_DIMS, preferred_element_type=F32).astype(o_ref.dtype)


def _fox_attention(q, k_aug, v_t, sel_q, q0):
    B, Tq, _ = q.shape
    Tk = k_aug.shape[1]
    WA = N_HEADS * HEAD_PAD
    tq = _largest_tile(Tq, 384)
    nk, tk = v_t.shape[1], v_t.shape[3]
    assert nk * tk == Tk
    nq = Tq // tq
    pairs = [(qi, ki) for qi in range(nq) for ki in range(min((q0 + (qi + 1) * tq - 1) // tk, nk - 1) + 1)]
    qi_tab = jnp.array([p[0] for p in pairs], I32)
    ki_tab = jnp.array([p[1] for p in pairs], I32)
    last_tab = jnp.array([int(i + 1 == len(pairs) or pairs[i + 1][0] != pairs[i][0]) for i in range(len(pairs))], I32)
    qspec = lambda w: pl.BlockSpec((None, tq, w), lambda b, p, qt, kt, lt: (b, qt[p], 0))
    grid_spec = pltpu.PrefetchScalarGridSpec(
        num_scalar_prefetch=3,
        grid=(B, len(pairs)),
        in_specs=[qspec(HW),
                  pl.BlockSpec((None, tk, WA), lambda b, p, qt, kt, lt: (b, kt[p], 0)),
                  pl.BlockSpec((None, None, HW, tk), lambda b, p, qt, kt, lt: (b, kt[p], 0, 0)),
                  pl.BlockSpec(sel_q.shape, lambda b, p, qt, kt, lt: (0, 0))],
        out_specs=qspec(HW),
        scratch_shapes=[pltpu.VMEM((tq, WA), BF16), pltpu.VMEM((N_HEADS, tq), F32),
                        pltpu.VMEM((N_HEADS, tq), F32), pltpu.VMEM((HW, tq), F32)],
    )
    return pl.pallas_call(
        functools.partial(_fox_kernel, q0=q0, tq=tq, tk=tk),
        grid_spec=grid_spec,
        out_shape=jax.ShapeDtypeStruct((B, Tq, HW), BF16),
        compiler_params=_params("arbitrary", "arbitrary"),
        name="fox_attention",
    )(qi_tab, ki_tab, last_tab, q, k_aug, v_t, sel_q)


def _rwkv_kernel(rkv_ref, lmid_ref, rkv0_ref, s0_ref, mu_ref, w0_ref, w2_ref, a0_ref, a2_ref, g2_ref,
                 kk_ref, ka_ref, rk_ref, gng_ref, gnb_ref, o_ref, sfin_ref, state_ref, carry_ref, *, chunk, levels):
    C = chunk
    NB, T = rkv_ref.shape[0], rkv_ref.shape[1]
    state_ref[...] = s0_ref[...]
    carry_ref[...] = rkv0_ref[...]
    row_w = lax.broadcasted_iota(I32, (C, 3 * HW), 0)
    row_h = lax.broadcasted_iota(I32, (C, HW), 0)
    r_i = lax.broadcasted_iota(I32, (C, C), 0)
    c_i = lax.broadcasted_iota(I32, (C, C), 1)
    strict = r_i > c_i
    incl = r_i >= c_i
    mid = C // 2 - 1 if C > 1 else 0
    hsl = [slice(h * HEAD_DIM, (h + 1) * HEAD_DIM) for h in range(N_HEADS)]
    nt = lambda x, y: lax.dot_general(x, y, NT_DIMS, preferred_element_type=F32)
    tn = lambda x, y: lax.dot_general(x, y, TN_DIMS, preferred_element_type=F32)

    def row_inputs(bb, off):
        x = rkv_ref[bb, pl.ds(off, C), :].astype(F32)
        prev = jnp.where(row_w == 0, carry_ref[bb], pltpu.roll(x, 1, axis=0))
        carry_ref[bb] = x[C - 1:C, :]
        x = x + (prev - x) * mu_ref[...]
        r, k0, v = x[:, 0:HW], x[:, HW:2 * HW], x[:, 2 * HW:3 * HW]
        lm = lmid_ref[bb, pl.ds(off, C), :]
        w_pre = w0_ref[...] + _dot(jnp.tanh(lm[:, 0:64]).astype(BF16), w2_ref[...])
        a = _sigmoid(a0_ref[...] + _dot(lm[:, 64:128].astype(BF16), a2_ref[...]))
        g = _dot(_sigmoid(lm[:, 128:256]).astype(BF16), g2_ref[...])
        w_log = -_softplus(-w_pre) - 0.5
        logdec = -jnp.exp(w_log)
        L = logdec
        sh = 1
        while sh < C:
            L = L + jnp.where(row_h >= sh, pltpu.roll(L, sh, axis=0), 0.0)
            sh *= 2
        l_mid = L[mid:mid + 1, :]
        l_tot = L[C - 1:C, :]
        return dict(r=r, v=v, a=a, g=g, kk_raw=k0 * kk_ref[...], k=k0 * (1.0 + (a - 1.0) * ka_ref[...]),
                    e_a=jnp.exp(L - logdec - l_mid), e_r=jnp.exp(L - l_mid), e_k=jnp.exp(l_mid - L),
                    e_s=jnp.exp(l_tot - L), w_tot=jnp.exp(l_tot), e_mid=jnp.exp(l_mid))

    gap = jnp.zeros((C, LANES - C), F32)
    gap2 = jnp.zeros((LANES - C, HEAD_DIM), F32)
    lane_pair = lambda left, right: jnp.concatenate([left, gap, right], axis=1)

    def chunk_body(i, carry):
        off = pl.multiple_of(i * C, C)
        rows = [row_inputs(bb, off) for bb in range(NB)]
        rk = rk_ref[...]
        chains = [(bb, h) for bb in range(NB) for h in range(N_HEADS)]
        X = range(len(chains))
        col = lambda name: [rows[bb][name][:, hsl[h]] for bb, h in chains]
        kkh = [x * lax.rsqrt(jnp.maximum(jnp.sum(x * x, -1, keepdims=True), 1e-24)) for x in col('kk_raw')]
        r_h, k_h, v_h, a_h = col('r'), col('k'), col('v'), col('a')
        e_a, e_r, e_k, e_s, w_tot, e_mid = col('e_a'), col('e_r'), col('e_k'), col('e_s'), col('w_tot'), col('e_mid')
        b_h = [kkh[c] * a_h[c] for c in X]
        ar = [jnp.concatenate([-kkh[c] * e_a[c], r_h[c] * e_r[c]], axis=0) for c in X]
        bkd = [jnp.concatenate([b_h[c] * e_k[c], gap2, k_h[c] * e_k[c]], axis=0) for c in X]
        s_old = [state_ref[bb, h] for bb, h in chains]
        gram = [nt(ar[c], bkd[c]) for c in X]
        x0 = [nt(ar[c], s_old[c] * e_mid[c]) for c in X]
        g_b = [gram[c][:, 0:C] for c in X]
        g_k = [gram[c][:, LANES:LANES + C] for c in X]
        a_mat = [jnp.where(strict, g_b[c][0:C], 0.0) for c in X]
        kv = [_dot(jnp.concatenate([jnp.where(strict, g_k[c][0:C], 0.0), jnp.where(incl, g_k[c][C:2 * C], 0.0)],
                                   axis=0), v_h[c]) for c in X]
        u = [x0[c][0:C] + kv[c][0:C] for c in X]
        for lvl in range(levels):
            if lvl + 1 < levels:
                prod = [_dot(a_mat[c], lane_pair(a_mat[c], u[c])) for c in X]
                a_mat = [prod[c][:, 0:C] for c in X]
                u = [u[c] + prod[c][:, LANES:LANES + HEAD_DIM] for c in X]
            else:
                u = [u[c] + _dot(a_mat[c], u[c]) for c in X]
        y = [x0[c][C:2 * C] + _dot(jnp.where(incl, g_b[c][C:2 * C], 0.0), u[c]) + kv[c][C:2 * C] for c in X]
        for c, (bb, h) in enumerate(chains):
            uv = jnp.concatenate([u[c], v_h[c]], axis=0)
            bks = jnp.concatenate([b_h[c] * e_s[c], k_h[c] * e_s[c]], axis=0)
            state_ref[bb, h] = s_old[c] * w_tot[c] + tn(uv, bks)
        outs = []
        for c, (bb, h) in enumerate(chains):
            mu = jnp.mean(y[c], -1, keepdims=True)
            yc = y[c] - mu
            var = jnp.mean(yc * yc, -1, keepdims=True)
            bonus = jnp.sum(r_h[c] * k_h[c] * rk[:, hsl[h]], -1, keepdims=True) * v_h[c]
            outs.append((yc * lax.rsqrt(var + GN_EPS), bonus))
        for bb in range(NB):
            mine = outs[bb * N_HEADS:(bb + 1) * N_HEADS]
            yn = jnp.concatenate([o[0] for o in mine], axis=1)
            bonus = jnp.concatenate([o[1] for o in mine], axis=1)
            out = (yn * gng_ref[...] + gnb_ref[...] + bonus) * rows[bb]['g']
            o_ref[bb, pl.ds(off, C), :] = out.astype(o_ref.dtype)
        return carry

    lax.fori_loop(0, T // C, chunk_body, 0)
    sfin_ref[...] = state_ref[...]


RWKV_ROWS_PER_STEP = 2


def _rwkv(rkv, lmid, rkv0, s0, mu_rkv, w0, w2, a0, a2, g2, k_k, k_a, r_k, gn_g, gn_b):
    B, T, _ = rkv.shape
    nb = RWKV_ROWS_PER_STEP if B % RWKV_ROWS_PER_STEP == 0 else 1
    chunk = _largest_tile(T, 64, mult=16)
    levels = max(1, math.ceil(math.log2(chunk)))
    seq =lambda w: pl.BlockSpec((nb, T, w), lambda b: (b, 0, 0))
    st = pl.BlockSpec((nb, N_HEADS, HEAD_DIM, HEAD_DIM), lambda b: (b, 0, 0, 0))
    vec = lambda a: _full(a.shape)
    params = (mu_rkv, w0, w2, a0, a2, g2, k_k, k_a, r_k, gn_g, gn_b)
    return pl.pallas_call(
        functools.partial(_rwkv_kernel, chunk=chunk, levels=levels),
        grid=(B // nb,),
        in_specs=[seq(3 * HW), seq(256), pl.BlockSpec((nb, 1, 3 * HW), lambda b: (b, 0, 0)), st]
                 + [vec(p) for p in params],
        out_specs=[seq(HW), st],
        out_shape=[jax.ShapeDtypeStruct((B, T, HW), BF16),
                   jax.ShapeDtypeStruct((B, N_HEADS, HEAD_DIM, HEAD_DIM), F32)],
        scratch_shapes=[pltpu.VMEM((nb, N_HEADS, HEAD_DIM, HEAD_DIM), F32), pltpu.VMEM((nb, 1, 3 * HW), F32)],
        compiler_params=_params("arbitrary"),
        name="rwkv7",
    )(rkv, lmid, rkv0, s0, *params)


def _merge_kernel(fox_ref, rw_ref, gate_ref, h_ref, wa_ref, wb_ref, wo_ref, g_ref, b_ref, wrh_ref, wrl_ref, br_ref,
                  cnt0_ref, h1_ref, h1p_ref, idx_ref, gt_ref, rank_ref, cnt_ref, carry_ref, *, dn_alpha):
    i = pl.program_id(0)
    tm, D = h_ref.shape

    @pl.when(i == 0)
    def _():
        carry_ref[...] = cnt0_ref[...].astype(F32)

    gates = gate_ref[...].astype(F32)
    merged = gates[:, 0:D] * _dot(fox_ref[...], wa_ref[...]) + gates[:, D:2 * D] * _dot(rw_ref[...], wb_ref[...])
    z = dn_alpha * h_ref[...] + _dot(merged.astype(BF16), wo_ref[...])
    h1 = _layer_norm(z, g_ref[...], b_ref[...])
    h1_ref[...] = h1
    h1p_ref[...] = _pack_bf16_pair(h1)
    hi = h1.astype(BF16)
    lo = (h1 - hi.astype(F32)).astype(BF16)
    logits = _dot(hi, wrh_ref[...]) + _dot(hi, wrl_ref[...]) + _dot(lo, wrh_ref[...]) + br_ref[...]
    lane = lax.broadcasted_iota(I32, (tm, LANES), 1)
    lane_f = lane.astype(F32)
    cur = logits
    vals, idxs = [], []
    for _ in range(TOP_K):
        m = jnp.max(cur, -1, keepdims=True)
        ix = jnp.min(jnp.where(cur == m, lane_f, float(LANES)), -1, keepdims=True)
        vals.append(m)
        idxs.append(ix)
        cur = jnp.where(lane_f == ix, -3e38, cur)
    exps = [jnp.exp(vk - vals[0]) for vk in vals]
    denom = exps[0] + exps[1] + exps[2] + exps[3]
    onehot = jnp.zeros((tm, LANES), F32)
    for ix in idxs:
        onehot = onehot + jnp.where(lane_f == ix, 1.0, 0.0)
    r_i = lax.broadcasted_iota(I32, (tm, tm), 0)
    c_i = lax.broadcasted_iota(I32, (tm, tm), 1)
    tri = jnp.where(r_i > c_i, 1.0, 0.0).astype(BF16)
    before = _dot(tri, onehot.astype(BF16)) + carry_ref[...]
    idx_out = jnp.zeros((tm, LANES), F32)
    gt_out = jnp.zeros((tm, LANES), F32)
    rank_out = jnp.zeros((tm, LANES), F32)
    for kx in range(TOP_K):
        rank_k = jnp.sum(jnp.where(lane_f == idxs[kx], before, 0.0), -1, keepdims=True)
        idx_out = jnp.where(lane == kx, idxs[kx], idx_out)
        gt_out = jnp.where(lane == kx, exps[kx] / denom, gt_out)
        rank_out = jnp.where(lane == kx, rank_k, rank_out)
    idx_ref[...] = idx_out.astype(I32)
    gt_ref[...] = gt_out
    rank_ref[...] = rank_out.astype(I32)
    total = carry_ref[...] + jnp.sum(onehot, 0, keepdims=True)
    carry_ref[...] = total
    cnt_ref[...] = total.astype(I32)


def _merge_route(fox, rw, gates, h, wa, wb, wo, ln_g, ln_b, wr_hi, wr_lo, br, counts_before, dn_alpha):
    N, D = h.shape
    tm = _largest_tile(N, 384)
    tile = lambda w: pl.BlockSpec((tm, w), lambda i: (i, 0))
    return pl.pallas_call(
        functools.partial(_merge_kernel, dn_alpha=dn_alpha),
        grid=(N // tm,),
        in_specs=[tile(HW), tile(HW), tile(2 * D), tile(D), _full(wa.shape), _full(wb.shape), _full(wo.shape),
                  _full((1, D)), _full((1, D)), _full(wr_hi.shape), _full(wr_lo.shape), _full(br.shape),
                  _full((1, LANES))],
        out_specs=[tile(D), tile(D // 2), tile(LANES), tile(LANES), tile(LANES), _full((1, LANES))],
        out_shape=[jax.ShapeDtypeStruct((N, D), F32), jax.ShapeDtypeStruct((N, D // 2), U32),
                   jax.ShapeDtypeStruct((N, LANES), I32),
                   jax.ShapeDtypeStruct((N, LANES), F32), jax.ShapeDtypeStruct((N, LANES), I32),
                   jax.ShapeDtypeStruct((1, LANES), I32)],
        scratch_shapes=[pltpu.VMEM((1, LANES), F32)],
        compiler_params=_params("arbitrary"),
        name="merge_route",
    )(fox, rw, gates, h, wa, wb, wo, ln_g, ln_b, wr_hi, wr_lo, br, counts_before)


PERM_W = 256


def _deinterleave_to_bf16(w_ref, g_ref, l_ref):
    half = PERM_W // 2
    ii = lax.broadcasted_iota(I32, (PERM_W, PERM_W), 0)
    jj = lax.broadcasted_iota(I32, (PERM_W, PERM_W), 1)
    src = jnp.where(jj < half, 2 * jj, 2 * (jj - half) + 1)
    perm = jnp.where(ii == src, 1.0, 0.0).astype(BF16)
    for c in range(w_ref.shape[1] // PERM_W):
        w = w_ref[:, c * PERM_W:(c + 1) * PERM_W].astype(BF16)
        out = _dot(w, perm)
        g_ref[:, c * half:(c + 1) * half] = out[:, :half].astype(BF16)
        l_ref[:, c * half:(c + 1) * half] = out[:, half:].astype(BF16)


def _dispatch_kernel(pad_ref, dest_ref, x_ref, *rest, n_pad):
    xs_hbm, zero_ref, sem = rest[-3:]
    i = pl.program_id(0)
    tm = x_ref.shape[0]

    if n_pad:
        @pl.when(i == 0)
        def _():
            zero_ref[...] = jnp.zeros_like(zero_ref)

            def zbody(r, c):
                pltpu.make_async_copy(zero_ref.at[pl.ds(0, 1)], xs_hbm.at[pl.ds(pad_ref[r], 1)], sem.at[1]).start()
                return c
            lax.fori_loop(0, n_pad, zbody, 0, unroll=8)
            for _ in range(n_pad // tm):
                pltpu.make_async_copy(zero_ref, xs_hbm.at[pl.ds(0, tm)], sem.at[1]).wait()
            if n_pad % tm:
                pltpu.make_async_copy(zero_ref.at[pl.ds(0, n_pad % tm)], xs_hbm.at[pl.ds(0, n_pad % tm)],
                                      sem.at[1]).wait()

    for r in range(tm):
        for kx in range(TOP_K):
            d = dest_ref[0, 0, r * TOP_K + kx]
            pltpu.make_async_copy(x_ref.at[pl.ds(r, 1)], xs_hbm.at[pl.ds(d, 1)], sem.at[0]).start()
    for _ in range(TOP_K):
        pltpu.make_async_copy(x_ref, xs_hbm.at[pl.ds(0, tm)], sem.at[0]).wait()


def _moe_dispatch(xp, dest, pad_slots, rows, extend=None):
    N, W = xp.shape
    tm = _largest_tile(N, 256)
    n = N // tm
    n_pad = 0 if extend is not None else pad_slots.shape[0]
    in_specs = [pl.BlockSpec((1, 1, tm * TOP_K), lambda i, pad: (i, 0, 0), memory_space=pltpu.SMEM),
                pl.BlockSpec((tm, W), lambda i, pad: (i, 0))]
    operands = [pad_slots, dest.reshape(n, 1, tm * TOP_K), xp]
    aliases = {}
    if extend is not None:
        in_specs.append(pl.BlockSpec(memory_space=pl.ANY))
        operands.append(extend)
        aliases = {len(operands) - 1: 0}
    grid_spec = pltpu.PrefetchScalarGridSpec(
        num_scalar_prefetch=1,
        grid=(n,),
        in_specs=in_specs,
        out_specs=pl.BlockSpec(memory_space=pl.ANY),
        scratch_shapes=[pltpu.VMEM((tm, W), U32), pltpu.SemaphoreType.DMA((2,))],
    )
    return pl.pallas_call(
        functools.partial(_dispatch_kernel, n_pad=n_pad),
        grid_spec=grid_spec,
        out_shape=jax.ShapeDtypeStruct((rows, W), U32),
        input_output_aliases=aliases,
        compiler_params=_params("arbitrary"),
        name="moe_dispatch",
    )(*operands)


def _moe_kernel(be_ref, nused_ref, xs_ref, w1_ref, b1g_ref, b1l_ref, w2_ref, b2_ref, y_ref, w1g_s, w1l_s, w2_s):
    j = pl.program_id(0)
    nused = nused_ref[0]
    last = jnp.maximum(nused - 1, 0)
    e_now = be_ref[jnp.minimum(j, last)]
    e_before = be_ref[jnp.minimum(jnp.maximum(j - 1, 0), last)]

    @pl.when(jnp.logical_or(j == 0, e_now != e_before))
    def _():
        _deinterleave_to_bf16(w1_ref, w1g_s, w1l_s)
        w2_s[...] = w2_ref[...].astype(BF16)

    @pl.when(j < nused)
    def _():
        lo, hi = _unpack_bf16_pair(xs_ref[...])
        x = jnp.concatenate([lo, hi], axis=1).astype(BF16)
        glu = jnp.minimum(_dot(x, w1g_s[...]) + b1g_ref[...], SWIGLU_LIMIT)
        lin = jnp.clip(_dot(x, w1l_s[...]) + b1l_ref[...], -SWIGLU_LIMIT, SWIGLU_LIMIT)
        act = glu * _sigmoid(SWIGLU_ALPHA * glu) * (lin + 1.0)
        y_ref[...] = _pack_bf16_pair(_dot(act.astype(BF16), w2_s[...]) + b2_ref[...])

    @pl.when(j >= nused)
    def _():
        y_ref[...] = jnp.zeros_like(y_ref)


def _moe_experts(xs, blk_e, nused, w1, b1g, b1l, w2, b2, bm):
    rows, W = xs.shape
    nb = rows // bm
    D, F = w1.shape[1], w1.shape[2] // 2
    last = lambda j, be, nu: jnp.minimum(j, jnp.maximum(nu[0] - 1, 0))
    wspec = lambda k, n: pl.BlockSpec((None, k, n), lambda j, be, nu: (be[last(j, be, nu)], 0, 0))
    grid_spec = pltpu.PrefetchScalarGridSpec(
        num_scalar_prefetch=2,
        grid=(nb,),
        in_specs=[pl.BlockSpec((bm, W), lambda j, be, nu: (j, 0)),
                  wspec(D, 2 * F), wspec(1, F), wspec(1, F), wspec(F, D), wspec(1, D)],
        out_specs=pl.BlockSpec((bm, W), lambda j, be, nu: (j, 0)),
        scratch_shapes=[pltpu.VMEM((D, F), BF16), pltpu.VMEM((D, F), BF16), pltpu.VMEM((F, D), BF16)],
    )
    return pl.pallas_call(
        _moe_kernel,
        grid_spec=grid_spec,
        out_shape=jax.ShapeDtypeStruct((rows, W), U32),
        compiler_params=_params("arbitrary"),
        name="moe_experts",
    )(blk_e, nused, xs, w1, b1g, b1l, w2, b2)


def _combine_gather_start(dest_ref, y_hbm, buf, sem, slot, tm):
    for r in range(tm):
        for kx in range(TOP_K):
            d = dest_ref[0, 0, r * TOP_K + kx]
            pltpu.make_async_copy(y_hbm.at[pl.ds(d, 1)], buf.at[slot, kx, pl.ds(r, 1)], sem.at[slot]).start()


def _combine_kernel(dest_ref, destn_ref, gt_ref, h1_ref, g_ref, b_ref, y_hbm, o_hbm, buf, sem, obuf, osem,
                    *, dn_alpha, nt, skip):
    i = pl.program_id(0)
    n = pl.num_programs(0)
    tm = h1_ref.shape[0]
    slot = i % 2
    b = i // nt
    j = i % nt

    def out_wait(rows):
        pltpu.make_async_copy(obuf.at[0, pl.ds(0, rows)], o_hbm.at[0, pl.ds(0, rows)], osem.at[0]).wait()

    @pl.when(i == 0)
    def _():
        _combine_gather_start(dest_ref, y_hbm, buf, sem, 0, tm)

    for nxt in (0, 1):
        @pl.when(jnp.logical_and(i + 1 < n, slot == 1 - nxt))
        def _():
            _combine_gather_start(destn_ref, y_hbm, buf, sem, nxt, tm)

    for kx in range(TOP_K):
        pltpu.make_async_copy(y_hbm.at[pl.ds(0, tm)], buf.at[slot, kx], sem.at[slot]).wait()
    gt = gt_ref[...]
    lo, hi = _unpack_bf16_pair(buf[slot, 0])
    moe_lo, moe_hi = gt[:, 0:1] * lo, gt[:, 0:1] * hi
    for kx in range(1, TOP_K):
        lo, hi = _unpack_bf16_pair(buf[slot, kx])
        moe_lo = moe_lo + gt[:, kx:kx + 1] * lo
        moe_hi = moe_hi + gt[:, kx:kx + 1] * hi
    moe = jnp.concatenate([moe_lo, moe_hi], axis=1)
    obuf[slot] = _layer_norm(dn_alpha * h1_ref[...] + moe, g_ref[...], b_ref[...])

    @pl.when(jnp.logical_and(i > 0, (i - 1) % nt == 0))
    def _():
        out_wait(tm - skip)

    @pl.when(jnp.logical_and(i > 0, (i - 1) % nt != 0))
    def _():
        out_wait(tm)

    @pl.when(j == 0)
    def _():
        pltpu.make_async_copy(obuf.at[slot, pl.ds(skip, tm - skip)], o_hbm.at[b, pl.ds(0, tm - skip)],
                              osem.at[0]).start()

    @pl.when(j != 0)
    def _():
        start = pl.multiple_of(j * tm - skip, 8)
        pltpu.make_async_copy(obuf.at[slot], o_hbm.at[b, pl.ds(start, tm)], osem.at[0]).start()

    @pl.when(i == n - 1)
    def _():
        if nt == 1:
            out_wait(tm - skip)
        else:
            out_wait(tm)


def _moe_combine(dest, gate, h1, ln_g, ln_b, yb, dn_alpha, B, T, skip):
    N, D = h1.shape
    tm = _largest_tile(T, 384)
    nt = T // tm
    n = N // tm
    assert skip % 8 == 0 and skip < tm
    dest3 = dest.reshape(n, 1, tm * TOP_K)
    tile = lambda w: pl.BlockSpec((tm, w), lambda i: (i, 0))
    return pl.pallas_call(
        functools.partial(_combine_kernel, dn_alpha=dn_alpha, nt=nt, skip=skip),
        grid=(n,),
        in_specs=[
            pl.BlockSpec((1, 1, tm * TOP_K), lambda i: (i, 0, 0), memory_space=pltpu.SMEM),
            pl.BlockSpec((1, 1, tm * TOP_K), lambda i: (jnp.minimum(i + 1, n - 1), 0, 0), memory_space=pltpu.SMEM),
            tile(LANES), tile(D), _full((1, D)), _full((1, D)),
            pl.BlockSpec(memory_space=pl.ANY),
        ],
        out_specs=pl.BlockSpec(memory_space=pl.ANY),
        out_shape=jax.ShapeDtypeStruct((B, T - skip, D), F32),
        scratch_shapes=[pltpu.VMEM((2, TOP_K, tm, yb.shape[1]), U32), pltpu.SemaphoreType.DMA((2,)),
                        pltpu.VMEM((2, tm, D), F32), pltpu.SemaphoreType.DMA((1,))],
        compiler_params=_params("arbitrary"),
        name="moe_combine",
    )(dest3, dest3, gate, h1, ln_g, ln_b, yb)


MOE_BLOCK_ROWS = 512


def _route_tables(n_asg, counts, n_experts):
    bm = min(MOE_BLOCK_ROWS, max(8, 1 << int(math.log2(max(1, n_asg // n_experts)))))
    nb = -(-n_asg // bm) + n_experts
    padded = (counts + bm - 1) // bm * bm
    pends = jnp.cumsum(padded)
    starts = (pends - padded).astype(I32)
    blk_start = jnp.arange(nb, dtype=I32) * bm
    blk_e = jnp.minimum(jnp.sum(pends[None, :] <= blk_start[:, None], axis=1), n_experts - 1).astype(I32)
    nused = (pends[-1] // bm).astype(I32).reshape(1)
    n_pad = nb * bm - n_asg
    gap = padded - counts
    gap_end = jnp.cumsum(gap)
    i = jnp.arange(n_pad, dtype=I32)
    e = jnp.sum(gap_end[None, :] <= i[:, None], axis=1)
    ec = jnp.minimum(e, n_experts - 1)
    in_group = (pends - padded + counts)[ec] + i - (gap_end - gap)[ec]
    pad_slots = jnp.where(e < n_experts, in_group, pends[-1] + i - gap_end[-1]).astype(I32)
    return starts, pad_slots, blk_e, nused, bm, nb * bm


def _mixers(x, prefix, prev_row, s0, past_k, past_v, past_logf, wts):
    B, D = x.shape[0], x.shape[2]
    T = x.shape[1] + prefix.shape[0]
    h, q, k, v, rkv, gates, lmid, logf, rkv0 = _inproj(
        x, prefix, prev_row, wts['ln0_g'], wts['ln0_b'], wts['wm'], wts['wff'], wts['bff'], wts['mu3'],
        wts['w1'], wts['a1'], wts['g1'])
    if past_k is None:
        k_aug, v_bf = _fox_prep(logf, k, v, wts['sel_c'])
        fox = _fox_attention(q, k_aug, v_bf, wts['sel_q'], 0)
    else:
        P = past_k.shape[1]
        past_pad = jnp.pad(past_logf.astype(F32), ((0, 0), (0, 0), (0, LANES - N_HEADS)))
        k_aug, v_bf = _fox_prep(past_pad, past_k.reshape(B, P, HW), past_v.reshape(B, P, HW), wts['sel_c'],
                                new=(logf, k, v))
        fox = _fox_attention(q, k_aug, v_bf, wts['sel_q'], P)
    rw, s_fin = _rwkv(rkv, lmid, rkv0, s0, wts['mu_rkv'], wts['w0'], wts['w2'], wts['a0'], wts['a2'], wts['g2'],
                      wts['k_k'], wts['k_a'], wts['r_k'], wts['gn_g'], wts['gn_b'])
    N = B * T
    tokens = dict(fox=fox.reshape(N, HW), rw=rw.reshape(N, HW), gates=gates.reshape(N, 2 * D), h=h.reshape(N, D))
    new_k = k.reshape(B, T, N_HEADS, HEAD_DIM)
    new_v = v.reshape(B, T, N_HEADS, HEAD_DIM)
    return tokens, (new_k, new_v, logf[:, :, :N_HEADS], s_fin, h[:, T - 1:T, :])


def _merge_moe(streams, wts):
    dn_alpha, n_experts = wts['dn_alpha'], wts['n_experts']
    counts = jnp.zeros((1, LANES), I32)
    routed = []
    for tokens, B, T, y_skip in streams:
        h1, h1p, top_idx, gate, rank, counts = _merge_route(
            tokens['fox'], tokens['rw'], tokens['gates'], tokens['h'],
            wts['w_up_a'], wts['w_up_b'], wts['w_out'], wts['ln1_g'], wts['ln1_b'],
            wts['wr_hi'], wts['wr_lo'], wts['br'], counts, dn_alpha)
        routed.append((h1, h1p, top_idx[:, :TOP_K], gate, rank[:, :TOP_K]))
    n_asg = sum(r[0].shape[0] for r in routed) * TOP_K
    starts, pad_slots, blk_e, nused, bm, rows = _route_tables(n_asg, counts[0, :n_experts], n_experts)
    dests = [(starts[top_idx] + rank).astype(I32) for _, _, top_idx, _, rank in routed]
    zero_slots = jnp.concatenate([pad_slots] + [d.reshape(-1) for d in dests[1:]])
    xs = None
    for (h1, h1p, _, _, _), dest in zip(routed, dests):
        xs = _moe_dispatch(h1p, dest, zero_slots, rows, extend=xs)
    yb = _moe_experts(xs, blk_e, nused, wts['we1'], wts['b1g'], wts['b1l'], wts['we2'], wts['be2'], bm)
    return [_moe_combine(dest, gate, h1, wts['ln2_g'], wts['ln2_b'], yb, dn_alpha, B, T, y_skip)
            for (h1, _, _, gate, _), dest, (_, B, T, y_skip) in zip(routed, dests, streams)]


def kernel(x_prompt, x_sample, cache_fox_k, cache_fox_v, cache_fox_logf, state_rwkv, state_shift, meta, ln0_g, ln0_b, w_in, b_forget, mu_w, mu_a, mu_g, mu_rkv, w0, w1, w2, a0, a1, a2, g1, g2, k_k, k_a, r_k, gn_g, gn_b, w_up_a, w_up_b, w_out, ln1_g, ln1_b, w_router, b_router, w_e1, b_e1, w_e2, b_e2, ln2_g, ln2_b):
    depth, D, in_cols = w_in.shape
    assert depth == 1 and D == 1024 and in_cols == 6 * HW + N_HEADS + 2 * D
    n_experts = w_router.shape[2]
    assert n_experts <= LANES
    B = x_prompt.shape[0]
    l = 0
    w = w_in[l]
    off_ff = 3 * HW
    row = lambda a: a.reshape(1, -1).astype(F32)
    wr = jnp.pad(w_router[l], ((0, 0), (0, LANES - n_experts)))
    wr_hi = wr.astype(BF16)
    sel_q, sel_c = _aug_select_matrices()
    wts = dict(
        sel_q=sel_q, sel_c=sel_c,
        dn_alpha=float((2 * depth) ** 0.25), n_experts=n_experts,
        ln0_g=row(ln0_g), ln0_b=row(ln0_b),
        wm=jnp.concatenate([w[:, :off_ff], w[:, off_ff + N_HEADS:]], axis=1).astype(BF16),
        wff=jnp.pad(w[:, off_ff:off_ff + N_HEADS], ((0, 0), (0, LANES - N_HEADS))).astype(BF16),
        bff=jnp.pad(row(b_forget[l]), ((0, 0), (0, LANES - N_HEADS))),
        mu3=jnp.stack([mu_w[l], mu_a[l], mu_g[l]], axis=0),
        w1=w1[l].astype(BF16), a1=a1[l].astype(BF16), g1=g1[l].astype(BF16),
        mu_rkv=row(mu_rkv[l]), w0=row(w0[l]), w2=w2[l].astype(BF16), a0=row(a0[l]), a2=a2[l].astype(BF16),
        g2=g2[l].astype(BF16), k_k=row(k_k[l]), k_a=row(k_a[l]), r_k=row(r_k[l]), gn_g=row(gn_g[l]),
        gn_b=row(gn_b[l]),
        w_up_a=w_up_a[l].astype(BF16), w_up_b=w_up_b[l].astype(BF16), w_out=w_out[l].astype(BF16),
        ln1_g=row(ln1_g[l]), ln1_b=row(ln1_b[l]),
        wr_hi=wr_hi, wr_lo=(wr - wr_hi.astype(F32)).astype(BF16),
        br=jnp.pad(row(b_router[l]), ((0, 0), (0, LANES - n_experts)), constant_values=NEG_BIG),
        we1=w_e1[l],
        b1g=b_e1[l][:, None, 0::2], b1l=b_e1[l][:, None, 1::2],
        we2=w_e2[l], be2=b_e2[l][:, None, :],
        ln2_g=row(ln2_g[l]), ln2_b=row(ln2_b[l]),
    )
    zero_row = jnp.zeros((B, 1, D), F32)
    zero_state = jnp.zeros((B, N_HEADS, HEAD_DIM, HEAD_DIM), F32)
    no_prefix = jnp.zeros((0, D), x_sample.dtype)
    tok_p, (k_p, v_p, lf_p, s_p, sh_p) = _mixers(x_prompt, meta.astype(x_prompt.dtype), zero_row, zero_state,
                                                 None, None, None, wts)
    tok_s, (k_s, v_s, lf_s, s_s, sh_s) = _mixers(x_sample, no_prefix, state_shift[l], state_rwkv[l], cache_fox_k[l],
                                                 cache_fox_v[l], cache_fox_logf[l], wts)
    y_p, y_s = _merge_moe([(tok_p, B, x_prompt.shape[1] + N_META, N_META), (tok_s,) + x_sample.shape[:2] + (0,)],
                          wts)
    ex = lambda a: a[None]
    return (y_p, y_s, ex(k_p), ex(v_p), ex(lf_p), ex(s_p), ex(sh_p),
            ex(k_s), ex(v_s), ex(lf_s), ex(s_s), ex(sh_s))
```

```python
import functools
import math

import jax
import jax.numpy as jnp
import numpy as np
from jax import lax
from jax.experimental import pallas as pl
from jax.experimental.pallas import tpu as pltpu

F32 = jnp.float32
BF16 = jnp.bfloat16
I32 = jnp.int32
U32 = jnp.uint32

N_META = 16
HEAD_DIM = 64
N_HEADS = 8
HW = N_HEADS * HEAD_DIM
TOP_K = 4
SWIGLU_LIMIT = 7.0
SWIGLU_ALPHA = 1.702
LN_EPS = 1e-5
GN_EPS = 64e-5
LANES = 128
NEG_BIG = -1e30
VMEM_LIMIT_BYTES = 56 * 1024 * 1024

NT_DIMS = (((1,), (1,)), ((), ()))
TN_DIMS = (((0,), (0,)), ((), ()))


def _params(*sem):
    return pltpu.CompilerParams(dimension_semantics=sem, vmem_limit_bytes=VMEM_LIMIT_BYTES)


def _largest_tile(n, cap, mult=8):
    best = None
    for d in range(mult, min(n, cap) + 1, mult):
        if n % d == 0:
            best = d
    assert best is not None, (n, cap, mult)
    return best


def _sigmoid(x):
    return 1.0 / (1.0 + jnp.exp(-x))


def _softplus(x):
    return jnp.maximum(x, 0.0) + jnp.log1p(jnp.exp(-jnp.abs(x)))


def _layer_norm(x, g, b):
    mu = jnp.mean(x, -1, keepdims=True)
    xc = x - mu
    var = jnp.mean(xc * xc, -1, keepdims=True)
    return xc * lax.rsqrt(var + LN_EPS) * g + b


def _dot(a, b):
    return jnp.dot(a, b, preferred_element_type=F32)


def _full(shape):
    n = len(shape)
    return pl.BlockSpec(shape, lambda *_: (0,) * n)


def _pack_bf16_pair(x):
    w = x.shape[1] // 2
    bits = lambda t: lax.bitcast_convert_type(t.astype(BF16).astype(F32), U32)
    return (bits(x[:, :w]) >> 16) | (bits(x[:, w:]) & jnp.uint32(0xFFFF0000))


def _unpack_bf16_pair(u):
    return lax.bitcast_convert_type(u << 16, F32), lax.bitcast_convert_type(u & jnp.uint32(0xFFFF0000), F32)


C_Q, C_K, C_V, C_RKV, C_GA, C_END = 0, HW, 2 * HW, 3 * HW, 6 * HW, 6 * HW + 2048


def _inproj_kernel(x_ref, pre_ref, prev_ref, g_ref, b_ref, wm_ref, wff_ref, bff_ref, mu_ref, w1_ref, a1_ref, g1_ref,
                   h_ref, q_ref, k_ref, v_ref, rkv_ref, gate_ref, lmid_ref, logf_ref, rkv0_ref,
                   carry_ref, *, n_pre):
    t = pl.program_id(1)
    x = x_ref[0] if n_pre else x_ref[...]
    tt = x.shape[0]
    if n_pre:
        x = jnp.where(t == 0, jnp.concatenate([pre_ref[...], x[0:tt - n_pre]], axis=0), x)
    h = _layer_norm(x, g_ref[...], b_ref[...])
    h_ref[...] = h

    @pl.when(t == 0)
    def _():
        prev = prev_ref[...]
        carry_ref[...] = prev
        p8 = jnp.broadcast_to(prev, (8, prev.shape[1])).astype(BF16)
        rkv0_ref[...] = _dot(p8, wm_ref[:, C_RKV:C_GA])[0:1]

    rows = lax.broadcasted_iota(I32, h.shape, 0)
    hprev = jnp.where(rows == 0, carry_ref[...], pltpu.roll(h, 1, axis=0))
    carry_ref[...] = h[tt - 1:tt, :]
    dx = hprev - h
    hb = h.astype(BF16)
    q_ref[...] = _dot(hb, wm_ref[:, C_Q:C_K]).astype(BF16)
    k_ref[...] = _dot(hb, wm_ref[:, C_K:C_V])
    v_ref[...] = _dot(hb, wm_ref[:, C_V:C_RKV])
    rkv_ref[...] = _dot(hb, wm_ref[:, C_RKV:C_GA]).astype(BF16)
    gate_ref[...] = _sigmoid(_dot(hb, wm_ref[:, C_GA:C_END])).astype(BF16)
    ff = _dot(hb, wff_ref[...]) + bff_ref[...]
    logf_ref[...] = -_softplus(-ff)
    mu = mu_ref[...]
    lmid_ref[:, 0:64] = _dot((h + dx * mu[0:1]).astype(BF16), w1_ref[...])
    lmid_ref[:, 64:128] = _dot((h + dx * mu[1:2]).astype(BF16), a1_ref[...])
    lmid_ref[:, 128:256] = _dot((h + dx * mu[2:3]).astype(BF16), g1_ref[...])


def _inproj(x, prefix, prev_row, ln_g, ln_b, wm, wff, bff, mu3, w1, a1, g1):
    B, Tx, D = x.shape
    n_pre = prefix.shape[0]
    T = Tx + n_pre
    tt = _largest_tile(T, 384)
    nt = T // tt
    assert n_pre % 8 == 0 and n_pre < tt
    tile = lambda w: pl.BlockSpec((None, tt, w), lambda b, t: (b, t, 0))
    if n_pre:
        x_spec = pl.BlockSpec((pl.Element(1), pl.Element(tt), pl.Element(D)),
                              lambda b, t: (b, pl.multiple_of(jnp.maximum(t * tt - n_pre, 0), 8), 0))
        pre_in = prefix
    else:
        x_spec = tile(D)
        pre_in = jnp.zeros((8, D), x.dtype)
    row = lambda w: pl.BlockSpec((None, 1, w), lambda b, t: (b, 0, 0))
    out_shape = [
        jax.ShapeDtypeStruct((B, T, D), F32),
        jax.ShapeDtypeStruct((B, T, HW), BF16),
        jax.ShapeDtypeStruct((B, T, HW), F32),
        jax.ShapeDtypeStruct((B, T, HW), F32),
        jax.ShapeDtypeStruct((B, T, 3 * HW), BF16),
        jax.ShapeDtypeStruct((B, T, 2 * D), BF16),
        jax.ShapeDtypeStruct((B, T, 256), F32),
        jax.ShapeDtypeStruct((B, T, LANES), F32),
        jax.ShapeDtypeStruct((B, 1, 3 * HW), F32),
    ]
    return pl.pallas_call(
        functools.partial(_inproj_kernel, n_pre=n_pre),
        grid=(B, nt),
        in_specs=[x_spec, _full(pre_in.shape), row(D), _full((1, D)), _full((1, D)), _full(wm.shape), _full(wff.shape),
                  _full(bff.shape), _full(mu3.shape), _full(w1.shape), _full(a1.shape), _full(g1.shape)],
        out_specs=[tile(D), tile(HW), tile(HW), tile(HW), tile(3 * HW), tile(2 * D), tile(256), tile(LANES),
                   row(3 * HW)],
        out_shape=out_shape,
        scratch_shapes=[pltpu.VMEM((1, D), F32)],
        compiler_params=_params("arbitrary", "arbitrary"),
        name="inproj",
    )(x, pre_in, prev_row, ln_g, ln_b, wm, wff, bff, mu3, w1, a1, g1)


HEAD_PAD = 2 * HEAD_DIM
C_SPLIT = 3


def _aug_select_matrices():
    rows = np.arange(HW)
    sel_q = np.zeros((HW, N_HEADS * HEAD_PAD), np.float32)
    sel_q[rows, (rows // HEAD_DIM) * HEAD_PAD + rows % HEAD_DIM] = HEAD_DIM ** -0.5
    p = np.repeat(np.arange(C_SPLIT), N_HEADS)
    h = np.tile(np.arange(N_HEADS), C_SPLIT)
    sel_c = np.zeros((LANES, N_HEADS * HEAD_PAD), np.float32)
    sel_c[p * N_HEADS + h, h * HEAD_PAD + HEAD_DIM + p] = 1.0
    return jnp.asarray(sel_q, BF16), jnp.asarray(sel_c, BF16)


def _split3(x):
    hi = x.astype(BF16)
    r1 = x - hi.astype(F32)
    mid = r1.astype(BF16)
    return hi, mid, (r1 - mid.astype(F32)).astype(BF16)


def _fox_prep_kernel(lf_ref, k_ref, v_ref, *rest, n_new):
    if n_new:
        nlf_ref, nk_ref, nv_ref, selc_ref, ka_ref, vt_ref, carry_ref = rest
    else:
        selc_ref, ka_ref, vt_ref, carry_ref = rest
    t = pl.program_id(1)
    last = t == pl.num_programs(1) - 1

    def rows_of(ref, new_ref):
        if not n_new:
            return ref[...]
        x = ref[0]
        return jnp.where(last, jnp.concatenate([x[n_new:], new_ref[...]], axis=0), x)

    lf_in = rows_of(lf_ref, nlf_ref if n_new else None)
    k_in = rows_of(k_ref, nk_ref if n_new else None)
    v_in = rows_of(v_ref, nv_ref if n_new else None)
    tt = lf_in.shape[0]

    @pl.when(t == 0)
    def _():
        carry_ref[...] = jnp.zeros_like(carry_ref)

    r = lax.broadcasted_iota(I32, (tt, tt), 0)
    c = lax.broadcasted_iota(I32, (tt, tt), 1)
    tri = jnp.where(r >= c, 1.0, 0.0).astype(BF16)
    cs3 = _dot(tri, jnp.concatenate(_split3(lf_in), axis=1))
    cs = cs3[:, 0:LANES] + cs3[:, LANES:2 * LANES] + cs3[:, 2 * LANES:3 * LANES] + carry_ref[...]
    carry_ref[...] = cs[tt - 1:tt, :]
    hi, mid, lo = _split3(-cs)
    is_head = lax.broadcasted_iota(I32, (tt, LANES), 1) < N_HEADS
    keep = lambda part: jnp.where(is_head, part.astype(F32), 0.0)
    packed = keep(hi) + pltpu.roll(keep(mid), N_HEADS, axis=1) + pltpu.roll(keep(lo), 2 * N_HEADS, axis=1)
    kc = _dot(packed.astype(BF16), selc_ref[...])
    k = k_in
    pad = jnp.zeros((tt, HEAD_PAD - HEAD_DIM), F32)
    for h in range(N_HEADS):
        hp = slice(h * HEAD_PAD, (h + 1) * HEAD_PAD)
        ka_ref[:, hp] = (jnp.concatenate([k[:, h * HEAD_DIM:(h + 1) * HEAD_DIM], pad], axis=1) + kc[:, hp]).astype(BF16)
    ii = lax.broadcasted_iota(I32, (HW, HW), 0)
    jj = lax.broadcasted_iota(I32, (HW, HW), 1)
    eye = jnp.where(ii == jj, 1.0, 0.0).astype(BF16)
    vt_ref[...] = lax.dot_general(eye, v_in.astype(BF16), NT_DIMS, preferred_element_type=F32).astype(BF16)


def _fox_prep(logf, k, v, sel_c, new=None):
    B, P, _ = k.shape
    n_new = 0 if new is None else new[1].shape[1]
    T = P + n_new
    tt = _largest_tile(T, 384)
    nt = T // tt
    tile = lambda w: pl.BlockSpec((None, tt, w), lambda b, t: (b, t, 0))
    if n_new:
        assert n_new % 8 == 0 and n_new < tt <= P and (P - tt) % 8 == 0
        win = lambda w: pl.BlockSpec((pl.Element(1), pl.Element(tt), pl.Element(w)),
                                     lambda b, t: (b, pl.multiple_of(jnp.minimum(t * tt, P - tt), 8), 0))
        fresh = lambda w: pl.BlockSpec((None, n_new, w), lambda b, t: (b, 0, 0))
        in_specs = [win(LANES), win(HW), win(HW), fresh(LANES), fresh(HW), fresh(HW), _full(sel_c.shape)]
        operands = (logf, k, v) + tuple(new) + (sel_c,)
    else:
        in_specs = [tile(LANES), tile(HW), tile(HW), _full(sel_c.shape)]
        operands = (logf, k, v, sel_c)
    return pl.pallas_call(
        functools.partial(_fox_prep_kernel, n_new=n_new),
        grid=(B, nt),
        in_specs=in_specs,
        out_specs=[tile(N_HEADS * HEAD_PAD), pl.BlockSpec((None, None, HW, tt), lambda b, t: (b, t, 0, 0))],
        out_shape=[jax.ShapeDtypeStruct((B, T, N_HEADS * HEAD_PAD), BF16),
                   jax.ShapeDtypeStruct((B, nt, HW, tt), BF16)],
        scratch_shapes=[pltpu.VMEM((1, LANES), F32)],
        compiler_params=_params("arbitrary", "arbitrary"),
        name="fox_prep",
    )(*operands)


def _fox_kernel(qi_ref, ki_ref, last_ref, q_ref, ka_ref, vt_ref, selq_ref, o_ref, qa_ref, m_ref, l_ref, acc_ref,
                *, q0, tq, tk):
    p = pl.program_id(1)
    qi = qi_ref[p]
    ki = ki_ref[p]

    @pl.when(ki == 0)
    def _():
        m_ref[...] = jnp.full_like(m_ref, NEG_BIG)
        l_ref[...] = jnp.zeros_like(l_ref)
        acc_ref[...] = jnp.zeros_like(acc_ref)
        lane = lax.broadcasted_iota(I32, qa_ref.shape, 1) % HEAD_PAD
        ones = jnp.where(jnp.logical_and(lane >= HEAD_DIM, lane < HEAD_DIM + C_SPLIT), 1.0, 0.0)
        qa_ref[...] = (_dot(q_ref[...], selq_ref[...]) + ones).astype(BF16)

    first_q = q0 + qi * tq
    tile_first = ki * tk
    tile_last = tile_first + tk - 1

    def scores(h):
        hp = slice(h * HEAD_PAD, (h + 1) * HEAD_PAD)
        return lax.dot_general(ka_ref[:, hp], qa_ref[:, hp], NT_DIMS, preferred_element_type=F32)

    def tile_update(masked):
        if masked:
            key_pos = tile_first + lax.broadcasted_iota(I32, (tk, tq), 0)
            qry_pos = first_q + lax.broadcasted_iota(I32, (tk, tq), 1)
            bias = jnp.where(qry_pos >= key_pos, 0.0, NEG_BIG)
        m_all = m_ref[...]
        l_all = l_ref[...]
        m_rows, l_rows = [], []
        s_next = scores(0)
        for h in range(N_HEADS):
            hs = slice(h * HEAD_DIM, (h + 1) * HEAD_DIM)
            s = s_next
            if h + 1 < N_HEADS:
                s_next = scores(h + 1)
            if masked:
                s = s + bias
            m_prev = m_all[h:h + 1, :]
            m_new = jnp.maximum(m_prev, jnp.max(s, 0, keepdims=True))
            alpha = jnp.exp(m_prev - m_new)
            p = jnp.exp(s - m_new)
            l_rows.append(alpha * l_all[h:h + 1, :] + jnp.sum(p, 0, keepdims=True))
            m_rows.append(m_new)
            acc_ref[hs, :] = alpha * acc_ref[hs, :] + _dot(vt_ref[hs, :], p.astype(BF16))
        m_ref[...] = jnp.concatenate(m_rows, axis=0)
        l_ref[...] = jnp.concatenate(l_rows, axis=0)

    @pl.when(jnp.logical_and(tile_first <= first_q + tq - 1, tile_last > first_q))
    def _():
        tile_update(True)

    @pl.when(tile_last <= first_q)
    def _():
        tile_update(False)

    @pl.when(last_ref[p] == 1)
    def _():
        on = jnp.concatenate(
            [acc_ref[h * HEAD_DIM:(h + 1) * HEAD_DIM, :] / l_ref[h:h + 1, :] for h in range(N_HEADS)], axis=0)
        eye = jnp.where(lax.broadcasted_iota(I32, (tq, tq), 0) == lax.broadcasted_iota(I32, (tq, tq), 1),
                        1.0, 0.0).astype(BF16)
        o_ref[...] = lax.dot_general(eye, on.astype(BF16), NT_DIMS, preferred_element_type=F32).astype(o_ref.dtype)


def _fox_attention(q, k_aug, v_t, sel_q, q0):
    B, Tq, _ = q.shape
    Tk = k_aug.shape[1]
    WA = N_HEADS * HEAD_PAD
    tq = _largest_tile(Tq, 384)
    nk, tk = v_t.shape[1], v_t.shape[3]
    assert nk * tk == Tk
    nq = Tq // tq
    pairs = [(qi, ki) for qi in range(nq) for ki in range(min((q0 + (qi + 1) * tq - 1) // tk, nk - 1) + 1)]
    qi_tab = jnp.array([p[0] for p in pairs], I32)
    ki_tab = jnp.array([p[1] for p in pairs], I32)
    last_tab = jnp.array([int(i + 1 == len(pairs) or pairs[i + 1][0] != pairs[i][0]) for i in range(len(pairs))], I32)
    qspec = lambda w: pl.BlockSpec((None, tq, w), lambda b, p, qt, kt, lt: (b, qt[p], 0))
    grid_spec = pltpu.PrefetchScalarGridSpec(
        num_scalar_prefetch=3,
        grid=(B, len(pairs)),
        in_specs=[qspec(HW),
                  pl.BlockSpec((None, tk, WA), lambda b, p, qt, kt, lt: (b, kt[p], 0)),
                  pl.BlockSpec((None, None, HW, tk), lambda b, p, qt, kt, lt: (b, kt[p], 0, 0)),
                  pl.BlockSpec(sel_q.shape, lambda b, p, qt, kt, lt: (0, 0))],
        out_specs=qspec(HW),
        scratch_shapes=[pltpu.VMEM((tq, WA), BF16), pltpu.VMEM((N_HEADS, tq), F32),
                        pltpu.VMEM((N_HEADS, tq), F32), pltpu.VMEM((HW, tq), F32)],
    )
    return pl.pallas_call(
        functools.partial(_fox_kernel, q0=q0, tq=tq, tk=tk),
        grid_spec=grid_spec,
        out_shape=jax.ShapeDtypeStruct((B, Tq, HW), BF16),
        compiler_params=_params("arbitrary", "arbitrary"),
        name="fox_attention",
    )(qi_tab, ki_tab, last_tab, q, k_aug, v_t, sel_q)


def _rwkv_kernel(rkv_ref, lmid_ref, rkv0_ref, s0_ref, mu_ref, w0_ref, w2_ref, a0_ref, a2_ref, g2_ref,
                 kk_ref, ka_ref, rk_ref, gng_ref, gnb_ref, o_ref, sfin_ref, state_ref, carry_ref, *, chunk, levels):
    C = chunk
    NB, T = rkv_ref.shape[0], rkv_ref.shape[1]
    state_ref[...] = s0_ref[...]
    carry_ref[...] = rkv0_ref[...]
    row_w = lax.broadcasted_iota(I32, (C, 3 * HW), 0)
    row_h = lax.broadcasted_iota(I32, (C, HW), 0)
    r_i = lax.broadcasted_iota(I32, (C, C), 0)
    c_i = lax.broadcasted_iota(I32, (C, C), 1)
    strict = r_i > c_i
    incl = r_i >= c_i
    mid = C // 2 - 1 if C > 1 else 0
    hsl = [slice(h * HEAD_DIM, (h + 1) * HEAD_DIM) for h in range(N_HEADS)]
    nt = lambda x, y: lax.dot_general(x, y, NT_DIMS, preferred_element_type=F32)
    tn = lambda x, y: lax.dot_general(x, y, TN_DIMS, preferred_element_type=F32)

    def row_inputs(bb, off):
        x = rkv_ref[bb, pl.ds(off, C), :].astype(F32)
        prev = jnp.where(row_w == 0, carry_ref[bb], pltpu.roll(x, 1, axis=0))
        carry_ref[bb] = x[C - 1:C, :]
        x = x + (prev - x) * mu_ref[...]
        r, k0, v = x[:, 0:HW], x[:, HW:2 * HW], x[:, 2 * HW:3 * HW]
        lm = lmid_ref[bb, pl.ds(off, C), :]
        w_pre = w0_ref[...] + _dot(jnp.tanh(lm[:, 0:64]).astype(BF16), w2_ref[...])
        a = _sigmoid(a0_ref[...] + _dot(lm[:, 64:128].astype(BF16), a2_ref[...]))
        g = _dot(_sigmoid(lm[:, 128:256]).astype(BF16), g2_ref[...])
        w_log = -_softplus(-w_pre) - 0.5
        logdec = -jnp.exp(w_log)
        L = logdec
        sh = 1
        while sh < C:
            L = L + jnp.where(row_h >= sh, pltpu.roll(L, sh, axis=0), 0.0)
            sh *= 2
        l_mid = L[mid:mid + 1, :]
        l_tot = L[C - 1:C, :]
        return dict(r=r, v=v, a=a, g=g, kk_raw=k0 * kk_ref[...], k=k0 * (1.0 + (a - 1.0) * ka_ref[...]),
                    e_a=jnp.exp(L - logdec - l_mid), e_r=jnp.exp(L - l_mid), e_k=jnp.exp(l_mid - L),
                    e_s=jnp.exp(l_tot - L), w_tot=jnp.exp(l_tot), e_mid=jnp.exp(l_mid))

    gap = jnp.zeros((C, LANES - C), F32)
    gap2 = jnp.zeros((LANES - C, HEAD_DIM), F32)
    lane_pair = lambda left, right: jnp.concatenate([left, gap, right], axis=1)

    def chunk_body(i, carry):
        off = pl.multiple_of(i * C, C)
        rows = [row_inputs(bb, off) for bb in range(NB)]
        rk = rk_ref[...]
        chains = [(bb, h) for bb in range(NB) for h in range(N_HEADS)]
        X = range(len(chains))
        col = lambda name: [rows[bb][name][:, hsl[h]] for bb, h in chains]
        kkh = [x * lax.rsqrt(jnp.maximum(jnp.sum(x * x, -1, keepdims=True), 1e-24)) for x in col('kk_raw')]
        r_h, k_h, v_h, a_h = col('r'), col('k'), col('v'), col('a')
        e_a, e_r, e_k, e_s, w_tot, e_mid = col('e_a'), col('e_r'), col('e_k'), col('e_s'), col('w_tot'), col('e_mid')
        b_h = [kkh[c] * a_h[c] for c in X]
        ar = [jnp.concatenate([-kkh[c] * e_a[c], r_h[c] * e_r[c]], axis=0) for c in X]
        bkd = [jnp.concatenate([b_h[c] * e_k[c], gap2, k_h[c] * e_k[c]], axis=0) for c in X]
        s_old = [state_ref[bb, h] for bb, h in chains]
        gram = [nt(ar[c], bkd[c]) for c in X]
        x0 = [nt(ar[c], s_old[c] * e_mid[c]) for c in X]
        g_b = [gram[c][:, 0:C] for c in X]
        g_k = [gram[c][:, LANES:LANES + C] for c in X]
        a_mat = [jnp.where(strict, g_b[c][0:C], 0.0) for c in X]
        kv = [_dot(jnp.concatenate([jnp.where(strict, g_k[c][0:C], 0.0), jnp.where(incl, g_k[c][C:2 * C], 0.0)],
                                   axis=0), v_h[c]) for c in X]
        u = [x0[c][0:C] + kv[c][0:C] for c in X]
        for lvl in range(levels):
            if lvl + 1 < levels:
                prod = [_dot(a_mat[c], lane_pair(a_mat[c], u[c])) for c in X]
                a_mat = [prod[c][:, 0:C] for c in X]
                u = [u[c] + prod[c][:, LANES:LANES + HEAD_DIM] for c in X]
            else:
                u = [u[c] + _dot(a_mat[c], u[c]) for c in X]
        y = [x0[c][C:2 * C] + _dot(jnp.where(incl, g_b[c][C:2 * C], 0.0), u[c]) + kv[c][C:2 * C] for c in X]
        for c, (bb, h) in enumerate(chains):
            uv = jnp.concatenate([u[c], v_h[c]], axis=0)
            bks = jnp.concatenate([b_h[c] * e_s[c], k_h[c] * e_s[c]], axis=0)
            state_ref[bb, h] = s_old[c] * w_tot[c] + tn(uv, bks)
        outs = []
        for c, (bb, h) in enumerate(chains):
            mu = jnp.mean(y[c], -1, keepdims=True)
            yc = y[c] - mu
            var = jnp.mean(yc * yc, -1, keepdims=True)
            bonus = jnp.sum(r_h[c] * k_h[c] * rk[:, hsl[h]], -1, keepdims=True) * v_h[c]
            outs.append((yc * lax.rsqrt(var + GN_EPS), bonus))
        for bb in range(NB):
            mine = outs[bb * N_HEADS:(bb + 1) * N_HEADS]
            yn = jnp.concatenate([o[0] for o in mine], axis=1)
            bonus = jnp.concatenate([o[1] for o in mine], axis=1)
            out = (yn * gng_ref[...] + gnb_ref[...] + bonus) * rows[bb]['g']
            o_ref[bb, pl.ds(off, C), :] = out.astype(o_ref.dtype)
        return carry

    lax.fori_loop(0, T // C, chunk_body, 0)
    sfin_ref[...] = state_ref[...]


RWKV_ROWS_PER_STEP = 2


def _rwkv(rkv, lmid, rkv0, s0, mu_rkv, w0, w2, a0, a2, g2, k_k, k_a, r_k, gn_g, gn_b):
    B, T, _ = rkv.shape
    nb = RWKV_ROWS_PER_STEP if B % RWKV_ROWS_PER_STEP == 0 else 1
    chunk = _largest_tile(T, 64, mult=16)
    levels = max(1, math.ceil(math.log2(chunk)))
    seq =lambda w: pl.BlockSpec((nb, T, w), lambda b: (b, 0, 0))
    st = pl.BlockSpec((nb, N_HEADS, HEAD_DIM, HEAD_DIM), lambda b: (b, 0, 0, 0))
    vec = lambda a: _full(a.shape)
    params = (mu_rkv, w0, w2, a0, a2, g2, k_k, k_a, r_k, gn_g, gn_b)
    return pl.pallas_call(
        functools.partial(_rwkv_kernel, chunk=chunk, levels=levels),
        grid=(B // nb,),
        in_specs=[seq(3 * HW), seq(256), pl.BlockSpec((nb, 1, 3 * HW), lambda b: (b, 0, 0)), st]
                 + [vec(p) for p in params],
        out_specs=[seq(HW), st],
        out_shape=[jax.ShapeDtypeStruct((B, T, HW), BF16),
                   jax.ShapeDtypeStruct((B, N_HEADS, HEAD_DIM, HEAD_DIM), F32)],
        scratch_shapes=[pltpu.VMEM((nb, N_HEADS, HEAD_DIM, HEAD_DIM), F32), pltpu.VMEM((nb, 1, 3 * HW), F32)],
        compiler_params=_params("arbitrary"),
        name="rwkv7",
    )(rkv, lmid, rkv0, s0, *params)


def _merge_kernel(fox_ref, rw_ref, gate_ref, h_ref, wa_ref, wb_ref, wo_ref, g_ref, b_ref, wrh_ref, wrl_ref, br_ref,
                  cnt0_ref, h1_ref, h1p_ref, idx_ref, gt_ref, rank_ref, cnt_ref, carry_ref, *, dn_alpha):
    i = pl.program_id(0)
    tm, D = h_ref.shape

    @pl.when(i == 0)
    def _():
        carry_ref[...] = cnt0_ref[...].astype(F32)

    gates = gate_ref[...].astype(F32)
    merged = gates[:, 0:D] * _dot(fox_ref[...], wa_ref[...]) + gates[:, D:2 * D] * _dot(rw_ref[...], wb_ref[...])
    z = dn_alpha * h_ref[...] + _dot(merged.astype(BF16), wo_ref[...])
    h1 = _layer_norm(z, g_ref[...], b_ref[...])
    h1_ref[...] = h1
    h1p_ref[...] = _pack_bf16_pair(h1)
    hi = h1.astype(BF16)
    lo = (h1 - hi.astype(F32)).astype(BF16)
    hw = _dot(hi, jnp.concatenate([wrh_ref[...], wrl_ref[...]], axis=1))
    logits = hw[:, 0:LANES] + hw[:, LANES:2 * LANES] + _dot(lo, wrh_ref[...]) + br_ref[...]
    lane = lax.broadcasted_iota(I32, (tm, LANES), 1)
    lane_f = lane.astype(F32)
    cur = logits
    vals, idxs = [], []
    for _ in range(TOP_K):
        m = jnp.max(cur, -1, keepdims=True)
        ix = jnp.min(jnp.where(cur == m, lane_f, float(LANES)), -1, keepdims=True)
        vals.append(m)
        idxs.append(ix)
        cur = jnp.where(lane_f == ix, -3e38, cur)
    exps = [jnp.exp(vk - vals[0]) for vk in vals]
    denom = exps[0] + exps[1] + exps[2] + exps[3]
    onehot = jnp.zeros((tm, LANES), F32)
    for ix in idxs:
        onehot = onehot + jnp.where(lane_f == ix, 1.0, 0.0)
    r_i = lax.broadcasted_iota(I32, (tm, tm), 0)
    c_i = lax.broadcasted_iota(I32, (tm, tm), 1)
    tri = jnp.where(r_i > c_i, 1.0, 0.0).astype(BF16)
    before = _dot(tri, onehot.astype(BF16)) + carry_ref[...]
    idx_out = jnp.zeros((tm, LANES), F32)
    gt_out = jnp.zeros((tm, LANES), F32)
    rank_out = jnp.zeros((tm, LANES), F32)
    for kx in range(TOP_K):
        rank_k = jnp.sum(jnp.where(lane_f == idxs[kx], before, 0.0), -1, keepdims=True)
        idx_out = jnp.where(lane == kx, idxs[kx], idx_out)
        gt_out = jnp.where(lane == kx, exps[kx] / denom, gt_out)
        rank_out = jnp.where(lane == kx, rank_k, rank_out)
    idx_ref[...] = idx_out.astype(I32)
    gt_ref[...] = gt_out
    rank_ref[...] = rank_out.astype(I32)
    total = carry_ref[...] + jnp.sum(onehot, 0, keepdims=True)
    carry_ref[...] = total
    cnt_ref[...] = total.astype(I32)


def _merge_route(fox, rw, gates, h, wa, wb, wo, ln_g, ln_b, wr_hi, wr_lo, br, counts_before, dn_alpha):
    N, D = h.shape
    tm = _largest_tile(N, 384)
    tile = lambda w: pl.BlockSpec((tm, w), lambda i: (i, 0))
    return pl.pallas_call(
        functools.partial(_merge_kernel, dn_alpha=dn_alpha),
        grid=(N // tm,),
        in_specs=[tile(HW), tile(HW), tile(2 * D), tile(D), _full(wa.shape), _full(wb.shape), _full(wo.shape),
                  _full((1, D)), _full((1, D)), _full(wr_hi.shape), _full(wr_lo.shape), _full(br.shape),
                  _full((1, LANES))],
        out_specs=[tile(D), tile(D // 2), tile(LANES), tile(LANES), tile(LANES), _full((1, LANES))],
        out_shape=[jax.ShapeDtypeStruct((N, D), F32), jax.ShapeDtypeStruct((N, D // 2), U32),
                   jax.ShapeDtypeStruct((N, LANES), I32),
                   jax.ShapeDtypeStruct((N, LANES), F32), jax.ShapeDtypeStruct((N, LANES), I32),
                   jax.ShapeDtypeStruct((1, LANES), I32)],
        scratch_shapes=[pltpu.VMEM((1, LANES), F32)],
        compiler_params=_params("arbitrary"),
        name="merge_route",
    )(fox, rw, gates, h, wa, wb, wo, ln_g, ln_b, wr_hi, wr_lo, br, counts_before)


PERM_W = 256


def _deinterleave_to_bf16(w_ref, g_ref, l_ref):
    half = PERM_W // 2
    ii = lax.broadcasted_iota(I32, (PERM_W, PERM_W), 0)
    jj = lax.broadcasted_iota(I32, (PERM_W, PERM_W), 1)
    src = jnp.where(jj < half, 2 * jj, 2 * (jj - half) + 1)
    perm = jnp.where(ii == src, 1.0, 0.0).astype(BF16)
    for c in range(w_ref.shape[1] // PERM_W):
        w = w_ref[:, c * PERM_W:(c + 1) * PERM_W].astype(BF16)
        out = _dot(w, perm)
        g_ref[:, c * half:(c + 1) * half] = out[:, :half].astype(BF16)
        l_ref[:, c * half:(c + 1) * half] = out[:, half:].astype(BF16)


def _dispatch_kernel(pad_ref, dest_ref, x_ref, *rest, n_pad):
    xs_hbm, zero_ref, sem = rest[-3:]
    i = pl.program_id(0)
    tm = x_ref.shape[0]

    if n_pad:
        @pl.when(i == 0)
        def _():
            zero_ref[...] = jnp.zeros_like(zero_ref)

            def zbody(r, c):
                pltpu.make_async_copy(zero_ref.at[pl.ds(0, 1)], xs_hbm.at[pl.ds(pad_ref[r], 1)], sem.at[1]).start()
                return c
            lax.fori_loop(0, n_pad, zbody, 0, unroll=8)
            for _ in range(n_pad // tm):
                pltpu.make_async_copy(zero_ref, xs_hbm.at[pl.ds(0, tm)], sem.at[1]).wait()
            if n_pad % tm:
                pltpu.make_async_copy(zero_ref.at[pl.ds(0, n_pad % tm)], xs_hbm.at[pl.ds(0, n_pad % tm)],
                                      sem.at[1]).wait()

    for r in range(tm):
        for kx in range(TOP_K):
            d = dest_ref[0, 0, r * TOP_K + kx]
            pltpu.make_async_copy(x_ref.at[pl.ds(r, 1)], xs_hbm.at[pl.ds(d, 1)], sem.at[0]).start()
    for _ in range(TOP_K):
        pltpu.make_async_copy(x_ref, xs_hbm.at[pl.ds(0, tm)], sem.at[0]).wait()


def _moe_dispatch(xp, dest, pad_slots, rows, extend=None):
    N, W = xp.shape
    tm = _largest_tile(N, 256)
    n = N // tm
    n_pad = 0 if extend is not None else pad_slots.shape[0]
    in_specs = [pl.BlockSpec((1, 1, tm * TOP_K), lambda i, pad: (i, 0, 0), memory_space=pltpu.SMEM),
                pl.BlockSpec((tm, W), lambda i, pad: (i, 0))]
    operands = [pad_slots, dest.reshape(n, 1, tm * TOP_K), xp]
    aliases = {}
    if extend is not None:
        in_specs.append(pl.BlockSpec(memory_space=pl.ANY))
        operands.append(extend)
        aliases = {len(operands) - 1: 0}
    grid_spec = pltpu.PrefetchScalarGridSpec(
        num_scalar_prefetch=1,
        grid=(n,),
        in_specs=in_specs,
        out_specs=pl.BlockSpec(memory_space=pl.ANY),
        scratch_shapes=[pltpu.VMEM((tm, W), U32), pltpu.SemaphoreType.DMA((2,))],
    )
    return pl.pallas_call(
        functools.partial(_dispatch_kernel, n_pad=n_pad),
        grid_spec=grid_spec,
        out_shape=jax.ShapeDtypeStruct((rows, W), U32),
        input_output_aliases=aliases,
        compiler_params=_params("arbitrary"),
        name="moe_dispatch",
    )(*operands)


def _moe_kernel(be_ref, nused_ref, xs_ref, w1_ref, b1g_ref, b1l_ref, w2_ref, b2_ref, y_ref, w1g_s, w1l_s, w2_s):
    j = pl.program_id(0)
    nused = nused_ref[0]
    last = jnp.maximum(nused - 1, 0)
    e_now = be_ref[jnp.minimum(j, last)]
    e_before = be_ref[jnp.minimum(jnp.maximum(j - 1, 0), last)]

    @pl.when(jnp.logical_or(j == 0, e_now != e_before))
    def _():
        _deinterleave_to_bf16(w1_ref, w1g_s, w1l_s)
        w2_s[...] = w2_ref[...].astype(BF16)

    @pl.when(j < nused)
    def _():
        lo, hi = _unpack_bf16_pair(xs_ref[...])
        x = jnp.concatenate([lo, hi], axis=1).astype(BF16)
        glu = jnp.minimum(_dot(x, w1g_s[...]) + b1g_ref[...], SWIGLU_LIMIT)
        lin = jnp.clip(_dot(x, w1l_s[...]) + b1l_ref[...], -SWIGLU_LIMIT, SWIGLU_LIMIT)
        act = glu * _sigmoid(SWIGLU_ALPHA * glu) * (lin + 1.0)
        y_ref[...] = _pack_bf16_pair(_dot(act.astype(BF16), w2_s[...]) + b2_ref[...])

    @pl.when(j >= nused)
    def _():
        y_ref[...] = jnp.zeros_like(y_ref)


def _moe_experts(xs, blk_e, nused, w1, b1g, b1l, w2, b2, bm):
    rows, W = xs.shape
    nb = rows // bm
    D, F = w1.shape[1], w1.shape[2] // 2
    last = lambda j, be, nu: jnp.minimum(j, jnp.maximum(nu[0] - 1, 0))
    wspec = lambda k, n: pl.BlockSpec((None, k, n), lambda j, be, nu: (be[last(j, be, nu)], 0, 0))
    grid_spec = pltpu.PrefetchScalarGridSpec(
        num_scalar_prefetch=2,
        grid=(nb,),
        in_specs=[pl.BlockSpec((bm, W), lambda j, be, nu: (j, 0)),
                  wspec(D, 2 * F), wspec(1, F), wspec(1, F), wspec(F, D), wspec(1, D)],
        out_specs=pl.BlockSpec((bm, W), lambda j, be, nu: (j, 0)),
        scratch_shapes=[pltpu.VMEM((D, F), BF16), pltpu.VMEM((D, F), BF16), pltpu.VMEM((F, D), BF16)],
    )
    return pl.pallas_call(
        _moe_kernel,
        grid_spec=grid_spec,
        out_shape=jax.ShapeDtypeStruct((rows, W), U32),
        compiler_params=_params("arbitrary"),
        name="moe_experts",
    )(blk_e, nused, xs, w1, b1g, b1l, w2, b2)


def _combine_gather_start(dest_ref, y_hbm, buf, sem, slot, tm):
    for r in range(tm):
        for kx in range(TOP_K):
            d = dest_ref[0, 0, r * TOP_K + kx]
            pltpu.make_async_copy(y_hbm.at[pl.ds(d, 1)], buf.at[slot, kx, pl.ds(r, 1)], sem.at[slot]).start()


def _combine_kernel(dest_ref, destn_ref, gt_ref, h1_ref, g_ref, b_ref, y_hbm, o_hbm, buf, sem, obuf, osem,
                    *, dn_alpha, nt, skip):
    i = pl.program_id(0)
    n = pl.num_programs(0)
    tm = h1_ref.shape[0]
    slot = i % 2
    b = i // nt
    j = i % nt

    def out_wait(rows):
        pltpu.make_async_copy(obuf.at[0, pl.ds(0, rows)], o_hbm.at[0, pl.ds(0, rows)], osem.at[0]).wait()

    @pl.when(i == 0)
    def _():
        _combine_gather_start(dest_ref, y_hbm, buf, sem, 0, tm)

    for nxt in (0, 1):
        @pl.when(jnp.logical_and(i + 1 < n, slot == 1 - nxt))
        def _():
            _combine_gather_start(destn_ref, y_hbm, buf, sem, nxt, tm)

    for kx in range(TOP_K):
        pltpu.make_async_copy(y_hbm.at[pl.ds(0, tm)], buf.at[slot, kx], sem.at[slot]).wait()
    gt = gt_ref[...]
    lo, hi = _unpack_bf16_pair(buf[slot, 0])
    moe_lo, moe_hi = gt[:, 0:1] * lo, gt[:, 0:1] * hi
    for kx in range(1, TOP_K):
        lo, hi = _unpack_bf16_pair(buf[slot, kx])
        moe_lo = moe_lo + gt[:, kx:kx + 1] * lo
        moe_hi = moe_hi + gt[:, kx:kx + 1] * hi
    moe = jnp.concatenate([moe_lo, moe_hi], axis=1)
    obuf[slot] = _layer_norm(dn_alpha * h1_ref[...] + moe, g_ref[...], b_ref[...])

    @pl.when(jnp.logical_and(i > 0, (i - 1) % nt == 0))
    def _():
        out_wait(tm - skip)

    @pl.when(jnp.logical_and(i > 0, (i - 1) % nt != 0))
    def _():
        out_wait(tm)

    @pl.when(j == 0)
    def _():
        pltpu.make_async_copy(obuf.at[slot, pl.ds(skip, tm - skip)], o_hbm.at[b, pl.ds(0, tm - skip)],
                              osem.at[0]).start()

    @pl.when(j != 0)
    def _():
        start = pl.multiple_of(j * tm - skip, 8)
        pltpu.make_async_copy(obuf.at[slot], o_hbm.at[b, pl.ds(start, tm)], osem.at[0]).start()

    @pl.when(i == n - 1)
    def _():
        if nt == 1:
            out_wait(tm - skip)
        else:
            out_wait(tm)


def _moe_combine(dest, gate, h1, ln_g, ln_b, yb, dn_alpha, B, T, skip):
    N, D = h1.shape
    tm = _largest_tile(T, 384)
    nt = T // tm
    n = N // tm
    assert skip % 8 == 0 and skip < tm
    dest3 = dest.reshape(n, 1, tm * TOP_K)
    tile = lambda w: pl.BlockSpec((tm, w), lambda i: (i, 0))
    return pl.pallas_call(
        functools.partial(_combine_kernel, dn_alpha=dn_alpha, nt=nt, skip=skip),
        grid=(n,),
        in_specs=[
            pl.BlockSpec((1, 1, tm * TOP_K), lambda i: (i, 0, 0), memory_space=pltpu.SMEM),
            pl.BlockSpec((1, 1, tm * TOP_K), lambda i: (jnp.minimum(i + 1, n - 1), 0, 0), memory_space=pltpu.SMEM),
            tile(LANES), tile(D), _full((1, D)), _full((1, D)),
            pl.BlockSpec(memory_space=pl.ANY),
        ],
        out_specs=pl.BlockSpec(memory_space=pl.ANY),
        out_shape=jax.ShapeDtypeStruct((B, T - skip, D), F32),
        scratch_shapes=[pltpu.VMEM((2, TOP_K, tm, yb.shape[1]), U32), pltpu.SemaphoreType.DMA((2,)),
                        pltpu.VMEM((2, tm, D), F32), pltpu.SemaphoreType.DMA((1,))],
        compiler_params=_params("arbitrary"),
        name="moe_combine",
    )(dest3, dest3, gate, h1, ln_g, ln_b, yb)


MOE_BLOCK_ROWS = 512


def _route_tables(n_asg, counts, n_experts):
    bm = min(MOE_BLOCK_ROWS, max(8, 1 << int(math.log2(max(1, n_asg // n_experts)))))
    nb = -(-n_asg // bm) + n_experts
    padded = (counts + bm - 1) // bm * bm
    pends = jnp.cumsum(padded)
    starts = (pends - padded).astype(I32)
    blk_start = jnp.arange(nb, dtype=I32) * bm
    blk_e = jnp.minimum(jnp.sum(pends[None, :] <= blk_start[:, None], axis=1), n_experts - 1).astype(I32)
    nused = (pends[-1] // bm).astype(I32).reshape(1)
    n_pad = nb * bm - n_asg
    gap = padded - counts
    gap_end = jnp.cumsum(gap)
    i = jnp.arange(n_pad, dtype=I32)
    e = jnp.sum(gap_end[None, :] <= i[:, None], axis=1)
    ec = jnp.minimum(e, n_experts - 1)
    in_group = (pends - padded + counts)[ec] + i - (gap_end - gap)[ec]
    pad_slots = jnp.where(e < n_experts, in_group, pends[-1] + i - gap_end[-1]).astype(I32)
    return starts, pad_slots, blk_e, nused, bm, nb * bm


def _mixers(x, prefix, prev_row, s0, past_k, past_v, past_logf, wts):
    B, D = x.shape[0], x.shape[2]
    T = x.shape[1] + prefix.shape[0]
    h, q, k, v, rkv, gates, lmid, logf, rkv0 = _inproj(
        x, prefix, prev_row, wts['ln0_g'], wts['ln0_b'], wts['wm'], wts['wff'], wts['bff'], wts['mu3'],
        wts['w1'], wts['a1'], wts['g1'])
    if past_k is None:
        k_aug, v_bf = _fox_prep(logf, k, v, wts['sel_c'])
        fox = _fox_attention(q, k_aug, v_bf, wts['sel_q'], 0)
    else:
        P = past_k.shape[1]
        past_pad = jnp.pad(past_logf.astype(F32), ((0, 0), (0, 0), (0, LANES - N_HEADS)))
        k_aug, v_bf = _fox_prep(past_pad, past_k.reshape(B, P, HW), past_v.reshape(B, P, HW), wts['sel_c'],
                                new=(logf, k, v))
        fox = _fox_attention(q, k_aug, v_bf, wts['sel_q'], P)
    rw, s_fin = _rwkv(rkv, lmid, rkv0, s0, wts['mu_rkv'], wts['w0'], wts['w2'], wts['a0'], wts['a2'], wts['g2'],
                      wts['k_k'], wts['k_a'], wts['r_k'], wts['gn_g'], wts['gn_b'])
    N = B * T
    tokens = dict(fox=fox.reshape(N, HW), rw=rw.reshape(N, HW), gates=gates.reshape(N, 2 * D), h=h.reshape(N, D))
    new_k = k.reshape(B, T, N_HEADS, HEAD_DIM)
    new_v = v.reshape(B, T, N_HEADS, HEAD_DIM)
    return tokens, (new_k, new_v, logf[:, :, :N_HEADS], s_fin, h[:, T - 1:T, :])


def _merge_moe(streams, wts):
    dn_alpha, n_experts = wts['dn_alpha'], wts['n_experts']
    counts = jnp.zeros((1, LANES), I32)
    routed = []
    for tokens, B, T, y_skip in streams:
        h1, h1p, top_idx, gate, rank, counts = _merge_route(
            tokens['fox'], tokens['rw'], tokens['gates'], tokens['h'],
            wts['w_up_a'], wts['w_up_b'], wts['w_out'], wts['ln1_g'], wts['ln1_b'],
            wts['wr_hi'], wts['wr_lo'], wts['br'], counts, dn_alpha)
        routed.append((h1, h1p, top_idx[:, :TOP_K], gate, rank[:, :TOP_K]))
    n_asg = sum(r[0].shape[0] for r in routed) * TOP_K
    starts, pad_slots, blk_e, nused, bm, rows = _route_tables(n_asg, counts[0, :n_experts], n_experts)
    dests = [(starts[top_idx] + rank).astype(I32) for _, _, top_idx, _, rank in routed]
    zero_slots = jnp.concatenate([pad_slots] + [d.reshape(-1) for d in dests[1:]])
    xs = None
    for (h1, h1p, _, _, _), dest in zip(routed, dests):
        xs = _moe_dispatch(h1p, dest, zero_slots, rows, extend=xs)
    yb = _moe_experts(xs, blk_e, nused, wts['we1'], wts['b1g'], wts['b1l'], wts['we2'], wts['be2'], bm)
    return [_moe_combine(dest, gate, h1, wts['ln2_g'], wts['ln2_b'], yb, dn_alpha, B, T, y_skip)
            for (h1, _, _, gate, _), dest, (_, B, T, y_skip) in zip(routed, dests, streams)]


def kernel(x_prompt, x_sample, cache_fox_k, cache_fox_v, cache_fox_logf, state_rwkv, state_shift, meta, ln0_g, ln0_b, w_in, b_forget, mu_w, mu_a, mu_g, mu_rkv, w0, w1, w2, a0, a1, a2, g1, g2, k_k, k_a, r_k, gn_g, gn_b, w_up_a, w_up_b, w_out, ln1_g, ln1_b, w_router, b_router, w_e1, b_e1, w_e2, b_e2, ln2_g, ln2_b):
    depth, D, in_cols = w_in.shape
    assert depth == 1 and D == 1024 and in_cols == 6 * HW + N_HEADS + 2 * D
    n_experts = w_router.shape[2]
    assert n_experts <= LANES
    B = x_prompt.shape[0]
    l = 0
    w = w_in[l]
    off_ff = 3 * HW
    row = lambda a: a.reshape(1, -1).astype(F32)
    wr = jnp.pad(w_router[l], ((0, 0), (0, LANES - n_experts)))
    wr_hi = wr.astype(BF16)
    sel_q, sel_c = _aug_select_matrices()
    wts = dict(
        sel_q=sel_q, sel_c=sel_c,
        dn_alpha=float((2 * depth) ** 0.25), n_experts=n_experts,
        ln0_g=row(ln0_g), ln0_b=row(ln0_b),
        wm=jnp.concatenate([w[:, :off_ff], w[:, off_ff + N_HEADS:]], axis=1).astype(BF16),
        wff=jnp.pad(w[:, off_ff:off_ff + N_HEADS], ((0, 0), (0, LANES - N_HEADS))).astype(BF16),
        bff=jnp.pad(row(b_forget[l]), ((0, 0), (0, LANES - N_HEADS))),
        mu3=jnp.stack([mu_w[l], mu_a[l], mu_g[l]], axis=0),
        w1=w1[l].astype(BF16), a1=a1[l].astype(BF16), g1=g1[l].astype(BF16),
        mu_rkv=row(mu_rkv[l]), w0=row(w0[l]), w2=w2[l].astype(BF16), a0=row(a0[l]), a2=a2[l].astype(BF16),
        g2=g2[l].astype(BF16), k_k=row(k_k[l]), k_a=row(k_a[l]), r_k=row(r_k[l]), gn_g=row(gn_g[l]),
        gn_b=row(gn_b[l]),
        w_up_a=w_up_a[l].astype(BF16), w_up_b=w_up_b[l].astype(BF16), w_out=w_out[l].astype(BF16),
        ln1_g=row(ln1_g[l]), ln1_b=row(ln1_b[l]),
        wr_hi=wr_hi, wr_lo=(wr - wr_hi.astype(F32)).astype(BF16),
        br=jnp.pad(row(b_router[l]), ((0, 0), (0, LANES - n_experts)), constant_values=NEG_BIG),
        we1=w_e1[l],
        b1g=b_e1[l][:, None, 0::2], b1l=b_e1[l][:, None, 1::2],
        we2=w_e2[l], be2=b_e2[l][:, None, :],
        ln2_g=row(ln2_g[l]), ln2_b=row(ln2_b[l]),
    )
    zero_row = jnp.zeros((B, 1, D), F32)
    zero_state = jnp.zeros((B, N_HEADS, HEAD_DIM, HEAD_DIM), F32)
    no_prefix = jnp.zeros((0, D), x_sample.dtype)
    tok_p, (k_p, v_p, lf_p, s_p, sh_p) = _mixers(x_prompt, meta.astype(x_prompt.dtype), zero_row, zero_state,
                                                 None, None, None, wts)
    tok_s, (k_s, v_s, lf_s, s_s, sh_s) = _mixers(x_sample, no_prefix, state_shift[l], state_rwkv[l], cache_fox_k[l],
                                                 cache_fox_v[l], cache_fox_logf[l], wts)
    y_p, y_s = _merge_moe([(tok_p, B, x_prompt.shape[1] + N_META, N_META), (tok_s,) + x_sample.shape[:2] + (0,)],
                          wts)
    ex = lambda a: a[None]
    return (y_p, y_s, ex(k_p), ex(v_p), ex(lf_p), ex(s_p), ex(sh_p),
            ex(k_s), ex(v_s), ex(lf_s), ex(s_s), ex(sh_s))
```

```python
import functools
import math

import jax
import jax.numpy as jnp
import numpy as np
from jax import lax
from jax.experimental import pallas as pl
from jax.experimental.pallas import tpu as pltpu

F32 = jnp.float32
BF16 = jnp.bfloat16
I32 = jnp.int32
U32 = jnp.uint32

N_META = 16
HEAD_DIM = 64
N_HEADS = 8
HW = N_HEADS * HEAD_DIM
TOP_K = 4
SWIGLU_LIMIT = 7.0
SWIGLU_ALPHA = 1.702
LN_EPS = 1e-5
GN_EPS = 64e-5
LANES = 128
NEG_BIG = -1e30
VMEM_LIMIT_BYTES = 56 * 1024 * 1024

NT_DIMS = (((1,), (1,)), ((), ()))
TN_DIMS = (((0,), (0,)), ((), ()))


def _params(*sem):
    return pltpu.CompilerParams(dimension_semantics=sem, vmem_limit_bytes=VMEM_LIMIT_BYTES)


def _largest_tile(n, cap, mult=8):
    best = None
    for d in range(mult, min(n, cap) + 1, mult):
        if n % d == 0:
            best = d
    assert best is not None, (n, cap, mult)
    return best


def _sigmoid(x):
    return 1.0 / (1.0 + jnp.exp(-x))


def _softplus(x):
    return jnp.maximum(x, 0.0) + jnp.log1p(jnp.exp(-jnp.abs(x)))


def _layer_norm(x, g, b):
    mu = jnp.mean(x, -1, keepdims=True)
    xc = x - mu
    var = jnp.mean(xc * xc, -1, keepdims=True)
    return xc * lax.rsqrt(var + LN_EPS) * g + b


def _dot(a, b):
    return jnp.dot(a, b, preferred_element_type=F32)


def _full(shape):
    n = len(shape)
    return pl.BlockSpec(shape, lambda *_: (0,) * n)


def _pack_bf16_pair(x):
    w = x.shape[1] // 2
    bits = lambda t: lax.bitcast_convert_type(t.astype(BF16).astype(F32), U32)
    return (bits(x[:, :w]) >> 16) | (bits(x[:, w:]) & jnp.uint32(0xFFFF0000))


def _unpack_bf16_pair(u):
    return lax.bitcast_convert_type(u << 16, F32), lax.bitcast_convert_type(u & jnp.uint32(0xFFFF0000), F32)


C_Q, C_K, C_V, C_RKV, C_GA, C_END = 0, HW, 2 * HW, 3 * HW, 6 * HW, 6 * HW + 2048


def _inproj_kernel(x_ref, pre_ref, prev_ref, g_ref, b_ref, wm_ref, wff_ref, bff_ref, mu_ref, w1_ref, a1_ref, g1_ref,
                   h_ref, q_ref, k_ref, v_ref, rkv_ref, gate_ref, lmid_ref, logf_ref, rkv0_ref,
                   carry_ref, *, n_pre):
    t = pl.program_id(1)
    x = x_ref[0] if n_pre else x_ref[...]
    tt = x.shape[0]
    if n_pre:
        x = jnp.where(t == 0, jnp.concatenate([pre_ref[...], x[0:tt - n_pre]], axis=0), x)
    h = _layer_norm(x, g_ref[...], b_ref[...])
    h_ref[...] = h

    @pl.when(t == 0)
    def _():
        prev = prev_ref[...]
        carry_ref[...] = prev
        p8 = jnp.broadcast_to(prev, (8, prev.shape[1])).astype(BF16)
        rkv0_ref[...] = _dot(p8, wm_ref[:, C_RKV:C_GA])[0:1]

    rows = lax.broadcasted_iota(I32, h.shape, 0)
    hprev = jnp.where(rows == 0, carry_ref[...], pltpu.roll(h, 1, axis=0))
    carry_ref[...] = h[tt - 1:tt, :]
    dx = hprev - h
    hb = h.astype(BF16)
    q_ref[...] = _dot(hb, wm_ref[:, C_Q:C_K]).astype(BF16)
    k_ref[...] = _dot(hb, wm_ref[:, C_K:C_V])
    v_ref[...] = _dot(hb, wm_ref[:, C_V:C_RKV])
    rkv_ref[...] = _dot(hb, wm_ref[:, C_RKV:C_GA]).astype(BF16)
    gate_ref[...] = _sigmoid(_dot(hb, wm_ref[:, C_GA:C_END])).astype(BF16)
    ff = _dot(hb, wff_ref[...]) + bff_ref[...]
    logf_ref[...] = -_softplus(-ff)
    mu = mu_ref[...]
    lmid_ref[:, 0:64] = _dot((h + dx * mu[0:1]).astype(BF16), w1_ref[...])
    lmid_ref[:, 64:128] = _dot((h + dx * mu[1:2]).astype(BF16), a1_ref[...])
    lmid_ref[:, 128:256] = _dot((h + dx * mu[2:3]).astype(BF16), g1_ref[...])


def _inproj(x, prefix, prev_row, ln_g, ln_b, wm, wff, bff, mu3, w1, a1, g1):
    B, Tx, D = x.shape
    n_pre = prefix.shape[0]
    T = Tx + n_pre
    tt = _largest_tile(T, 384)
    nt = T // tt
    assert n_pre % 8 == 0 and n_pre < tt
    tile = lambda w: pl.BlockSpec((None, tt, w), lambda b, t: (b, t, 0))
    if n_pre:
        x_spec = pl.BlockSpec((pl.Element(1), pl.Element(tt), pl.Element(D)),
                              lambda b, t: (b, pl.multiple_of(jnp.maximum(t * tt - n_pre, 0), 8), 0))
        pre_in = prefix
    else:
        x_spec = tile(D)
        pre_in = jnp.zeros((8, D), x.dtype)
    row = lambda w: pl.BlockSpec((None, 1, w), lambda b, t: (b, 0, 0))
    out_shape = [
        jax.ShapeDtypeStruct((B, T, D), F32),
        jax.ShapeDtypeStruct((B, T, HW), BF16),
        jax.ShapeDtypeStruct((B, T, HW), F32),
        jax.ShapeDtypeStruct((B, T, HW), F32),
        jax.ShapeDtypeStruct((B, T, 3 * HW), BF16),
        jax.ShapeDtypeStruct((B, T, 2 * D), BF16),
        jax.ShapeDtypeStruct((B, T, 256), F32),
        jax.ShapeDtypeStruct((B, T, LANES), F32),
        jax.ShapeDtypeStruct((B, 1, 3 * HW), F32),
    ]
    return pl.pallas_call(
        functools.partial(_inproj_kernel, n_pre=n_pre),
        grid=(B, nt),
        in_specs=[x_spec, _full(pre_in.shape), row(D), _full((1, D)), _full((1, D)), _full(wm.shape), _full(wff.shape),
                  _full(bff.shape), _full(mu3.shape), _full(w1.shape), _full(a1.shape), _full(g1.shape)],
        out_specs=[tile(D), tile(HW), tile(HW), tile(HW), tile(3 * HW), tile(2 * D), tile(256), tile(LANES),
                   row(3 * HW)],
        out_shape=out_shape,
        scratch_shapes=[pltpu.VMEM((1, D), F32)],
        compiler_params=_params("arbitrary", "arbitrary"),
        name="inproj",
    )(x, pre_in, prev_row, ln_g, ln_b, wm, wff, bff, mu3, w1, a1, g1)


HEAD_PAD = 2 * HEAD_DIM
C_SPLIT = 3


def _aug_select_matrices():
    rows = np.arange(HW)
    sel_q = np.zeros((HW, N_HEADS * HEAD_PAD), np.float32)
    sel_q[rows, (rows // HEAD_DIM) * HEAD_PAD + rows % HEAD_DIM] = HEAD_DIM ** -0.5
    p = np.repeat(np.arange(C_SPLIT), N_HEADS)
    h = np.tile(np.arange(N_HEADS), C_SPLIT)
    sel_c = np.zeros((LANES, N_HEADS * HEAD_PAD), np.float32)
    sel_c[p * N_HEADS + h, h * HEAD_PAD + HEAD_DIM + p] = 1.0
    return jnp.asarray(sel_q, BF16), jnp.asarray(sel_c, BF16)


def _split3(x):
    hi = x.astype(BF16)
    r1 = x - hi.astype(F32)
    mid = r1.astype(BF16)
    return hi, mid, (r1 - mid.astype(F32)).astype(BF16)


def _fox_prep_kernel(lf_ref, k_ref, v_ref, *rest, n_new):
    if n_new:
        nlf_ref, nk_ref, nv_ref, selc_ref, ka_ref, vt_ref, carry_ref = rest
    else:
        selc_ref, ka_ref, vt_ref, carry_ref = rest
    t = pl.program_id(1)
    last = t == pl.num_programs(1) - 1

    def rows_of(ref, new_ref):
        if not n_new:
            return ref[...]
        x = ref[0]
        return jnp.where(last, jnp.concatenate([x[n_new:], new_ref[...]], axis=0), x)

    lf_in = rows_of(lf_ref, nlf_ref if n_new else None)
    k_in = rows_of(k_ref, nk_ref if n_new else None)
    v_in = rows_of(v_ref, nv_ref if n_new else None)
    tt = lf_in.shape[0]

    @pl.when(t == 0)
    def _():
        carry_ref[...] = jnp.zeros_like(carry_ref)

    r = lax.broadcasted_iota(I32, (tt, tt), 0)
    c = lax.broadcasted_iota(I32, (tt, tt), 1)
    tri = jnp.where(r >= c, 1.0, 0.0).astype(BF16)
    cs3 = _dot(tri, jnp.concatenate(_split3(lf_in), axis=1))
    cs = cs3[:, 0:LANES] + cs3[:, LANES:2 * LANES] + cs3[:, 2 * LANES:3 * LANES] + carry_ref[...]
    carry_ref[...] = cs[tt - 1:tt, :]
    hi, mid, lo = _split3(-cs)
    is_head = lax.broadcasted_iota(I32, (tt, LANES), 1) < N_HEADS
    keep = lambda part: jnp.where(is_head, part.astype(F32), 0.0)
    packed = keep(hi) + pltpu.roll(keep(mid), N_HEADS, axis=1) + pltpu.roll(keep(lo), 2 * N_HEADS, axis=1)
    kc = _dot(packed.astype(BF16), selc_ref[...])
    k = k_in
    pad = jnp.zeros((tt, HEAD_PAD - HEAD_DIM), F32)
    for h in range(N_HEADS):
        hp = slice(h * HEAD_PAD, (h + 1) * HEAD_PAD)
        ka_ref[:, hp] = (jnp.concatenate([k[:, h * HEAD_DIM:(h + 1) * HEAD_DIM], pad], axis=1) + kc[:, hp]).astype(BF16)
    ii = lax.broadcasted_iota(I32, (HW, HW), 0)
    jj = lax.broadcasted_iota(I32, (HW, HW), 1)
    eye = jnp.where(ii == jj, 1.0, 0.0).astype(BF16)
    vt_ref[...] = lax.dot_general(eye, v_in.astype(BF16), NT_DIMS, preferred_element_type=F32).astype(BF16)


def _fox_prep(logf, k, v, sel_c, new=None):
    B, P, _ = k.shape
    n_new = 0 if new is None else new[1].shape[1]
    T = P + n_new
    tt = _largest_tile(T, 384)
    nt = T // tt
    tile = lambda w: pl.BlockSpec((None, tt, w), lambda b, t: (b, t, 0))
    if n_new:
        assert n_new % 8 == 0 and n_new < tt <= P and (P - tt) % 8 == 0
        win = lambda w: pl.BlockSpec((pl.Element(1), pl.Element(tt), pl.Element(w)),
                                     lambda b, t: (b, pl.multiple_of(jnp.minimum(t * tt, P - tt), 8), 0))
        fresh = lambda w: pl.BlockSpec((None, n_new, w), lambda b, t: (b, 0, 0))
        in_specs = [win(LANES), win(HW), win(HW), fresh(LANES), fresh(HW), fresh(HW), _full(sel_c.shape)]
        operands = (logf, k, v) + tuple(new) + (sel_c,)
    else:
        in_specs = [tile(LANES), tile(HW), tile(HW), _full(sel_c.shape)]
        operands = (logf, k, v, sel_c)
    return pl.pallas_call(
        functools.partial(_fox_prep_kernel, n_new=n_new),
        grid=(B, nt),
        in_specs=in_specs,
        out_specs=[tile(N_HEADS * HEAD_PAD), pl.BlockSpec((None, None, HW, tt), lambda b, t: (b, t, 0, 0))],
        out_shape=[jax.ShapeDtypeStruct((B, T, N_HEADS * HEAD_PAD), BF16),
                   jax.ShapeDtypeStruct((B, nt, HW, tt), BF16)],
        scratch_shapes=[pltpu.VMEM((1, LANES), F32)],
        compiler_params=_params("arbitrary", "arbitrary"),
        name="fox_prep",
    )(*operands)


def _fox_kernel(qi_ref, ki_ref, last_ref, q_ref, ka_ref, vt_ref, selq_ref, o_ref, qa_ref, m_ref, l_ref, acc_ref,
                *, q0, tq, tk):
    p = pl.program_id(1)
    qi = qi_ref[p]
    ki = ki_ref[p]

    @pl.when(ki == 0)
    def _():
        m_ref[...] = jnp.full_like(m_ref, NEG_BIG)
        l_ref[...] = jnp.zeros_like(l_ref)
        acc_ref[...] = jnp.zeros_like(acc_ref)
        lane = lax.broadcasted_iota(I32, qa_ref.shape, 1) % HEAD_PAD
        ones = jnp.where(jnp.logical_and(lane >= HEAD_DIM, lane < HEAD_DIM + C_SPLIT), 1.0, 0.0)
        qa_ref[...] = (_dot(q_ref[...], selq_ref[...]) + ones).astype(BF16)

    first_q = q0 + qi * tq
    tile_first = ki * tk
    tile_last = tile_first + tk - 1

    def scores(h):
        hp = slice(h * HEAD_PAD, (h + 1) * HEAD_PAD)
        return lax.dot_general(ka_ref[:, hp], qa_ref[:, hp], NT_DIMS, preferred_element_type=F32)

    def tile_update(masked):
        if masked:
            key_pos = tile_first + lax.broadcasted_iota(I32, (tk, tq), 0)
            qry_pos = first_q + lax.broadcasted_iota(I32, (tk, tq), 1)
            bias = jnp.where(qry_pos >= key_pos, 0.0, NEG_BIG)
        m_all = m_ref[...]
        l_all = l_ref[...]
        m_rows, l_rows = [], []
        s_next = scores(0)
        for h in range(N_HEADS):
            hs = slice(h * HEAD_DIM, (h + 1) * HEAD_DIM)
            s = s_next
            if h + 1 < N_HEADS:
                s_next = scores(h + 1)
            if masked:
                s = s + bias
            m_prev = m_all[h:h + 1, :]
            m_new = jnp.maximum(m_prev, jnp.max(s, 0, keepdims=True))
            alpha = jnp.exp(m_prev - m_new)
            p = jnp.exp(s - m_new)
            l_rows.append(alpha * l_all[h:h + 1, :] + jnp.sum(p, 0, keepdims=True))
            m_rows.append(m_new)
            acc_ref[hs, :] = alpha * acc_ref[hs, :] + _dot(vt_ref[hs, :], p.astype(BF16))
        m_ref[...] = jnp.concatenate(m_rows, axis=0)
        l_ref[...] = jnp.concatenate(l_rows, axis=0)

    @pl.when(jnp.logical_and(tile_first <= first_q + tq - 1, tile_last > first_q))
    def _():
        tile_update(True)

    @pl.when(tile_last <= first_q)
    def _():
        tile_update(False)

    @pl.when(last_ref[p] == 1)
    def _():
        on = jnp.concatenate(
            [acc_ref[h * HEAD_DIM:(h + 1) * HEAD_DIM, :] / l_ref[h:h + 1, :] for h in range(N_HEADS)], axis=0)
        eye = jnp.where(lax.broadcasted_iota(I32, (tq, tq), 0) == lax.broadcasted_iota(I32, (tq, tq), 1),
                        1.0, 0.0).astype(BF16)
        o_ref[...] = lax.dot_general(eye, on.astype(BF16), NT_DIMS, preferred_element_type=F32).astype(o_ref.dtype)


def _fox_attention(q, k_aug, v_t, sel_q, q0):
    B, Tq, _ = q.shape
    Tk = k_aug.shape[1]
    WA = N_HEADS * HEAD_PAD
    tq = _largest_tile(Tq, 384)
    nk, tk = v_t.shape[1], v_t.shape[3]
    assert nk * tk == Tk
    nq = Tq // tq
    pairs = [(qi, ki) for qi in range(nq) for ki in range(min((q0 + (qi + 1) * tq - 1) // tk, nk - 1) + 1)]
    qi_tab = jnp.array([p[0] for p in pairs], I32)
    ki_tab = jnp.array([p[1] for p in pairs], I32)
    last_tab = jnp.array([int(i + 1 == len(pairs) or pairs[i + 1][0] != pairs[i][0]) for i in range(len(pairs))], I32)
    qspec = lambda w: pl.BlockSpec((None, tq, w), lambda b, p, qt, kt, lt: (b, qt[p], 0))
    grid_spec = pltpu.PrefetchScalarGridSpec(
        num_scalar_prefetch=3,
        grid=(B, len(pairs)),
        in_specs=[qspec(HW),
                  pl.BlockSpec((None, tk, WA), lambda b, p, qt, kt, lt: (b, kt[p], 0)),
                  pl.BlockSpec((None, None, HW, tk), lambda b, p, qt, kt, lt: (b, kt[p], 0, 0)),
                  pl.BlockSpec(sel_q.shape, lambda b, p, qt, kt, lt: (0, 0))],
        out_specs=qspec(HW),
        scratch_shapes=[pltpu.VMEM((tq, WA), BF16), pltpu.VMEM((N_HEADS, tq), F32),
                        pltpu.VMEM((N_HEADS, tq), F32), pltpu.VMEM((HW, tq), F32)],
    )
    return pl.pallas_call(
        functools.partial(_fox_kernel, q0=q0, tq=tq, tk=tk),
        grid_spec=grid_spec,
        out_shape=jax.ShapeDtypeStruct((B, Tq, HW), BF16),
        compiler_params=_params("arbitrary", "arbitrary"),
        name="fox_attention",
    )(qi_tab, ki_tab, last_tab, q, k_aug, v_t, sel_q)


def _rwkv_kernel(rkv_ref, lmid_ref, rkv0_ref, s0_ref, mu_ref, w0_ref, w2_ref, a0_ref, a2_ref, g2_ref,
                 kk_ref, ka_ref, rk_ref, gng_ref, gnb_ref, o_ref, sfin_ref, state_ref, carry_ref, *, chunk, levels):
    C = chunk
    NB, T = rkv_ref.shape[0], rkv_ref.shape[1]
    state_ref[...] = s0_ref[...]
    carry_ref[...] = rkv0_ref[...]
    row_w = lax.broadcasted_iota(I32, (C, 3 * HW), 0)
    row_h = lax.broadcasted_iota(I32, (C, HW), 0)
    r_i = lax.broadcasted_iota(I32, (C, C), 0)
    c_i = lax.broadcasted_iota(I32, (C, C), 1)
    strict = r_i > c_i
    incl = r_i >= c_i
    mid = C // 2 - 1 if C > 1 else 0
    hsl = [slice(h * HEAD_DIM, (h + 1) * HEAD_DIM) for h in range(N_HEADS)]
    nt = lambda x, y: lax.dot_general(x, y, NT_DIMS, preferred_element_type=F32)
    tn = lambda x, y: lax.dot_general(x, y, TN_DIMS, preferred_element_type=F32)

    def row_inputs(bb, off):
        x = rkv_ref[bb, pl.ds(off, C), :].astype(F32)
        prev = jnp.where(row_w == 0, carry_ref[bb], pltpu.roll(x, 1, axis=0))
        carry_ref[bb] = x[C - 1:C, :]
        x = x + (prev - x) * mu_ref[...]
        r, k0, v = x[:, 0:HW], x[:, HW:2 * HW], x[:, 2 * HW:3 * HW]
        lm = lmid_ref[bb, pl.ds(off, C), :]
        w_pre = w0_ref[...] + _dot(jnp.tanh(lm[:, 0:64]).astype(BF16), w2_ref[...])
        a = _sigmoid(a0_ref[...] + _dot(lm[:, 64:128].astype(BF16), a2_ref[...]))
        g = _dot(_sigmoid(lm[:, 128:256]).astype(BF16), g2_ref[...])
        w_log = -_softplus(-w_pre) - 0.5
        logdec = -jnp.exp(w_log)
        L = logdec
        sh = 1
        while sh < C:
            L = L + jnp.where(row_h >= sh, pltpu.roll(L, sh, axis=0), 0.0)
            sh *= 2
        l_mid = L[mid:mid + 1, :]
        l_tot = L[C - 1:C, :]
        return dict(r=r, v=v, a=a, g=g, kk_raw=k0 * kk_ref[...], k=k0 * (1.0 + (a - 1.0) * ka_ref[...]),
                    e_a=jnp.exp(L - logdec - l_mid), e_r=jnp.exp(L - l_mid), e_k=jnp.exp(l_mid - L),
                    e_s=jnp.exp(l_tot - L), w_tot=jnp.exp(l_tot), e_mid=jnp.exp(l_mid))

    gap = jnp.zeros((C, LANES - C), F32)
    gap2 = jnp.zeros((LANES - C, HEAD_DIM), F32)
    lane_pair = lambda left, right: jnp.concatenate([left, gap, right], axis=1)

    def chunk_body(i, carry):
        off = pl.multiple_of(i * C, C)
        rows = [row_inputs(bb, off) for bb in range(NB)]
        rk = rk_ref[...]
        chains = [(bb, h) for bb in range(NB) for h in range(N_HEADS)]
        X = range(len(chains))
        col = lambda name: [rows[bb][name][:, hsl[h]] for bb, h in chains]
        kkh = [x * lax.rsqrt(jnp.maximum(jnp.sum(x * x, -1, keepdims=True), 1e-24)) for x in col('kk_raw')]
        r_h, k_h, v_h, a_h = col('r'), col('k'), col('v'), col('a')
        e_a, e_r, e_k, e_s, w_tot, e_mid = col('e_a'), col('e_r'), col('e_k'), col('e_s'), col('w_tot'), col('e_mid')
        b_h = [kkh[c] * a_h[c] for c in X]
        ar = [jnp.concatenate([-kkh[c] * e_a[c], r_h[c] * e_r[c]], axis=0) for c in X]
        bkd = [jnp.concatenate([b_h[c] * e_k[c], gap2, k_h[c] * e_k[c]], axis=0) for c in X]
        s_old = [state_ref[bb, h] for bb, h in chains]
        gram = [nt(ar[c], bkd[c]) for c in X]
        x0 = [nt(ar[c], s_old[c] * e_mid[c]) for c in X]
        g_b = [gram[c][:, 0:C] for c in X]
        g_k = [gram[c][:, LANES:LANES + C] for c in X]
        a_mat = [jnp.where(strict, g_b[c][0:C], 0.0) for c in X]
        kv = [_dot(jnp.concatenate([jnp.where(strict, g_k[c][0:C], 0.0), jnp.where(incl, g_k[c][C:2 * C], 0.0)],
                                   axis=0), v_h[c]) for c in X]
        u = [x0[c][0:C] + kv[c][0:C] for c in X]
        for lvl in range(levels):
            if lvl + 1 < levels:
                prod = [_dot(a_mat[c], lane_pair(a_mat[c], u[c])) for c in X]
                a_mat = [prod[c][:, 0:C] for c in X]
                u = [u[c] + prod[c][:, LANES:LANES + HEAD_DIM] for c in X]
            else:
                u = [u[c] + _dot(a_mat[c], u[c]) for c in X]
        y = [x0[c][C:2 * C] + _dot(jnp.where(incl, g_b[c][C:2 * C], 0.0), u[c]) + kv[c][C:2 * C] for c in X]
        for c, (bb, h) in enumerate(chains):
            uv = jnp.concatenate([u[c], v_h[c]], axis=0)
            bks = jnp.concatenate([b_h[c] * e_s[c], k_h[c] * e_s[c]], axis=0)
            state_ref[bb, h] = s_old[c] * w_tot[c] + tn(uv, bks)
        outs = []
        for c, (bb, h) in enumerate(chains):
            mu = jnp.mean(y[c], -1, keepdims=True)
            yc = y[c] - mu
            var = jnp.mean(yc * yc, -1, keepdims=True)
            bonus = jnp.sum(r_h[c] * k_h[c] * rk[:, hsl[h]], -1, keepdims=True) * v_h[c]
            outs.append((yc * lax.rsqrt(var + GN_EPS), bonus))
        for bb in range(NB):
            mine = outs[bb * N_HEADS:(bb + 1) * N_HEADS]
            yn = jnp.concatenate([o[0] for o in mine], axis=1)
            bonus = jnp.concatenate([o[1] for o in mine], axis=1)
            out = (yn * gng_ref[...] + gnb_ref[...] + bonus) * rows[bb]['g']
            o_ref[bb, pl.ds(off, C), :] = out.astype(o_ref.dtype)
        return carry

    lax.fori_loop(0, T // C, chunk_body, 0)
    sfin_ref[...] = state_ref[...]


RWKV_ROWS_PER_STEP = 2


def _rwkv(rkv, lmid, rkv0, s0, mu_rkv, w0, w2, a0, a2, g2, k_k, k_a, r_k, gn_g, gn_b):
    B, T, _ = rkv.shape
    nb = RWKV_ROWS_PER_STEP if B % RWKV_ROWS_PER_STEP == 0 else 1
    chunk = _largest_tile(T, 64, mult=16)
    levels = max(1, math.ceil(math.log2(chunk)))
    seq =lambda w: pl.BlockSpec((nb, T, w), lambda b: (b, 0, 0))
    st = pl.BlockSpec((nb, N_HEADS, HEAD_DIM, HEAD_DIM), lambda b: (b, 0, 0, 0))
    vec = lambda a: _full(a.shape)
    params = (mu_rkv, w0, w2, a0, a2, g2, k_k, k_a, r_k, gn_g, gn_b)
    return pl.pallas_call(
        functools.partial(_rwkv_kernel, chunk=chunk, levels=levels),
        grid=(B // nb,),
        in_specs=[seq(3 * HW), seq(256), pl.BlockSpec((nb, 1, 3 * HW), lambda b: (b, 0, 0)), st]
                 + [vec(p) for p in params],
        out_specs=[seq(HW), st],
        out_shape=[jax.ShapeDtypeStruct((B, T, HW), BF16),
                   jax.ShapeDtypeStruct((B, N_HEADS, HEAD_DIM, HEAD_DIM), F32)],
        scratch_shapes=[pltpu.VMEM((nb, N_HEADS, HEAD_DIM, HEAD_DIM), F32), pltpu.VMEM((nb, 1, 3 * HW), F32)],
        compiler_params=_params("arbitrary"),
        name="rwkv7",
    )(rkv, lmid, rkv0, s0, *params)


def _merge_kernel(fox_ref, rw_ref, gate_ref, h_ref, wa_ref, wb_ref, wo_ref, g_ref, b_ref, wrh_ref, wrl_ref, br_ref,
                  cnt0_ref, h1_ref, h1p_ref, idx_ref, gt_ref, rank_ref, cnt_ref, carry_ref, *, dn_alpha):
    i = pl.program_id(0)
    tm, D = h_ref.shape

    @pl.when(i == 0)
    def _():
        carry_ref[...] = cnt0_ref[...].astype(F32)

    gates = gate_ref[...].astype(F32)
    merged = gates[:, 0:D] * _dot(fox_ref[...], wa_ref[...]) + gates[:, D:2 * D] * _dot(rw_ref[...], wb_ref[...])
    z = dn_alpha * h_ref[...] + _dot(merged.astype(BF16), wo_ref[...])
    h1 = _layer_norm(z, g_ref[...], b_ref[...])
    h1_ref[...] = h1
    h1p_ref[...] = _pack_bf16_pair(h1)
    hi = h1.astype(BF16)
    lo = (h1 - hi.astype(F32)).astype(BF16)
    hw = _dot(hi, jnp.concatenate([wrh_ref[...], wrl_ref[...]], axis=1))
    logits = hw[:, 0:LANES] + hw[:, LANES:2 * LANES] + _dot(lo, wrh_ref[...]) + br_ref[...]
    lane = lax.broadcasted_iota(I32, (tm, LANES), 1)
    lane_f = lane.astype(F32)
    cur = logits
    vals, idxs = [], []
    for _ in range(TOP_K):
        m = jnp.max(cur, -1, keepdims=True)
        ix = jnp.min(jnp.where(cur == m, lane_f, float(LANES)), -1, keepdims=True)
        vals.append(m)
        idxs.append(ix)
        cur = jnp.where(lane_f == ix, -3e38, cur)
    exps = [jnp.exp(vk - vals[0]) for vk in vals]
    denom = exps[0] + exps[1] + exps[2] + exps[3]
    onehot = jnp.zeros((tm, LANES), F32)
    for ix in idxs:
        onehot = onehot + jnp.where(lane_f == ix, 1.0, 0.0)
    r_i = lax.broadcasted_iota(I32, (tm, tm), 0)
    c_i = lax.broadcasted_iota(I32, (tm, tm), 1)
    tri = jnp.where(r_i > c_i, 1.0, 0.0).astype(BF16)
    before = _dot(tri, onehot.astype(BF16)) + carry_ref[...]
    idx_out = jnp.zeros((tm, LANES), F32)
    gt_out = jnp.zeros((tm, LANES), F32)
    rank_out = jnp.zeros((tm, LANES), F32)
    for kx in range(TOP_K):
        rank_k = jnp.sum(jnp.where(lane_f == idxs[kx], before, 0.0), -1, keepdims=True)
        idx_out = jnp.where(lane == kx, idxs[kx], idx_out)
        gt_out = jnp.where(lane == kx, exps[kx] / denom, gt_out)
        rank_out = jnp.where(lane == kx, rank_k, rank_out)
    idx_ref[...] = idx_out.astype(I32)
    gt_ref[...] = gt_out
    rank_ref[...] = rank_out.astype(I32)
    total = carry_ref[...] + jnp.sum(onehot, 0, keepdims=True)
    carry_ref[...] = total
    cnt_ref[...] = total.astype(I32)


def _merge_route(fox, rw, gates, h, wa, wb, wo, ln_g, ln_b, wr_hi, wr_lo, br, counts_before, dn_alpha):
    N, D = h.shape
    tm = _largest_tile(N, 384)
    tile = lambda w: pl.BlockSpec((tm, w), lambda i: (i, 0))
    return pl.pallas_call(
        functools.partial(_merge_kernel, dn_alpha=dn_alpha),
        grid=(N // tm,),
        in_specs=[tile(HW), tile(HW), tile(2 * D), tile(D), _full(wa.shape), _full(wb.shape), _full(wo.shape),
                  _full((1, D)), _full((1, D)), _full(wr_hi.shape), _full(wr_lo.shape), _full(br.shape),
                  _full((1, LANES))],
        out_specs=[tile(D), tile(D // 2), tile(LANES), tile(LANES), tile(LANES), _full((1, LANES))],
        out_shape=[jax.ShapeDtypeStruct((N, D), F32), jax.ShapeDtypeStruct((N, D // 2), U32),
                   jax.ShapeDtypeStruct((N, LANES), I32),
                   jax.ShapeDtypeStruct((N, LANES), F32), jax.ShapeDtypeStruct((N, LANES), I32),
                   jax.ShapeDtypeStruct((1, LANES), I32)],
        scratch_shapes=[pltpu.VMEM((1, LANES), F32)],
        compiler_params=_params("arbitrary"),
        name="merge_route",
    )(fox, rw, gates, h, wa, wb, wo, ln_g, ln_b, wr_hi, wr_lo, br, counts_before)


PERM_W = 256


def _deinterleave_to_bf16(w_ref, g_ref, l_ref):
    half = PERM_W // 2
    ii = lax.broadcasted_iota(I32, (PERM_W, PERM_W), 0)
    jj = lax.broadcasted_iota(I32, (PERM_W, PERM_W), 1)
    src = jnp.where(jj < half, 2 * jj, 2 * (jj - half) + 1)
    perm = jnp.where(ii == src, 1.0, 0.0).astype(BF16)
    for c in range(w_ref.shape[1] // PERM_W):
        w = w_ref[:, c * PERM_W:(c + 1) * PERM_W].astype(BF16)
        out = _dot(w, perm)
        g_ref[:, c * half:(c + 1) * half] = out[:, :half].astype(BF16)
        l_ref[:, c * half:(c + 1) * half] = out[:, half:].astype(BF16)


def _dispatch_kernel(pad_ref, dest_ref, x_ref, *rest, n_pad):
    xs_hbm, zero_ref, sem = rest[-3:]
    i = pl.program_id(0)
    tm = x_ref.shape[0]

    if n_pad:
        @pl.when(i == 0)
        def _():
            zero_ref[...] = jnp.zeros_like(zero_ref)

            def zbody(r, c):
                for p in range(2):
                    pltpu.make_async_copy(zero_ref.at[pl.ds(0, 1)], xs_hbm.at[pl.ds(pad_ref[2 * r + p], 1)],
                                          sem.at[1]).start(priority=p)
                return c
            lax.fori_loop(0, n_pad // 2, zbody, 0, unroll=4)
            if n_pad % 2:
                pltpu.make_async_copy(zero_ref.at[pl.ds(0, 1)], xs_hbm.at[pl.ds(pad_ref[n_pad - 1], 1)],
                                      sem.at[1]).start()
            for _ in range(n_pad // tm):
                pltpu.make_async_copy(zero_ref, xs_hbm.at[pl.ds(0, tm)], sem.at[1]).wait()
            if n_pad % tm:
                pltpu.make_async_copy(zero_ref.at[pl.ds(0, n_pad % tm)], xs_hbm.at[pl.ds(0, n_pad % tm)],
                                      sem.at[1]).wait()

    for r in range(tm):
        for kx in range(TOP_K):
            d = dest_ref[0, 0, r * TOP_K + kx]
            pltpu.make_async_copy(x_ref.at[pl.ds(r, 1)], xs_hbm.at[pl.ds(d, 1)], sem.at[0]).start(priority=kx % 2)
    for _ in range(TOP_K):
        pltpu.make_async_copy(x_ref, xs_hbm.at[pl.ds(0, tm)], sem.at[0]).wait()


def _moe_dispatch(xp, dest, pad_slots, rows, extend=None):
    N, W = xp.shape
    tm = _largest_tile(N, 256)
    n = N // tm
    n_pad = 0 if extend is not None else pad_slots.shape[0]
    in_specs = [pl.BlockSpec((1, 1, tm * TOP_K), lambda i, pad: (i, 0, 0), memory_space=pltpu.SMEM),
                pl.BlockSpec((tm, W), lambda i, pad: (i, 0))]
    operands = [pad_slots, dest.reshape(n, 1, tm * TOP_K), xp]
    aliases = {}
    if extend is not None:
        in_specs.append(pl.BlockSpec(memory_space=pl.ANY))
        operands.append(extend)
        aliases = {len(operands) - 1: 0}
    grid_spec = pltpu.PrefetchScalarGridSpec(
        num_scalar_prefetch=1,
        grid=(n,),
        in_specs=in_specs,
        out_specs=pl.BlockSpec(memory_space=pl.ANY),
        scratch_shapes=[pltpu.VMEM((tm, W), U32), pltpu.SemaphoreType.DMA((2,))],
    )
    return pl.pallas_call(
        functools.partial(_dispatch_kernel, n_pad=n_pad),
        grid_spec=grid_spec,
        out_shape=jax.ShapeDtypeStruct((rows, W), U32),
        input_output_aliases=aliases,
        compiler_params=_params("arbitrary"),
        name="moe_dispatch",
    )(*operands)


def _moe_kernel(be_ref, nused_ref, xs_ref, w1_ref, b1g_ref, b1l_ref, w2_ref, b2_ref, y_ref, w1g_s, w1l_s, w2_s):
    j = pl.program_id(0)
    nused = nused_ref[0]
    last = jnp.maximum(nused - 1, 0)
    e_now = be_ref[jnp.minimum(j, last)]
    e_before = be_ref[jnp.minimum(jnp.maximum(j - 1, 0), last)]

    @pl.when(jnp.logical_or(j == 0, e_now != e_before))
    def _():
        _deinterleave_to_bf16(w1_ref, w1g_s, w1l_s)
        w2_s[...] = w2_ref[...].astype(BF16)

    @pl.when(j < nused)
    def _():
        lo, hi = _unpack_bf16_pair(xs_ref[...])
        x = jnp.concatenate([lo, hi], axis=1).astype(BF16)
        glu = jnp.minimum(_dot(x, w1g_s[...]) + b1g_ref[...], SWIGLU_LIMIT)
        lin = jnp.clip(_dot(x, w1l_s[...]) + b1l_ref[...], -SWIGLU_LIMIT, SWIGLU_LIMIT)
        act = glu * _sigmoid(SWIGLU_ALPHA * glu) * (lin + 1.0)
        y_ref[...] = _pack_bf16_pair(_dot(act.astype(BF16), w2_s[...]) + b2_ref[...])

    @pl.when(j >= nused)
    def _():
        y_ref[...] = jnp.zeros_like(y_ref)


def _moe_experts(xs, blk_e, nused, w1, b1g, b1l, w2, b2, bm):
    rows, W = xs.shape
    nb = rows // bm
    D, F = w1.shape[1], w1.shape[2] // 2
    last = lambda j, be, nu: jnp.minimum(j, jnp.maximum(nu[0] - 1, 0))
    wspec = lambda k, n: pl.BlockSpec((None, k, n), lambda j, be, nu: (be[last(j, be, nu)], 0, 0))
    grid_spec = pltpu.PrefetchScalarGridSpec(
        num_scalar_prefetch=2,
        grid=(nb,),
        in_specs=[pl.BlockSpec((bm, W), lambda j, be, nu: (j, 0)),
                  wspec(D, 2 * F), wspec(1, F), wspec(1, F), wspec(F, D), wspec(1, D)],
        out_specs=pl.BlockSpec((bm, W), lambda j, be, nu: (j, 0)),
        scratch_shapes=[pltpu.VMEM((D, F), BF16), pltpu.VMEM((D, F), BF16), pltpu.VMEM((F, D), BF16)],
    )
    return pl.pallas_call(
        _moe_kernel,
        grid_spec=grid_spec,
        out_shape=jax.ShapeDtypeStruct((rows, W), U32),
        compiler_params=_params("arbitrary"),
        name="moe_experts",
    )(blk_e, nused, xs, w1, b1g, b1l, w2, b2)


def _combine_gather_start(dest_ref, y_hbm, buf, sem, slot, tm):
    for r in range(tm):
        for kx in range(TOP_K):
            d = dest_ref[0, 0, r * TOP_K + kx]
            pltpu.make_async_copy(y_hbm.at[pl.ds(d, 1)], buf.at[slot, kx, pl.ds(r, 1)], sem.at[slot]).start()


def _combine_kernel(dest_ref, destn_ref, gt_ref, h1_ref, g_ref, b_ref, y_hbm, o_hbm, buf, sem, obuf, osem,
                    *, dn_alpha, nt, skip):
    i = pl.program_id(0)
    n = pl.num_programs(0)
    tm = h1_ref.shape[0]
    slot = i % 2
    b = i // nt
    j = i % nt

    def out_wait(rows):
        pltpu.make_async_copy(obuf.at[0, pl.ds(0, rows)], o_hbm.at[0, pl.ds(0, rows)], osem.at[0]).wait()

    @pl.when(i == 0)
    def _():
        _combine_gather_start(dest_ref, y_hbm, buf, sem, 0, tm)

    for nxt in (0, 1):
        @pl.when(jnp.logical_and(i + 1 < n, slot == 1 - nxt))
        def _():
            _combine_gather_start(destn_ref, y_hbm, buf, sem, nxt, tm)

    for kx in range(TOP_K):
        pltpu.make_async_copy(y_hbm.at[pl.ds(0, tm)], buf.at[slot, kx], sem.at[slot]).wait()
    gt = gt_ref[...]
    lo, hi = _unpack_bf16_pair(buf[slot, 0])
    moe_lo, moe_hi = gt[:, 0:1] * lo, gt[:, 0:1] * hi
    for kx in range(1, TOP_K):
        lo, hi = _unpack_bf16_pair(buf[slot, kx])
        moe_lo = moe_lo + gt[:, kx:kx + 1] * lo
        moe_hi = moe_hi + gt[:, kx:kx + 1] * hi
    moe = jnp.concatenate([moe_lo, moe_hi], axis=1)
    obuf[slot] = _layer_norm(dn_alpha * h1_ref[...] + moe, g_ref[...], b_ref[...])

    @pl.when(jnp.logical_and(i > 0, (i - 1) % nt == 0))
    def _():
        out_wait(tm - skip)

    @pl.when(jnp.logical_and(i > 0, (i - 1) % nt != 0))
    def _():
        out_wait(tm)

    @pl.when(j == 0)
    def _():
        pltpu.make_async_copy(obuf.at[slot, pl.ds(skip, tm - skip)], o_hbm.at[b, pl.ds(0, tm - skip)],
                              osem.at[0]).start()

    @pl.when(j != 0)
    def _():
        start = pl.multiple_of(j * tm - skip, 8)
        pltpu.make_async_copy(obuf.at[slot], o_hbm.at[b, pl.ds(start, tm)], osem.at[0]).start()

    @pl.when(i == n - 1)
    def _():
        if nt == 1:
            out_wait(tm - skip)
        else:
            out_wait(tm)


def _moe_combine(dest, gate, h1, ln_g, ln_b, yb, dn_alpha, B, T, skip):
    N, D = h1.shape
    tm = _largest_tile(T, 384)
    nt = T // tm
    n = N // tm
    assert skip % 8 == 0 and skip < tm
    dest3 = dest.reshape(n, 1, tm * TOP_K)
    tile = lambda w: pl.BlockSpec((tm, w), lambda i: (i, 0))
    return pl.pallas_call(
        functools.partial(_combine_kernel, dn_alpha=dn_alpha, nt=nt, skip=skip),
        grid=(n,),
        in_specs=[
            pl.BlockSpec((1, 1, tm * TOP_K), lambda i: (i, 0, 0), memory_space=pltpu.SMEM),
            pl.BlockSpec((1, 1, tm * TOP_K), lambda i: (jnp.minimum(i + 1, n - 1), 0, 0), memory_space=pltpu.SMEM),
            tile(LANES), tile(D), _full((1, D)), _full((1, D)),
            pl.BlockSpec(memory_space=pl.ANY),
        ],
        out_specs=pl.BlockSpec(memory_space=pl.ANY),
        out_shape=jax.ShapeDtypeStruct((B, T - skip, D), F32),
        scratch_shapes=[pltpu.VMEM((2, TOP_K, tm, yb.shape[1]), U32), pltpu.SemaphoreType.DMA((2,)),
                        pltpu.VMEM((2, tm, D), F32), pltpu.SemaphoreType.DMA((1,))],
        compiler_params=_params("arbitrary"),
        name="moe_combine",
    )(dest3, dest3, gate, h1, ln_g, ln_b, yb)


MOE_BLOCK_ROWS = 512


def _route_tables(n_asg, counts, n_experts):
    bm = min(MOE_BLOCK_ROWS, max(8, 1 << int(math.log2(max(1, n_asg // n_experts)))))
    nb = -(-n_asg // bm) + n_experts
    padded = (counts + bm - 1) // bm * bm
    pends = jnp.cumsum(padded)
    starts = (pends - padded).astype(I32)
    blk_start = jnp.arange(nb, dtype=I32) * bm
    blk_e = jnp.minimum(jnp.sum(pends[None, :] <= blk_start[:, None], axis=1), n_experts - 1).astype(I32)
    nused = (pends[-1] // bm).astype(I32).reshape(1)
    n_pad = nb * bm - n_asg
    gap = padded - counts
    gap_end = jnp.cumsum(gap)
    i = jnp.arange(n_pad, dtype=I32)
    e = jnp.sum(gap_end[None, :] <= i[:, None], axis=1)
    ec = jnp.minimum(e, n_experts - 1)
    in_group = (pends - padded + counts)[ec] + i - (gap_end - gap)[ec]
    pad_slots = jnp.where(e < n_experts, in_group, pends[-1] + i - gap_end[-1]).astype(I32)
    return starts, pad_slots, blk_e, nused, bm, nb * bm


def _mixers(x, prefix, prev_row, s0, past_k, past_v, past_logf, wts):
    B, D = x.shape[0], x.shape[2]
    T = x.shape[1] + prefix.shape[0]
    h, q, k, v, rkv, gates, lmid, logf, rkv0 = _inproj(
        x, prefix, prev_row, wts['ln0_g'], wts['ln0_b'], wts['wm'], wts['wff'], wts['bff'], wts['mu3'],
        wts['w1'], wts['a1'], wts['g1'])
    if past_k is None:
        k_aug, v_bf = _fox_prep(logf, k, v, wts['sel_c'])
        fox = _fox_attention(q, k_aug, v_bf, wts['sel_q'], 0)
    else:
        P = past_k.shape[1]
        past_pad = jnp.pad(past_logf.astype(F32), ((0, 0), (0, 0), (0, LANES - N_HEADS)))
        k_aug, v_bf = _fox_prep(past_pad, past_k.reshape(B, P, HW), past_v.reshape(B, P, HW), wts['sel_c'],
                                new=(logf, k, v))
        fox = _fox_attention(q, k_aug, v_bf, wts['sel_q'], P)
    rw, s_fin = _rwkv(rkv, lmid, rkv0, s0, wts['mu_rkv'], wts['w0'], wts['w2'], wts['a0'], wts['a2'], wts['g2'],
                      wts['k_k'], wts['k_a'], wts['r_k'], wts['gn_g'], wts['gn_b'])
    N = B * T
    tokens = dict(fox=fox.reshape(N, HW), rw=rw.reshape(N, HW), gates=gates.reshape(N, 2 * D), h=h.reshape(N, D))
    new_k = k.reshape(B, T, N_HEADS, HEAD_DIM)
    new_v = v.reshape(B, T, N_HEADS, HEAD_DIM)
    return tokens, (new_k, new_v, logf[:, :, :N_HEADS], s_fin, h[:, T - 1:T, :])


def _merge_moe(streams, wts):
    dn_alpha, n_experts = wts['dn_alpha'], wts['n_experts']
    counts = jnp.zeros((1, LANES), I32)
    routed = []
    for tokens, B, T, y_skip in streams:
        h1, h1p, top_idx, gate, rank, counts = _merge_route(
            tokens['fox'], tokens['rw'], tokens['gates'], tokens['h'],
            wts['w_up_a'], wts['w_up_b'], wts['w_out'], wts['ln1_g'], wts['ln1_b'],
            wts['wr_hi'], wts['wr_lo'], wts['br'], counts, dn_alpha)
        routed.append((h1, h1p, top_idx[:, :TOP_K], gate, rank[:, :TOP_K]))
    n_asg = sum(r[0].shape[0] for r in routed) * TOP_K
    starts, pad_slots, blk_e, nused, bm, rows = _route_tables(n_asg, counts[0, :n_experts], n_experts)
    dests = [(starts[top_idx] + rank).astype(I32) for _, _, top_idx, _, rank in routed]
    zero_slots = jnp.concatenate([pad_slots] + [d.reshape(-1) for d in dests[1:]])
    xs = None
    for (h1, h1p, _, _, _), dest in zip(routed, dests):
        xs = _moe_dispatch(h1p, dest, zero_slots, rows, extend=xs)
    yb = _moe_experts(xs, blk_e, nused, wts['we1'], wts['b1g'], wts['b1l'], wts['we2'], wts['be2'], bm)
    return [_moe_combine(dest, gate, h1, wts['ln2_g'], wts['ln2_b'], yb, dn_alpha, B, T, y_skip)
            for (h1, _, _, gate, _), dest, (_, B, T, y_skip) in zip(routed, dests, streams)]


def kernel(x_prompt, x_sample, cache_fox_k, cache_fox_v, cache_fox_logf, state_rwkv, state_shift, meta, ln0_g, ln0_b, w_in, b_forget, mu_w, mu_a, mu_g, mu_rkv, w0, w1, w2, a0, a1, a2, g1, g2, k_k, k_a, r_k, gn_g, gn_b, w_up_a, w_up_b, w_out, ln1_g, ln1_b, w_router, b_router, w_e1, b_e1, w_e2, b_e2, ln2_g, ln2_b):
    depth, D, in_cols = w_in.shape
    assert depth == 1 and D == 1024 and in_cols == 6 * HW + N_HEADS + 2 * D
    n_experts = w_router.shape[2]
    assert n_experts <= LANES
    B = x_prompt.shape[0]
    l = 0
    w = w_in[l]
    off_ff = 3 * HW
    row = lambda a: a.reshape(1, -1).astype(F32)
    wr = jnp.pad(w_router[l], ((0, 0), (0, LANES - n_experts)))
    wr_hi = wr.astype(BF16)
    sel_q, sel_c = _aug_select_matrices()
    wts = dict(
        sel_q=sel_q, sel_c=sel_c,
        dn_alpha=float((2 * depth) ** 0.25), n_experts=n_experts,
        ln0_g=row(ln0_g), ln0_b=row(ln0_b),
        wm=jnp.concatenate([w[:, :off_ff], w[:, off_ff + N_HEADS:]], axis=1).astype(BF16),
        wff=jnp.pad(w[:, off_ff:off_ff + N_HEADS], ((0, 0), (0, LANES - N_HEADS))).astype(BF16),
        bff=jnp.pad(row(b_forget[l]), ((0, 0), (0, LANES - N_HEADS))),
        mu3=jnp.stack([mu_w[l], mu_a[l], mu_g[l]], axis=0),
        w1=w1[l].astype(BF16), a1=a1[l].astype(BF16), g1=g1[l].astype(BF16),
        mu_rkv=row(mu_rkv[l]), w0=row(w0[l]), w2=w2[l].astype(BF16), a0=row(a0[l]), a2=a2[l].astype(BF16),
        g2=g2[l].astype(BF16), k_k=row(k_k[l]), k_a=row(k_a[l]), r_k=row(r_k[l]), gn_g=row(gn_g[l]),
        gn_b=row(gn_b[l]),
        w_up_a=w_up_a[l].astype(BF16), w_up_b=w_up_b[l].astype(BF16), w_out=w_out[l].astype(BF16),
        ln1_g=row(ln1_g[l]), ln1_b=row(ln1_b[l]),
        wr_hi=wr_hi, wr_lo=(wr - wr_hi.astype(F32)).astype(BF16),
        br=jnp.pad(row(b_router[l]), ((0, 0), (0, LANES - n_experts)), constant_values=NEG_BIG),
        we1=w_e1[l],
        b1g=b_e1[l][:, None, 0::2], b1l=b_e1[l][:, None, 1::2],
        we2=w_e2[l], be2=b_e2[l][:, None, :],
        ln2_g=row(ln2_g[l]), ln2_b=row(ln2_b[l]),
    )
    zero_row = jnp.zeros((B, 1, D), F32)
    zero_state = jnp.zeros((B, N_HEADS, HEAD_DIM, HEAD_DIM), F32)
    no_prefix = jnp.zeros((0, D), x_sample.dtype)
    tok_p, (k_p, v_p, lf_p, s_p, sh_p) = _mixers(x_prompt, meta.astype(x_prompt.dtype), zero_row, zero_state,
                                                 None, None, None, wts)
    tok_s, (k_s, v_s, lf_s, s_s, sh_s) = _mixers(x_sample, no_prefix, state_shift[l], state_rwkv[l], cache_fox_k[l],
                                                 cache_fox_v[l], cache_fox_logf[l], wts)
    y_p, y_s = _merge_moe([(tok_p, B, x_prompt.shape[1] + N_META, N_META), (tok_s,) + x_sample.shape[:2] + (0,)],
                          wts)
    ex = lambda a: a[None]
    return (y_p, y_s, ex(k_p), ex(v_p), ex(lf_p), ex(s_p), ex(sh_p),
            ex(k_s), ex(v_s), ex(lf_s), ex(s_s), ex(sh_s))
```
